```python
import jax
import jax.numpy as jnp
from jax import lax
import numpy as np

D_MODEL = 1024
BATCH = 32
SEQ = 2048
DEPTH = 2

CTX_LEN = 256
GRID_W = 64
HEAD_DIM = 64
Q_BLOCK = 128
ROPE_THETA = 10000.0
NORM_EPS = 1e-6
NEG_INF = -1e30

A_HEADS = 8
A_KV_HEADS = 2
A_GROUP = A_HEADS // A_KV_HEADS
A_WINDOW = 128
B_HEADS = 8
NA_ROWS = 8
NA_COLS = 16
NA_QCOLS = 16
NA_KCOLS = 32
C_HEADS = 8
C_Q_LORA = 384
C_KV_LORA = 256
C_NOPE = 64
C_ROPE = 32
C_V = 64
D_HEADS = 8
D_KV_HEADS = 2
D_GROUP = D_HEADS // D_KV_HEADS
N_EXPERTS = 64
TOP_K = 6
EXPERT_FF = 256
SHARED_FF = 256
ROUTED_SCALE = 2.5

A_Q = A_HEADS * HEAD_DIM
A_KV = A_KV_HEADS * HEAD_DIM
B_QKV = B_HEADS * HEAD_DIM
EVEN_SPLITS = (A_Q, A_Q + A_KV, A_Q + 2 * A_KV, A_Q + 2 * A_KV + B_QKV, A_Q + 2 * A_KV + 2 * B_QKV)
EVEN_IN = A_Q + 2 * A_KV + 3 * B_QKV
D_Q = D_HEADS * HEAD_DIM
D_KV = D_KV_HEADS * HEAD_DIM
ODD_Q = C_Q_LORA + D_Q
ODD_KV_SPLITS = (C_KV_LORA, C_KV_LORA + C_ROPE, C_KV_LORA + C_ROPE + D_KV)
ODD_IN = ODD_Q + C_KV_LORA + C_ROPE + 2 * D_KV
MIX_OUT = A_HEADS * HEAD_DIM + B_HEADS * HEAD_DIM

kernel_name = "hybrid_dit_window_na_mla_gqa_moe"


def rms_norm(x, g):
    xf = x.astype(jnp.float32)
    y = xf * lax.rsqrt(jnp.mean(xf * xf, axis=-1, keepdims=True) + NORM_EPS)
    return (y * g.astype(jnp.float32)).astype(x.dtype)


def ada_params(cond, w_ada, b_ada):
    return jnp.split(jax.nn.silu(cond) @ w_ada + b_ada, 6, axis=-1)


def modulate(h, shift, scale):
    return h * (1 + scale) + shift


def axial_rope(n_tokens, rot_dim):
    t = jnp.arange(n_tokens, dtype=jnp.int32)
    row = (t // GRID_W).astype(jnp.float32)
    col = (t % GRID_W).astype(jnp.float32)
    n_axis = rot_dim // 4
    inv_freq = ROPE_THETA ** (-jnp.arange(n_axis, dtype=jnp.float32) / n_axis)
    ang = jnp.concatenate([row[:, None] * inv_freq, col[:, None] * inv_freq], axis=-1)
    return jnp.cos(ang), jnp.sin(ang)


def apply_rope(x, cos, sin):
    bshape = (cos.shape[0],) + (1,) * (x.ndim - 3) + (cos.shape[1],)
    cs, sn = cos.reshape(bshape), sin.reshape(bshape)
    xp = x.astype(jnp.float32).reshape(x.shape[:-1] + (x.shape[-1] // 2, 2))
    x0, x1 = xp[..., 0], xp[..., 1]
    out = jnp.stack([x0 * cs - x1 * sn, x0 * sn + x1 * cs], axis=-1)
    return out.reshape(x.shape).astype(x.dtype)


def softmax_with_sink(s, sink):
    m = jnp.maximum(jnp.max(s, axis=-1, keepdims=True), sink)
    e = jnp.exp(s - m)
    return e / (jnp.sum(e, axis=-1, keepdims=True) + jnp.exp(sink - m))


def context_attention(q, k, v, sink=None):
    s = jnp.einsum('bqhgd,bkhd->bhgqk', q, k).astype(jnp.float32) * q.shape[-1] ** -0.5
    if sink is None:
        p = jax.nn.softmax(s, axis=-1)
    else:
        p = softmax_with_sink(s, sink[None, :, :, None, None].astype(jnp.float32))
    return jnp.einsum('bhgqk,bkhd->bqhgd', p.astype(v.dtype), v)


def sweep_attention(q, k, v, scale):
    B, Sq, Hk, G, dq = q.shape
    nb = Sq // Q_BLOCK
    qb = jnp.moveaxis(q.reshape(B, nb, Q_BLOCK, Hk, G, dq), 1, 0)

    def one_block(qi):
        s = jnp.einsum('bqhgd,bkhd->bhgqk', qi, k).astype(jnp.float32) * scale
        p = jax.nn.softmax(s, axis=-1).astype(v.dtype)
        return jnp.einsum('bhgqk,bkhd->bqhgd', p, v)

    out = lax.map(one_block, qb)
    return jnp.moveaxis(out, 0, 1).reshape(B, Sq, Hk, G, v.shape[-1])


def window_attention_sink(q, k, v, kc, vc, sink):
    B, S, Hk, G, d = q.shape
    nb = S // Q_BLOCK
    span = Q_BLOCK + 2 * A_WINDOW
    n_ctx = kc.shape[1]
    pad = ((0, 0), (A_WINDOW, A_WINDOW), (0, 0), (0, 0))
    kp, vp = jnp.pad(k, pad), jnp.pad(v, pad)
    qb = jnp.moveaxis(q.reshape(B, nb, Q_BLOCK, Hk, G, d), 1, 0)
    band = jnp.abs(jnp.arange(Q_BLOCK)[:, None] - jnp.arange(span)[None, :] + A_WINDOW) <= A_WINDOW
    sink_b = sink[None, :, :, None, None].astype(jnp.float32)
    scale = d ** -0.5

    def one_block(args):
        n, qi = args
        start = n * Q_BLOCK
        kb = lax.dynamic_slice_in_dim(kp, start, span, axis=1)
        vb = lax.dynamic_slice_in_dim(vp, start, span, axis=1)
        kpos = start + jnp.arange(span) - A_WINDOW
        valid = band & ((kpos >= 0) & (kpos < S))[None, :]
        s_loc = jnp.einsum('bqhgd,bkhd->bhgqk', qi, kb).astype(jnp.float32) * scale
        s_loc = jnp.where(valid, s_loc, NEG_INF)
        s_ctx = jnp.einsum('bqhgd,bkhd->bhgqk', qi, kc).astype(jnp.float32) * scale
        p = softmax_with_sink(jnp.concatenate([s_ctx, s_loc], axis=-1), sink_b).astype(v.dtype)
        return (jnp.einsum('bhgqk,bkhd->bqhgd', p[..., :n_ctx], vc)
                + jnp.einsum('bhgqk,bkhd->bqhgd', p[..., n_ctx:], vb))

    out = lax.map(one_block, (jnp.arange(nb), qb))
    return jnp.moveaxis(out, 0, 1).reshape(B, S, Hk * G * d)


def neighbourhood_attention(q, k, v, kc, vc, rpb):
    B, S, H, d = q.shape
    rows = S // GRID_W
    kr = min(NA_ROWS, rows)
    n_cb = GRID_W // NA_QCOLS
    n_ctx = kc.shape[1]
    qcol = np.arange(GRID_W).reshape(n_cb, NA_QCOLS)
    kcol = np.clip(qcol[:, :1] - NA_COLS // 2, 0, GRID_W - NA_KCOLS) + np.arange(NA_KCOLS)
    cstart = np.clip(qcol - NA_COLS // 2, 0, GRID_W - NA_COLS)[:, :, None]
    kcb = kcol[:, None, :]
    col_ok = (kcb >= cstart) & (kcb < cstart + NA_COLS)
    dc_idx = np.clip(kcb - qcol[:, :, None] + NA_COLS - 1, 0, 2 * NA_COLS - 2)
    rpb_c = rpb[:, :, dc_idx]
    qg = jnp.moveaxis(q.reshape(B, rows, n_cb, NA_QCOLS, H, d), 1, 0)
    kg = k.reshape(B, rows, GRID_W, H, d)
    vg = v.reshape(B, rows, GRID_W, H, d)
    mask = col_ok[:, None, :, None, :]
    scale = d ** -0.5

    def one_row(args):
        r, qr = args
        rs = jnp.clip(r - kr // 2, 0, rows - kr)
        kb = lax.dynamic_slice_in_dim(kg, rs, kr, axis=1)[:, :, kcol]
        vb = lax.dynamic_slice_in_dim(vg, rs, kr, axis=1)[:, :, kcol]
        dr_idx = rs + jnp.arange(kr) - r + NA_ROWS - 1
        bias = jnp.transpose(rpb_c[:, dr_idx], (2, 0, 3, 1, 4)).astype(jnp.float32)
        s_loc = jnp.einsum('bcqhd,brcjhd->bchqrj', qr, kb).astype(jnp.float32) * scale + bias
        s_loc = jnp.where(mask, s_loc, NEG_INF).reshape(B, n_cb, H, NA_QCOLS, kr * NA_KCOLS)
        s_ctx = jnp.einsum('bcqhd,bkhd->bchqk', qr, kc).astype(jnp.float32) * scale
        p = jax.nn.softmax(jnp.concatenate([s_ctx, s_loc], axis=-1), axis=-1).astype(v.dtype)
        p_loc = p[..., n_ctx:].reshape(B, n_cb, H, NA_QCOLS, kr, NA_KCOLS)
        return (jnp.einsum('bchqk,bkhd->bcqhd', p[..., :n_ctx], vc)
                + jnp.einsum('bchqrj,brcjhd->bcqhd', p_loc, vb))

    out = lax.map(one_row, (jnp.arange(rows), qg))
    return jnp.moveaxis(out, 0, 1).reshape(B, S, H * d)


def even_mixer(h, hc, w_in, sink, rpb, w_out, cos, sin):
    def heads(p):
        qa, ka, va, qb, kb, vb = jnp.split(p, EVEN_SPLITS, axis=-1)
        lead = p.shape[:-1]
        return (qa.reshape(lead + (A_KV_HEADS, A_GROUP, HEAD_DIM)), ka.reshape(lead + (A_KV_HEADS, HEAD_DIM)),
                va.reshape(lead + (A_KV_HEADS, HEAD_DIM)), qb.reshape(lead + (B_HEADS, HEAD_DIM)),
                kb.reshape(lead + (B_HEADS, HEAD_DIM)), vb.reshape(lead + (B_HEADS, HEAD_DIM)))

    B, S = h.shape[:2]
    n_ctx = hc.shape[1]
    qa, ka, va, qb, kb, vb = heads(h @ w_in)
    qa_c, ka_c, va_c, qb_c, kb_c, vb_c = heads(hc @ w_in)
    qa, ka = apply_rope(qa, cos, sin), apply_rope(ka, cos, sin)
    sink_hg = sink.reshape(A_KV_HEADS, A_GROUP)
    ya = window_attention_sink(qa, ka, va, ka_c, va_c, sink_hg)
    yb = neighbourhood_attention(qb, kb, vb, kb_c, vb_c, rpb)
    y = jnp.concatenate([ya, yb], axis=-1) @ w_out
    ya_c = context_attention(qa_c, ka_c, va_c, sink_hg).reshape(B, n_ctx, -1)
    yb_c = context_attention(qb_c[:, :, :, None, :], kb_c, vb_c).reshape(B, n_ctx, -1)
    y_c = jnp.concatenate([ya_c, yb_c], axis=-1) @ w_out
    return y, y_c


def odd_mixer(h, hc, w_in, g_q_lora, w_uq, g_kv_lora, w_ukv, g_qn, g_kn, w_out, cos_h, sin_h, cos_r, sin_r):
    B, S = h.shape[:2]

    def kv_heads(pkv):
        ckv, k_rope, kd, vd = jnp.split(pkv, ODD_KV_SPLITS, axis=-1)
        lead = pkv.shape[:-1]
        kv = (rms_norm(ckv, g_kv_lora) @ w_ukv).reshape(lead + (C_HEADS, C_NOPE + C_V))
        kd = rms_norm(kd.reshape(lead + (D_KV_HEADS, HEAD_DIM)), g_kn)
        return kv[..., :C_NOPE], kv[..., C_NOPE:], k_rope, kd, vd.reshape(lead + (D_KV_HEADS, HEAD_DIM))

    def mla_keys(k_nope, k_rope):
        return jnp.concatenate([k_nope, jnp.broadcast_to(k_rope[..., None, :], k_nope.shape[:-1] + (C_ROPE,))], axis=-1)

    p = h @ w_in
    cq, qd = jnp.split(p[..., :ODD_Q], (C_Q_LORA,), axis=-1)
    k_nope, v_m, k_rope, kd, vd = kv_heads(p[..., ODD_Q:])
    k_nope_c, v_m_c, k_rope_c, kd_c, vd_c = kv_heads(hc @ w_in[:, ODD_Q:])
    qm = (rms_norm(cq, g_q_lora) @ w_uq).reshape(B, S, C_HEADS, C_NOPE + C_ROPE)
    qm = jnp.concatenate([qm[..., :C_NOPE], apply_rope(qm[..., C_NOPE:], cos_r, sin_r)], axis=-1)
    k_rope = apply_rope(k_rope, cos_r, sin_r)
    km = jnp.concatenate([mla_keys(k_nope_c, k_rope_c), mla_keys(k_nope, k_rope)], axis=1)
    vm = jnp.concatenate([v_m_c, v_m], axis=1)
    ym = sweep_attention(qm[:, :, :, None, :], km, vm, (C_NOPE + C_ROPE) ** -0.5).reshape(B, S, -1)
    qd = apply_rope(rms_norm(qd.reshape(B, S, D_HEADS, HEAD_DIM), g_qn), cos_h, sin_h)
    qd = qd.reshape(B, S, D_KV_HEADS, D_GROUP, HEAD_DIM)
    kd = apply_rope(kd, cos_h, sin_h)
    yd = sweep_attention(qd, jnp.concatenate([kd_c, kd], axis=1), jnp.concatenate([vd_c, vd], axis=1),
                         HEAD_DIM ** -0.5).reshape(B, S, -1)
    return jnp.concatenate([ym, yd], axis=-1) @ w_out


def moe_ffn(h, w_router, b_router, w_gate, w_up, w_down, ws_gate, ws_up, ws_down):
    def per_sample(ht):
        scores = jax.nn.sigmoid((ht @ w_router).astype(jnp.float32))
        _, idx = lax.top_k(scores + b_router.astype(jnp.float32), TOP_K)
        sel = jnp.take_along_axis(scores, idx, axis=-1)
        wts = ROUTED_SCALE * sel / jnp.sum(sel, axis=-1, keepdims=True)
        combine = jnp.einsum('tk,tke->te', wts, jax.nn.one_hot(idx, N_EXPERTS, dtype=jnp.float32)).astype(ht.dtype)
        act = jax.nn.silu(jnp.einsum('td,edf->tef', ht, w_gate)) * jnp.einsum('td,edf->tef', ht, w_up)
        routed = jnp.einsum('tef,te,efd->td', act, combine, w_down)
        shared = (jax.nn.silu(ht @ ws_gate) * (ht @ ws_up)) @ ws_down
        return routed + shared
    return lax.map(per_sample, h)


def setup_inputs(seed: int = 0) -> dict:
    key = jax.random.key(seed)
    keys = iter(jax.random.split(key, 64))

    def normal(shape, scale):
        return jax.random.normal(next(keys), shape, jnp.float32) * scale

    def gain(n):
        return 1.0 + normal((n,), 0.01)

    inp = {}
    inp["x"] = normal((BATCH, SEQ, D_MODEL), 1.0)
    inp["c"] = normal((BATCH, D_MODEL), 1.0)
    inp["ctx"] = normal((BATCH, CTX_LEN, D_MODEL), 1.0)
    inp["c_ctx"] = normal((D_MODEL,), 1.0)

    def ada(p):
        inp[p + "w_ada"] = normal((D_MODEL, 6 * D_MODEL), 0.5 * D_MODEL ** -0.5)
        inp[p + "b_ada"] = normal((6 * D_MODEL,), 0.01)
        inp[p + "g_mix"] = gain(D_MODEL)

    def moe(p):
        inp[p + "g_moe"] = gain(D_MODEL)
        inp[p + "w_router"] = normal((D_MODEL, N_EXPERTS), D_MODEL ** -0.5)
        inp[p + "b_router"] = normal((N_EXPERTS,), 0.01)
        inp[p + "w_gate"] = normal((N_EXPERTS, D_MODEL, EXPERT_FF), D_MODEL ** -0.5)
        inp[p + "w_up"] = normal((N_EXPERTS, D_MODEL, EXPERT_FF), D_MODEL ** -0.5)
        inp[p + "w_down"] = normal((N_EXPERTS, EXPERT_FF, D_MODEL), EXPERT_FF ** -0.5)
        inp[p + "ws_gate"] = normal((D_MODEL, SHARED_FF), D_MODEL ** -0.5)
        inp[p + "ws_up"] = normal((D_MODEL, SHARED_FF), D_MODEL ** -0.5)
        inp[p + "ws_down"] = normal((SHARED_FF, D_MODEL), SHARED_FF ** -0.5)

    ada("l0_")
    inp["l0_w_in"] = normal((D_MODEL, EVEN_IN), D_MODEL ** -0.5)
    inp["l0_sink"] = normal((A_HEADS,), 0.5)
    inp["l0_rpb"] = normal((B_HEADS, 2 * NA_ROWS - 1, 2 * NA_COLS - 1), 0.1)
    inp["l0_w_out"] = normal((MIX_OUT, D_MODEL), MIX_OUT ** -0.5)
    moe("l0_")
    ada("l1_")
    inp["l1_w_in"] = normal((D_MODEL, ODD_IN), D_MODEL ** -0.5)
    inp["l1_g_q_lora"] = gain(C_Q_LORA)
    inp["l1_w_uq"] = normal((C_Q_LORA, C_HEADS * (C_NOPE + C_ROPE)), C_Q_LORA ** -0.5)
    inp["l1_g_kv_lora"] = gain(C_KV_LORA)
    inp["l1_w_ukv"] = normal((C_KV_LORA, C_HEADS * (C_NOPE + C_V)), C_KV_LORA ** -0.5)
    inp["l1_g_qn"] = gain(HEAD_DIM)
    inp["l1_g_kn"] = gain(HEAD_DIM)
    inp["l1_w_out"] = normal((MIX_OUT, D_MODEL), MIX_OUT ** -0.5)
    moe("l1_")
    inp["g_final"] = gain(D_MODEL)
    return inp


def reference(x, c, ctx, c_ctx,
              l0_w_ada, l0_b_ada, l0_g_mix, l0_w_in, l0_sink, l0_rpb, l0_w_out,
              l0_g_moe, l0_w_router, l0_b_router, l0_w_gate, l0_w_up, l0_w_down, l0_ws_gate, l0_ws_up, l0_ws_down,
              l1_w_ada, l1_b_ada, l1_g_mix, l1_w_in, l1_g_q_lora, l1_w_uq, l1_g_kv_lora, l1_w_ukv, l1_g_qn, l1_g_kn,
              l1_w_out,
              l1_g_moe, l1_w_router, l1_b_router, l1_w_gate, l1_w_up, l1_w_down, l1_ws_gate, l1_ws_up, l1_ws_down,
              g_final):
    S = x.shape[1]
    n_ctx = ctx.shape[1]
    cos_h, sin_h = axial_rope(S, HEAD_DIM)
    cos_r, sin_r = axial_rope(S, C_ROPE)
    ada = ((l0_w_ada, l0_b_ada), (l1_w_ada, l1_b_ada))
    g_mix = (l0_g_mix, l1_g_mix)
    g_moe = (l0_g_moe, l1_g_moe)
    moe_w = ((l0_w_router, l0_b_router, l0_w_gate, l0_w_up, l0_w_down, l0_ws_gate, l0_ws_up, l0_ws_down),
             (l1_w_router, l1_b_router, l1_w_gate, l1_w_up, l1_w_down, l1_ws_gate, l1_ws_up, l1_ws_down))
    mix_w = ((l0_w_in, l0_sink, l0_rpb, l0_w_out),
             (l1_w_in, l1_g_q_lora, l1_w_uq, l1_g_kv_lora, l1_w_ukv, l1_g_qn, l1_g_kn, l1_w_out))

    for layer in range(DEPTH):
        last = layer == DEPTH - 1
        sh1, sc1, gt1, sh2, sc2, gt2 = [m[:, None, :] for m in ada_params(c, *ada[layer])]
        sh1c, sc1c, gt1c, sh2c, sc2c, gt2c = ada_params(c_ctx, *ada[layer])
        h = modulate(rms_norm(x, g_mix[layer]), sh1, sc1)
        hc = modulate(rms_norm(ctx, g_mix[layer]), sh1c, sc1c)
        if layer % 2 == 0:
            y, y_c = even_mixer(h, hc, *mix_w[layer], cos_h, sin_h)
            ctx = ctx + gt1c * y_c
        else:
            y = odd_mixer(h, hc, *mix_w[layer], cos_h, sin_h, cos_r, sin_r)
        x = x + gt1 * y
        h = modulate(rms_norm(x, g_moe[layer]), sh2, sc2)
        if last:
            x = x + gt2 * moe_ffn(h, *moe_w[layer])
        else:
            hc = modulate(rms_norm(ctx, g_moe[layer]), sh2c, sc2c)
            y_all = moe_ffn(jnp.concatenate([hc, h], axis=1), *moe_w[layer])
            ctx = ctx + gt2c * y_all[:, :n_ctx]
            x = x + gt2 * y_all[:, n_ctx:]
    return rms_norm(x, g_final)
```

```python
import functools

import numpy as np
import jax
import jax.numpy as jnp
from jax import lax
from jax.experimental import pallas as pl
from jax.experimental.pallas import tpu as pltpu

F32 = jnp.float32
BF16 = jnp.bfloat16
U32 = jnp.uint32
I32 = jnp.int32

D_MODEL = 1024
CTX_LEN = 256
GRID_W = 64
HEAD_DIM = 64
ROPE_THETA = 10000.0
NORM_EPS = 1e-6
NEG_INF = -1e30
A_WINDOW = 128
NA_ROWS = 8
NA_COLS = 16
C_Q_LORA = 384
C_KV_LORA = 256
C_NOPE = 64
C_ROPE = 32
C_V = 64
N_EXPERTS = 64
TOP_K = 6
EXPERT_FF = 256
SHARED_FF = 256
ROUTED_SCALE = 2.5

LANES = 128
TOK_TILE = 256
SLOT_TILE = 256
KPAD = 8
VMEM_LIMIT = 48 * 1024 * 1024


def _cparams(sem):
    return pltpu.CompilerParams(dimension_semantics=sem, vmem_limit_bytes=VMEM_LIMIT)


def _rms(x, g):
    return x * lax.rsqrt(jnp.mean(x * x, axis=-1, keepdims=True) + NORM_EPS) * g


def _norm_mod(x, g, shift, scale):
    return _rms(x, g) * (1.0 + scale) + shift


def _rope128(x, cos, sin):
    lane = lax.broadcasted_iota(I32, x.shape, 1)
    swapped = jnp.where(lane % 2 == 0, pltpu.roll(x, LANES - 1, 1), pltpu.roll(x, 1, 1))
    return x * cos + swapped * sin


def _group_sumsq(x, ones_bd):
    sq = x * x
    hi = sq.astype(BF16)
    lo = (sq - hi.astype(F32)).astype(BF16)
    return (jnp.dot(hi, ones_bd, preferred_element_type=F32) + jnp.dot(lo, ones_bd, preferred_element_type=F32))


def _head_norm(x, ones_bd, g):
    return x * lax.rsqrt(_group_sumsq(x, ones_bd) * (1.0 / HEAD_DIM) + NORM_EPS) * g


def _pack_bf16_pair(lo, hi):
    lo_bits = lax.bitcast_convert_type(lo.astype(BF16).astype(F32), U32)
    hi_bits = lax.bitcast_convert_type(hi.astype(BF16).astype(F32), U32)
    return (hi_bits & jnp.uint32(0xFFFF0000)) | (lo_bits >> 16)


def _unpack_bf16_pair(u):
    lo = lax.bitcast_convert_type(u << 16, F32)
    hi = lax.bitcast_convert_type(u & jnp.uint32(0xFFFF0000), F32)
    return lo, hi


def _keep_half(q128, half):
    lane = lax.broadcasted_iota(I32, q128.shape, 1)
    keep = (lane < HEAD_DIM) if half == 0 else (lane >= HEAD_DIM)
    return jnp.where(keep, q128, jnp.zeros_like(q128))


def _merge_halves(o_even, o_odd):
    lane = lax.broadcasted_iota(I32, o_even.shape, 1)
    return jnp.where(lane < HEAD_DIM, o_even, o_odd)


def _qk(q, k):
    return lax.dot_general(q, k, (((1,), (1,)), ((), ())), preferred_element_type=F32)


def _softmax_pv(s, v, sink=None):
    m = jnp.max(s, axis=-1, keepdims=True)
    if sink is not None:
        m = jnp.maximum(m, sink)
    e = jnp.exp(s - m)
    den = jnp.sum(e, axis=-1, keepdims=True)
    if sink is not None:
        den = den + jnp.exp(sink - m)
    return jnp.dot(e.astype(BF16), v, preferred_element_type=F32) / den


def _ada_kernel(c_ref, w_ref, b_ref, o_ref):
    c = c_ref[...]
    a = c * jax.nn.sigmoid(c)
    o_ref[...] = jnp.dot(a, w_ref[...], precision=lax.Precision.HIGHEST, preferred_element_type=F32) + b_ref[...]


def _ada(cond, w_ada, b_ada):
    n, d = cond.shape
    nout = w_ada.shape[1]
    bn = 512
    return pl.pallas_call(
        _ada_kernel,
        grid=(nout // bn,),
        in_specs=[pl.BlockSpec((n, d), lambda j: (0, 0)),
                  pl.BlockSpec((d, bn), lambda j: (0, j)),
                  pl.BlockSpec((1, bn), lambda j: (0, j))],
        out_specs=pl.BlockSpec((n, bn), lambda j: (0, j)),
        out_shape=jax.ShapeDtypeStruct((n, nout), F32),
        compiler_params=_cparams(("arbitrary",)),
        name="ada",
    )(cond, w_ada, b_ada.reshape(1, nout))


def _mods(c, c_ctx, w_ada, b_ada):
    b = c.shape[0]
    rows = ((b + 1 + 7) // 8) * 8
    cond = jnp.zeros((rows, D_MODEL), F32).at[:b].set(c).at[b].set(c_ctx)
    out = _ada(cond, w_ada, b_ada)
    lat = out[:b].reshape(b, 1, 6, D_MODEL)
    cx = jnp.broadcast_to(out[b].reshape(1, 1, 6, D_MODEL), (b, 1, 6, D_MODEL))
    return jnp.concatenate([cx, lat], axis=1)


def _axial_angles(n_tokens, rot_dim):
    t = np.arange(n_tokens)
    row = (t // GRID_W).astype(np.float32)
    col = (t % GRID_W).astype(np.float32)
    n_axis = rot_dim // 4
    inv_freq = (np.float32(ROPE_THETA) ** (-np.arange(n_axis, dtype=np.float32) / n_axis)).astype(np.float32)
    return jnp.concatenate([jnp.asarray(row[:, None] * inv_freq), jnp.asarray(col[:, None] * inv_freq)], axis=-1)


def _rope_tables(seq, rot_dim, lane_start, period, scale):
    ang = _axial_angles(seq, rot_dim)
    cos = jnp.repeat(jnp.cos(ang), 2, axis=-1)
    sin = jnp.repeat(jnp.sin(ang), 2, axis=-1) * jnp.tile(jnp.asarray([-1.0, 1.0], F32), rot_dim // 2)
    cos_p = jnp.ones((seq, period), F32).at[:, lane_start:lane_start + rot_dim].set(cos)
    sin_p = jnp.zeros((seq, period), F32).at[:, lane_start:lane_start + rot_dim].set(sin)
    cos_f = jnp.concatenate([jnp.ones((CTX_LEN, period), F32), cos_p], axis=0)
    sin_f = jnp.concatenate([jnp.zeros((CTX_LEN, period), F32), sin_p], axis=0)
    reps = LANES // period
    return jnp.tile(cos_f, (1, reps)) * scale, jnp.tile(sin_f, (1, reps)) * scale


def _na_bias_table(rpb):
    h = rpb.shape[0]
    qc = np.arange(GRID_W)
    kc = np.arange(GRID_W)
    cstart = np.clip(qc - NA_COLS // 2, 0, GRID_W - NA_COLS)
    valid = (kc[None, :] >= cstart[:, None]) & (kc[None, :] < cstart[:, None] + NA_COLS)
    dc = np.clip(kc[None, :] - qc[:, None] + NA_COLS - 1, 0, 2 * NA_COLS - 2)
    tabs = []
    for v in range(NA_ROWS):
        dr = np.clip(np.arange(NA_ROWS) - v + NA_ROWS - 1, 0, 2 * NA_ROWS - 2)
        b = rpb[:, dr][:, :, dc]
        b = jnp.where(jnp.asarray(valid)[None, None], b.astype(F32), NEG_INF)
        tabs.append(jnp.transpose(b, (0, 2, 1, 3)).reshape(h, GRID_W, NA_ROWS * GRID_W))
    tabs.append(jnp.full((h, GRID_W, NA_ROWS * GRID_W), NEG_INF, F32))
    loc = jnp.stack(tabs, axis=0)
    return jnp.concatenate([jnp.zeros(loc.shape[:3] + (CTX_LEN,), F32), loc], axis=-1)


def _dup_heads(w, n_heads):
    d = w.shape[0]
    w = w.reshape(d, n_heads, 1, HEAD_DIM)
    return jnp.broadcast_to(w, (d, n_heads, 2, HEAD_DIM)).reshape(d, n_heads * 2 * HEAD_DIM)


def _proj0_kernel(x_ref, mod_ref, g_ref, w_ref, cos_ref, sin_ref, qa_ref, kva_ref, qb_ref, kb_ref, vb_ref):
    h = _norm_mod(x_ref[0], g_ref[...], mod_ref[0, 0, 0:1, :], mod_ref[0, 0, 1:2, :])
    r = jnp.dot(h.astype(BF16), w_ref[...], preferred_element_type=F32)
    cos, sin = cos_ref[...], sin_ref[...]
    roped = [_rope128(r[:, i * LANES:(i + 1) * LANES], cos, sin) for i in range(6)]
    qa_ref[0] = jnp.concatenate(roped[0:4], axis=1).astype(BF16)
    kva_ref[0] = jnp.concatenate(roped[4:6] + [r[:, 768:1024]], axis=1).astype(BF16)
    qb_ref[0] = r[:, 1024:1536].astype(BF16)
    kb_ref[0] = r[:, 1536:2048].astype(BF16)
    vb_ref[0] = r[:, 2048:2560].astype(BF16)


def _proj0(xa, mods, g_mix, w_in, cos, sin):
    b, u, d = xa.shape
    tm = TOK_TILE
    s = 1.0 / 8.0
    w = jnp.concatenate([w_in[:, 0:512] * s, _dup_heads(w_in[:, 512:640], 2), _dup_heads(w_in[:, 640:768], 2),
                         w_in[:, 768:1280] * s, w_in[:, 1280:1792], w_in[:, 1792:2304]], axis=1).astype(BF16)
    nw = w.shape[1]
    tok = lambda n: pl.BlockSpec((1, tm, n), lambda i, t: (i, t, 0))
    return pl.pallas_call(
        _proj0_kernel,
        grid=(b, u // tm),
        in_specs=[tok(d),
                  pl.BlockSpec((1, 1, 6, d), lambda i, t: (i, jnp.minimum(t, 1), 0, 0)),
                  pl.BlockSpec((1, d), lambda i, t: (0, 0)),
                  pl.BlockSpec((d, nw), lambda i, t: (0, 0)),
                  pl.BlockSpec((tm, LANES), lambda i, t: (t, 0)),
                  pl.BlockSpec((tm, LANES), lambda i, t: (t, 0))],
        out_specs=[tok(512)] * 5,
        out_shape=[jax.ShapeDtypeStruct((b, u, 512), BF16)] * 5,
        compiler_params=_cparams(("arbitrary", "arbitrary")),
        name="proj0",
    )(xa, mods, g_mix.reshape(1, d), w, cos, sin)


def _window_kernel(sink_ref, q_ref, kv_ref, o_ref, *, tq, span):
    n = pl.program_id(1)
    u = kv_ref.shape[1]
    ls = pl.multiple_of(jnp.clip((n - 1) * tq, CTX_LEN, u - span), tq)
    qpos = n * tq + lax.broadcasted_iota(I32, (tq, span), 0)
    kpos = ls + lax.broadcasted_iota(I32, (tq, span), 1)
    valid = (kpos >= CTX_LEN) & (jnp.abs(qpos - kpos) <= A_WINDOW) & (qpos >= CTX_LEN)
    kv_c = kv_ref[0, 0:CTX_LEN, :]
    kv_l = kv_ref[0, pl.ds(ls, span), :]
    outs = [None] * 8
    for g in range(2):
        kc = kv_c[:, g * LANES:(g + 1) * LANES]
        kl = kv_l[:, g * LANES:(g + 1) * LANES]
        v_all = jnp.concatenate([kv_c[:, 256 + g * LANES:256 + (g + 1) * LANES],
                                 kv_l[:, 256 + g * LANES:256 + (g + 1) * LANES]], axis=0)
        for hh in range(4):
            h = 4 * g + hh
            q = _keep_half(q_ref[0, :, (h // 2) * LANES:(h // 2 + 1) * LANES], h % 2)
            s_l = jnp.where(valid, _qk(q, kl), NEG_INF)
            s = jnp.concatenate([_qk(q, kc), s_l], axis=1)
            outs[h] = _softmax_pv(s, v_all, sink_ref[h])
    o_ref[0] = jnp.concatenate([_merge_halves(outs[2 * p], outs[2 * p + 1]) for p in range(4)], axis=1).astype(BF16)


def _window_attention(qa, kva, sink):
    b, u, _ = qa.shape
    tq = A_WINDOW
    span = 3 * A_WINDOW
    return pl.pallas_call(
        functools.partial(_window_kernel, tq=tq, span=span),
        grid=(b, u // tq),
        in_specs=[pl.BlockSpec(memory_space=pltpu.SMEM),
                  pl.BlockSpec((1, tq, 512), lambda i, n: (i, n, 0)),
                  pl.BlockSpec((1, u, 512), lambda i, n: (i, 0, 0))],
        out_specs=pl.BlockSpec((1, tq, 512), lambda i, n: (i, n, 0)),
        out_shape=jax.ShapeDtypeStruct((b, u, 512), BF16),
        compiler_params=_cparams(("arbitrary", "arbitrary")),
        name="window_attn",
    )(sink.astype(F32), qa, kva)


def _na_variant(j, rows):
    r = jnp.maximum(j - CTX_LEN // GRID_W, 0)
    rs = jnp.clip(r - NA_ROWS // 2, 0, rows - NA_ROWS)
    return r, rs


def _na_kernel(q_ref, k_ref, v_ref, bias_ref, o_ref, *, rows):
    j = pl.program_id(1)
    _, rs = _na_variant(j, rows)
    band = NA_ROWS * GRID_W
    start = pl.multiple_of(CTX_LEN + rs * GRID_W, GRID_W)
    k_all = jnp.concatenate([k_ref[0, 0:CTX_LEN, :], k_ref[0, pl.ds(start, band), :]], axis=0)
    v_all = jnp.concatenate([v_ref[0, 0:CTX_LEN, :], v_ref[0, pl.ds(start, band), :]], axis=0)
    outs = []
    for p in range(4):
        q128 = q_ref[0, :, p * LANES:(p + 1) * LANES]
        k128 = k_all[:, p * LANES:(p + 1) * LANES]
        v128 = v_all[:, p * LANES:(p + 1) * LANES]
        pair = []
        for half in range(2):
            s = _qk(_keep_half(q128, half), k128) + bias_ref[0, 2 * p + half]
            pair.append(_softmax_pv(s, v128))
        outs.append(_merge_halves(pair[0], pair[1]))
    o_ref[0] = jnp.concatenate(outs, axis=1).astype(BF16)


def _na_attention(qb, kb, vb, bias_tab):
    b, u, _ = qb.shape
    rows = (u - CTX_LEN) // GRID_W
    assert rows >= NA_ROWS
    nk = bias_tab.shape[-1]

    def bias_map(i, j):
        r, rs = _na_variant(j, rows)
        return (jnp.where(j < CTX_LEN // GRID_W, NA_ROWS, r - rs), 0, 0, 0)

    return pl.pallas_call(
        functools.partial(_na_kernel, rows=rows),
        grid=(b, u // GRID_W),
        in_specs=[pl.BlockSpec((1, GRID_W, 512), lambda i, j: (i, j, 0)),
                  pl.BlockSpec((1, u, 512), lambda i, j: (i, 0, 0)),
                  pl.BlockSpec((1, u, 512), lambda i, j: (i, 0, 0)),
                  pl.BlockSpec((1, 8, GRID_W, nk), bias_map)],
        out_specs=pl.BlockSpec((1, GRID_W, 512), lambda i, j: (i, j, 0)),
        out_shape=jax.ShapeDtypeStruct((b, u, 512), BF16),
        compiler_params=_cparams(("arbitrary", "arbitrary")),
        name="na_attn",
    )(qb, kb, vb, bias_tab)


def _proj1_kernel(x_ref, mod_ref, g_ref, w_ref, gq_ref, wuq_ref, gkv_ref, wkv_ref, gqn_ref, gkn_ref, ones_ref,
                  cosa_ref, sina_ref, cosq_ref, sinq_ref, cosk_ref, sink_ref,
                  qm_ref, qd_ref, km_ref, vm_ref, kvd_ref):
    h = _norm_mod(x_ref[0], g_ref[...], mod_ref[0, 0, 0:1, :], mod_ref[0, 0, 1:2, :])
    r = jnp.dot(h.astype(BF16), w_ref[...], preferred_element_type=F32)
    ones_bd = ones_ref[...]
    cosa, sina = cosa_ref[...], sina_ref[...]
    cq = _rms(r[:, 0:C_Q_LORA], gq_ref[...])
    qm = jnp.dot(cq.astype(BF16), wuq_ref[...], preferred_element_type=F32)
    cosq, sinq = cosq_ref[...], sinq_ref[...]
    qm_ref[0] = jnp.concatenate([_rope128(qm[:, i * LANES:(i + 1) * LANES], cosq, sinq) for i in range(8)],
                                axis=1).astype(BF16)
    gqn = gqn_ref[...]
    qd_ref[0] = jnp.concatenate(
        [_rope128(_head_norm(r[:, 384 + i * LANES:384 + (i + 1) * LANES], ones_bd, gqn), cosa, sina)
         for i in range(4)], axis=1).astype(BF16)
    ckv = _rms(r[:, 896:1152], gkv_ref[...])
    kr = _rope128(r[:, 1152:1280], cosk_ref[...], sink_ref[...])
    kv = jnp.dot(jnp.concatenate([ckv, kr], axis=1).astype(BF16), wkv_ref[...], preferred_element_type=F32)
    km_ref[0] = kv[:, 0:1024].astype(BF16)
    vm_ref[0] = kv[:, 1024:1536].astype(BF16)
    gkn = gkn_ref[...]
    kd = [_rope128(_head_norm(r[:, 1280 + i * LANES:1280 + (i + 1) * LANES], ones_bd, gkn), cosa, sina)
          for i in range(2)]
    kvd_ref[0] = jnp.concatenate(kd + [r[:, 1536:1792]], axis=1).astype(BF16)


def _proj1(xa, mods, g_mix, w_in, g_q_lora, w_uq, g_kv_lora, w_ukv, g_qn, g_kn, tabs):
    b, u, d = xa.shape
    tm = TOK_TILE
    zpad = jnp.zeros((d, LANES - C_ROPE), F32)
    w = jnp.concatenate([w_in[:, 0:896], w_in[:, 896:1152], w_in[:, 1152:1184], zpad,
                         _dup_heads(w_in[:, 1184:1312], 2), _dup_heads(w_in[:, 1312:1440], 2)], axis=1).astype(BF16)
    nw = w.shape[1]
    wuq = w_uq.reshape(C_Q_LORA, 8, C_NOPE + C_ROPE)
    wuq = jnp.concatenate([wuq, jnp.zeros((C_Q_LORA, 8, LANES - C_NOPE - C_ROPE), F32)], axis=-1)
    wuq = wuq.reshape(C_Q_LORA, 8 * LANES).astype(BF16)
    wukv = w_ukv.reshape(C_KV_LORA, 8, C_NOPE + C_V)
    wk = jnp.concatenate([wukv[:, :, :C_NOPE], jnp.zeros((C_KV_LORA, 8, LANES - C_NOPE), F32)], axis=-1)
    wk = wk.reshape(C_KV_LORA, 8 * LANES)
    wv = wukv[:, :, C_NOPE:].reshape(C_KV_LORA, 8 * C_V)
    place = np.zeros((LANES, 8, LANES), np.float32)
    for j in range(C_ROPE):
        place[j, :, C_NOPE + j] = 1.0
    place = jnp.asarray(place.reshape(LANES, 8 * LANES))
    wkv = jnp.concatenate([jnp.concatenate([wk, wv], axis=1),
                           jnp.concatenate([place, jnp.zeros((LANES, 8 * C_V), F32)], axis=1)], axis=0).astype(BF16)
    ones_bd = jnp.asarray(np.kron(np.eye(2, dtype=np.float32), np.ones((HEAD_DIM, HEAD_DIM), np.float32))).astype(BF16)
    gqn = jnp.tile(g_qn.astype(F32) * (1.0 / 8.0), 2).reshape(1, LANES)
    gkn = jnp.tile(g_kn.astype(F32), 2).reshape(1, LANES)
    cosa, sina, cosq, sinq, cosk, sink = tabs
    tok = lambda n: pl.BlockSpec((1, tm, n), lambda i, t: (i, t, 0))
    whole = lambda a: pl.BlockSpec(a.shape, lambda i, t: (0,) * a.ndim)
    tab = pl.BlockSpec((tm, LANES), lambda i, t: (t, 0))
    small = [g_mix.reshape(1, d), w, g_q_lora.reshape(1, -1).astype(F32), wuq, g_kv_lora.reshape(1, -1).astype(F32),
             wkv, gqn, gkn, ones_bd]
    return pl.pallas_call(
        _proj1_kernel,
        grid=(b, u // tm),
        in_specs=[tok(d), pl.BlockSpec((1, 1, 6, d), lambda i, t: (i, jnp.minimum(t, 1), 0, 0))]
                 + [whole(a) for a in small] + [tab] * 6,
        out_specs=[tok(1024), tok(512), tok(1024), tok(512), tok(512)],
        out_shape=[jax.ShapeDtypeStruct((b, u, n), BF16) for n in (1024, 512, 1024, 512, 512)],
        compiler_params=_cparams(("arbitrary", "arbitrary")),
        name="proj1",
    )(xa, mods, *small, cosa, sina, cosq, sinq, cosk, sink)


def _mla_kernel(q_ref, k_ref, v_ref, o_ref):
    outs = []
    for h in range(8):
        s = _qk(q_ref[0, :, h * LANES:(h + 1) * LANES], k_ref[0, :, h * LANES:(h + 1) * LANES])
        outs.append(_softmax_pv(s, v_ref[0, :, (h // 2) * LANES:(h // 2 + 1) * LANES]))
    o_ref[0] = jnp.concatenate([_merge_halves(outs[2 * p], outs[2 * p + 1]) for p in range(4)], axis=1).astype(BF16)


def _mla_attention(qm, km, vm):
    b, u, _ = qm.shape
    tq = 256
    nq = (u - CTX_LEN) // tq
    off = CTX_LEN // tq
    return pl.pallas_call(
        _mla_kernel,
        grid=(b, nq),
        in_specs=[pl.BlockSpec((1, tq, 1024), lambda i, n: (i, n + off, 0)),
                  pl.BlockSpec((1, u, 1024), lambda i, n: (i, 0, 0)),
                  pl.BlockSpec((1, u, 512), lambda i, n: (i, 0, 0))],
        out_specs=pl.BlockSpec((1, tq, 512), lambda i, n: (i, n, 0)),
        out_shape=jax.ShapeDtypeStruct((b, u - CTX_LEN, 512), BF16),
        compiler_params=_cparams(("arbitrary", "arbitrary")),
        name="mla_attn",
    )(qm, km, vm)


def _gqa_kernel(q_ref, kv_ref, o_ref):
    outs = [None] * 8
    for g in range(2):
        k = kv_ref[0, :, g * LANES:(g + 1) * LANES]
        v = kv_ref[0, :, 256 + g * LANES:256 + (g + 1) * LANES]
        for hh in range(4):
            h = 4 * g + hh
            q = _keep_half(q_ref[0, :, (h // 2) * LANES:(h // 2 + 1) * LANES], h % 2)
            outs[h] = _softmax_pv(_qk(q, k), v)
    o_ref[0] = jnp.concatenate([_merge_halves(outs[2 * p], outs[2 * p + 1]) for p in range(4)], axis=1).astype(BF16)


def _gqa_attention(qd, kvd):
    b, u, _ = qd.shape
    tq = 256
    nq = (u - CTX_LEN) // tq
    off = CTX_LEN // tq
    return pl.pallas_call(
        _gqa_kernel,
        grid=(b, nq),
        in_specs=[pl.BlockSpec((1, tq, 512), lambda i, n: (i, n + off, 0)),
                  pl.BlockSpec((1, u, 512), lambda i, n: (i, 0, 0))],
        out_specs=pl.BlockSpec((1, tq, 512), lambda i, n: (i, n, 0)),
        out_shape=jax.ShapeDtypeStruct((b, u - CTX_LEN, 512), BF16),
        compiler_params=_cparams(("arbitrary", "arbitrary")),
        name="gqa_attn",
    )(qd, kvd)


def _post_attn_kernel(ya_ref, yb_ref, w_ref, x_ref, mod_ref, g_ref, wr_ref, br_ref,
                      xmid_ref, h2p_ref, idx_ref, wts_ref, rank_ref, cnt_ref, run_ref):
    first = (pl.program_id(0) == 0) & (pl.program_id(1) == 0)

    @pl.when(first)
    def _():
        run_ref[...] = jnp.zeros_like(run_ref)

    y = (jnp.dot(ya_ref[0], w_ref[0:512, :], preferred_element_type=F32)
         + jnp.dot(yb_ref[0], w_ref[512:1024, :], preferred_element_type=F32))
    x1 = x_ref[0] + mod_ref[0, 0, 2:3, :] * y
    xmid_ref[0] = x1
    h2 = _norm_mod(x1, g_ref[...], mod_ref[0, 0, 3:4, :], mod_ref[0, 0, 4:5, :])
    half = h2.shape[1] // 2
    h2p_ref[...] = _pack_bf16_pair(h2[:, :half], h2[:, half:])

    logits = jnp.dot(h2, wr_ref[...], precision=lax.Precision.HIGHEST, preferred_element_type=F32)
    scores = jax.nn.sigmoid(logits)
    tm = scores.shape[0]
    lane = lax.broadcasted_iota(I32, (tm, LANES), 1).astype(F32)
    biased = jnp.where(lane < N_EXPERTS, scores + br_ref[...], -jnp.inf)
    picked = jnp.zeros((tm, LANES), F32)
    sel_idx, sel_val = [], []
    for _k in range(TOP_K):
        m = jnp.max(biased, axis=-1, keepdims=True)
        i_k = jnp.min(jnp.where(biased == m, lane, float(LANES)), axis=-1, keepdims=True)
        hit = lane == i_k
        sel_idx.append(i_k)
        sel_val.append(jnp.sum(jnp.where(hit, scores, 0.0), axis=-1, keepdims=True))
        picked = jnp.where(hit, 1.0, picked)
        biased = jnp.where(hit, -jnp.inf, biased)
    total = sel_val[0]
    for v in sel_val[1:]:
        total = total + v
    r_i = lax.broadcasted_iota(I32, (tm, tm), 0)
    c_i = lax.broadcasted_iota(I32, (tm, tm), 1)
    before = jnp.dot(jnp.where(c_i < r_i, 1.0, 0.0).astype(BF16), picked.astype(BF16), preferred_element_type=F32)
    base = before + run_ref[...]
    idx_o = jnp.zeros((tm, LANES), F32)
    wts_o = jnp.zeros((tm, LANES), F32)
    rank_o = jnp.zeros((tm, LANES), F32)
    for k in range(TOP_K):
        idx_o = jnp.where(lane == float(k), sel_idx[k], idx_o)
        wts_o = jnp.where(lane == float(k), ROUTED_SCALE * sel_val[k] / total, wts_o)
        rank_k = jnp.sum(jnp.where(lane == sel_idx[k], base, 0.0), axis=-1, keepdims=True)
        rank_o = jnp.where(lane == float(k), rank_k, rank_o)
    idx_ref[...] = idx_o[:, :KPAD].astype(I32)
    wts_ref[...] = wts_o[:, :KPAD]
    rank_ref[...] = rank_o[:, :KPAD].astype(I32)
    new_run = run_ref[...] + jnp.sum(picked, axis=0, keepdims=True)
    run_ref[...] = new_run
    cnt_ref[...] = new_run


def _post_attn(ya, yb, w_out, x, x_tile_off, mods, mod_sel, g_moe, w_router, b_router):
    b, n, _ = ya.shape
    d = D_MODEL
    tm = TOK_TILE
    nt = n // tm
    t_tot = b * n
    wr = jnp.zeros((d, LANES), F32).at[:, :N_EXPERTS].set(w_router.astype(F32))
    br = jnp.zeros((1, LANES), F32).at[0, :N_EXPERTS].set(b_router.astype(F32))
    flat = lambda c: pl.BlockSpec((tm, c), lambda i, t: (i * nt + t, 0))
    outs = pl.pallas_call(
        _post_attn_kernel,
        grid=(b, nt),
        in_specs=[pl.BlockSpec((1, tm, 512), lambda i, t: (i, t, 0)),
                  pl.BlockSpec((1, tm, 512), lambda i, t: (i, t, 0)),
                  pl.BlockSpec((d, d), lambda i, t: (0, 0)),
                  pl.BlockSpec((1, tm, d), lambda i, t: (i, t + x_tile_off, 0)),
                  pl.BlockSpec((1, 1, 6, d), lambda i, t: (i, mod_sel(t), 0, 0)),
                  pl.BlockSpec((1, d), lambda i, t: (0, 0)),
                  pl.BlockSpec((d, LANES), lambda i, t: (0, 0)),
                  pl.BlockSpec((1, LANES), lambda i, t: (0, 0))],
        out_specs=[pl.BlockSpec((1, tm, d), lambda i, t: (i, t, 0)),
                   flat(d // 2), flat(KPAD), flat(KPAD), flat(KPAD),
                   pl.BlockSpec((1, LANES), lambda i, t: (0, 0))],
        out_shape=[jax.ShapeDtypeStruct((b, n, d), F32),
                   jax.ShapeDtypeStruct((t_tot, d // 2), U32),
                   jax.ShapeDtypeStruct((t_tot, KPAD), I32),
                   jax.ShapeDtypeStruct((t_tot, KPAD), F32),
                   jax.ShapeDtypeStruct((t_tot, KPAD), I32),
                   jax.ShapeDtypeStruct((1, LANES), F32)],
        scratch_shapes=[pltpu.VMEM((1, LANES), F32)],
        compiler_params=_cparams(("arbitrary", "arbitrary")),
        name="post_attn_route",
    )(ya, yb, w_out.astype(BF16), x, mods, g_moe.reshape(1, d), wr, br)
    return outs


def _row_copy(src, src_row, dst, dst_row, sem):
    return pltpu.make_async_copy(src.at[pl.ds(src_row, 1), :], dst.at[pl.ds(dst_row, 1), :], sem)


def _dispatch_kernel(poff_ref, idx_ref, rank_ref, h_ref, xs_ref, sem, *, tm):
    def issue(t, carry):
        for k in range(TOP_K):
            dst = poff_ref[idx_ref[0, 0, t * KPAD + k]] + rank_ref[0, 0, t * KPAD + k]
            _row_copy(h_ref, t, xs_ref, dst, sem).start()
        return carry

    lax.fori_loop(0, tm, issue, 0)

    def drain(t, carry):
        for k in range(TOP_K):
            _row_copy(h_ref, 0, xs_ref, 0, sem).wait()
        return carry

    lax.fori_loop(0, tm, drain, 0)


def _dispatch(h2p, idx3, rank3, poff, n_slots):
    t_tot, w = h2p.shape
    tm = TOK_TILE
    nt = t_tot // tm
    smem_tile = pl.BlockSpec((1, 1, tm * KPAD), lambda i: (i, 0, 0), memory_space=pltpu.SMEM)
    return pl.pallas_call(
        functools.partial(_dispatch_kernel, tm=tm),
        grid=(nt,),
        in_specs=[pl.BlockSpec(memory_space=pltpu.SMEM), smem_tile, smem_tile,
                  pl.BlockSpec((tm, w), lambda i: (i, 0))],
        out_specs=pl.BlockSpec(memory_space=pl.ANY),
        out_shape=jax.ShapeDtypeStruct((n_slots, w), U32),
        scratch_shapes=[pltpu.SemaphoreType.DMA],
        compiler_params=pltpu.CompilerParams(dimension_semantics=("arbitrary",), vmem_limit_bytes=VMEM_LIMIT,
                                             has_side_effects=True),
        name="moe_dispatch",
    )(poff, idx3, rank3, h2p)


def _expert_kernel(te_ref, nused_ref, xs_ref, wg_ref, wu_ref, wd_ref, ys_ref, wgu_s, wd_s):
    i = pl.program_id(0)
    changed = (i == 0) | (te_ref[i] != te_ref[jnp.maximum(i - 1, 0)])

    @pl.when(changed)
    def _():
        wgu_s[:, 0:EXPERT_FF] = wg_ref[0].astype(BF16)
        wgu_s[:, EXPERT_FF:2 * EXPERT_FF] = wu_ref[0].astype(BF16)
        wd_s[...] = wd_ref[0].astype(BF16)

    @pl.when(i < nused_ref[0])
    def _():
        lo, hi = _unpack_bf16_pair(xs_ref[...])
        x = jnp.concatenate([lo.astype(BF16), hi.astype(BF16)], axis=1)
        gu = jnp.dot(x, wgu_s[...], preferred_element_type=F32)
        gate, up = gu[:, :EXPERT_FF], gu[:, EXPERT_FF:]
        act = gate * jax.nn.sigmoid(gate) * up
        y = jnp.dot(act.astype(BF16), wd_s[...], preferred_element_type=F32)
        half = y.shape[1] // 2
        ys_ref[...] = _pack_bf16_pair(y[:, :half], y[:, half:])


def _expert_ffn(xs, tile_expert, n_used, w_gate, w_up, w_down):
    n_slots, w = xs.shape
    ts = SLOT_TILE
    d, f = w_gate.shape[1], w_gate.shape[2]
    grid_spec = pltpu.PrefetchScalarGridSpec(
        num_scalar_prefetch=2,
        grid=(n_slots // ts,),
        in_specs=[pl.BlockSpec((ts, w), lambda i, te, nu: (i, 0)),
                  pl.BlockSpec((1, d, f), lambda i, te, nu: (te[i], 0, 0)),
                  pl.BlockSpec((1, d, f), lambda i, te, nu: (te[i], 0, 0)),
                  pl.BlockSpec((1, f, d), lambda i, te, nu: (te[i], 0, 0))],
        out_specs=pl.BlockSpec((ts, w), lambda i, te, nu: (i, 0)),
        scratch_shapes=[pltpu.VMEM((d, 2 * f), BF16), pltpu.VMEM((f, d), BF16)],
    )
    return pl.pallas_call(
        _expert_kernel,
        grid_spec=grid_spec,
        out_shape=jax.ShapeDtypeStruct((n_slots, w), U32),
        compiler_params=_cparams(("arbitrary",)),
        name="moe_experts",
    )(tile_expert, n_used, xs, w_gate, w_up, w_down)


def _combine_kernel(poff_ref, idx_ref, rank_ref, ys_ref, wts_ref, h_ref, x_ref, mod_ref, wsgu_ref, wsd_ref, gf_ref,
                    o_ref, buf, sem, *, tm, final_norm):
    def issue(t, carry):
        for k in range(TOP_K):
            src = poff_ref[idx_ref[0, 0, t * KPAD + k]] + rank_ref[0, 0, t * KPAD + k]
            _row_copy(ys_ref, src, buf.at[k], t, sem).start()
        return carry

    lax.fori_loop(0, tm, issue, 0)

    lo, hi = _unpack_bf16_pair(h_ref[...])
    h = jnp.concatenate([lo.astype(BF16), hi.astype(BF16)], axis=1)
    gu = jnp.dot(h, wsgu_ref[...], preferred_element_type=F32)
    gate, up = gu[:, :SHARED_FF], gu[:, SHARED_FF:]
    acc = jnp.dot((gate * jax.nn.sigmoid(gate) * up).astype(BF16), wsd_ref[...], preferred_element_type=F32)

    def drain(t, carry):
        for k in range(TOP_K):
            _row_copy(ys_ref, 0, buf.at[k], 0, sem).wait()
        return carry

    lax.fori_loop(0, tm, drain, 0)

    wts = wts_ref[...]
    r_lo = jnp.zeros((tm, buf.shape[2]), F32)
    r_hi = jnp.zeros((tm, buf.shape[2]), F32)
    for k in range(TOP_K):
        y_lo, y_hi = _unpack_bf16_pair(buf[k])
        r_lo = r_lo + wts[:, k:k + 1] * y_lo
        r_hi = r_hi + wts[:, k:k + 1] * y_hi
    out = x_ref[...] + mod_ref[0, 0, 5:6, :] * (jnp.concatenate([r_lo, r_hi], axis=1) + acc)
    if final_norm:
        out = _rms(out, gf_ref[...])
    o_ref[...] = out


def _combine(ys, idx3, rank3, poff, wts, h2p, xmid, mods, mod_map, ws_gate, ws_up, ws_down, g_final, final_norm):
    t_tot, w = h2p.shape
    d = D_MODEL
    tm = TOK_TILE
    nt = t_tot // tm
    wsgu = jnp.concatenate([ws_gate, ws_up], axis=1).astype(BF16)
    smem_tile = pl.BlockSpec((1, 1, tm * KPAD), lambda i: (i, 0, 0), memory_space=pltpu.SMEM)
    return pl.pallas_call(
        functools.partial(_combine_kernel, tm=tm, final_norm=final_norm),
        grid=(nt,),
        in_specs=[pl.BlockSpec(memory_space=pltpu.SMEM), smem_tile, smem_tile,
                  pl.BlockSpec(memory_space=pl.ANY),
                  pl.BlockSpec((tm, KPAD), lambda i: (i, 0)),
                  pl.BlockSpec((tm, w), lambda i: (i, 0)),
                  pl.BlockSpec((tm, d), lambda i: (i, 0)),
                  pl.BlockSpec((1, 1, 6, d), mod_map),
                  pl.BlockSpec((d, 2 * SHARED_FF), lambda i: (0, 0)),
                  pl.BlockSpec((SHARED_FF, d), lambda i: (0, 0)),
                  pl.BlockSpec((1, d), lambda i: (0, 0))],
        out_specs=pl.BlockSpec((tm, d), lambda i: (i, 0)),
        out_shape=jax.ShapeDtypeStruct((t_tot, d), F32),
        scratch_shapes=[pltpu.VMEM((TOP_K, tm, w), U32), pltpu.SemaphoreType.DMA],
        compiler_params=_cparams(("arbitrary",)),
        name="moe_combine",
    )(poff, idx3, rank3, ys, wts, h2p, xmid.reshape(t_tot, d), mods, wsgu, ws_down.astype(BF16),
      g_final.reshape(1, d).astype(F32))


def _moe(h2p, idx, wts, rank, counts, xmid, mods, mod_map, moe_w, g_final, final_norm):
    (w_gate, w_up, w_down, ws_gate, ws_up, ws_down) = moe_w
    t_tot = h2p.shape[0]
    ts = SLOT_TILE
    nt = t_tot // TOK_TILE
    n_tiles = (t_tot * TOP_K) // ts + N_EXPERTS
    cnt = counts[0, :N_EXPERTS].astype(I32)
    tiles_e = (cnt + ts - 1) // ts
    ends = jnp.cumsum(tiles_e)
    poff = jnp.zeros((LANES,), I32).at[:N_EXPERTS].set((ends - tiles_e) * ts)
    n_used = ends[-1:]
    tile_ids = jnp.minimum(jnp.arange(n_tiles, dtype=I32), n_used[0] - 1)
    tile_expert = jnp.minimum(jnp.searchsorted(ends, tile_ids, side="right"), N_EXPERTS - 1).astype(I32)
    idx3 = idx.reshape(nt, 1, TOK_TILE * KPAD)
    rank3 = rank.reshape(nt, 1, TOK_TILE * KPAD)
    xs = _dispatch(h2p, idx3, rank3, poff, n_tiles * ts)
    ys = _expert_ffn(xs, tile_expert, n_used.astype(I32), w_gate, w_up, w_down)
    return _combine(ys, idx3, rank3, poff, wts, h2p, xmid, mods, mod_map, ws_gate, ws_up, ws_down, g_final,
                    final_norm)


def kernel(x, c, ctx, c_ctx, l0_w_ada, l0_b_ada, l0_g_mix, l0_w_in, l0_sink, l0_rpb, l0_w_out, l0_g_moe, l0_w_router, l0_b_router, l0_w_gate, l0_w_up, l0_w_down, l0_ws_gate, l0_ws_up, l0_ws_down, l1_w_ada, l1_b_ada, l1_g_mix, l1_w_in, l1_g_q_lora, l1_w_uq, l1_g_kv_lora, l1_w_ukv, l1_g_qn, l1_g_kn, l1_w_out, l1_g_moe, l1_w_router, l1_b_router, l1_w_gate, l1_w_up, l1_w_down, l1_ws_gate, l1_ws_up, l1_ws_down, g_final):
    b, seq, d = x.shape
    assert d == D_MODEL and ctx.shape[1] == CTX_LEN and seq % TOK_TILE == 0
    u = CTX_LEN + seq
    tiles_u = u // TOK_TILE
    tiles_s = seq // TOK_TILE
    xa = jnp.concatenate([ctx, x], axis=1).astype(F32)

    cos_a, sin_a = _rope_tables(seq, HEAD_DIM, 0, HEAD_DIM, 1.0)
    cos_q, sin_q = _rope_tables(seq, C_ROPE, C_NOPE, LANES, float((C_NOPE + C_ROPE) ** -0.5))
    cos_k, sin_k = _rope_tables(seq, C_ROPE, 0, LANES, 1.0)

    mods0 = _mods(c, c_ctx, l0_w_ada, l0_b_ada)
    qa, kva, qb, kb, vb = _proj0(xa, mods0, l0_g_mix, l0_w_in, cos_a, sin_a)
    ya = _window_attention(qa, kva, l0_sink)
    yb = _na_attention(qb, kb, vb, _na_bias_table(l0_rpb))
    xmid, h2p, idx, wts, rank, counts = _post_attn(ya, yb, l0_w_out, xa, 0, mods0, lambda t: jnp.minimum(t, 1),
                                                   l0_g_moe, l0_w_router, l0_b_router)
    xa = _moe(h2p, idx, wts, rank, counts, xmid, mods0,
              lambda i: (i // tiles_u, jnp.minimum(i % tiles_u, 1), 0, 0),
              (l0_w_gate, l0_w_up, l0_w_down, l0_ws_gate, l0_ws_up, l0_ws_down), g_final, False).reshape(b, u, d)

    mods1 = _mods(c, c_ctx, l1_w_ada, l1_b_ada)
    qm, qd, km, vm, kvd = _proj1(xa, mods1, l1_g_mix, l1_w_in, l1_g_q_lora, l1_w_uq, l1_g_kv_lora, l1_w_ukv,
                                 l1_g_qn, l1_g_kn, (cos_a, sin_a, cos_q, sin_q, cos_k, sin_k))
    ym = _mla_attention(qm, km, vm)
    yd = _gqa_attention(qd, kvd)
    xmid, h2p, idx, wts, rank, counts = _post_attn(ym, yd, l1_w_out, xa, CTX_LEN // TOK_TILE, mods1, lambda t: 1,
                                                   l1_g_moe, l1_w_router, l1_b_router)
    out = _moe(h2p, idx, wts, rank, counts, xmid, mods1, lambda i: (i // tiles_s, 1, 0, 0),
               (l1_w_gate, l1_w_up, l1_w_down, l1_ws_gate, l1_ws_up, l1_ws_down), g_final, True)
    return out.reshape(b, seq, d)
```

```python
import functools

import numpy as np
import jax
import jax.numpy as jnp
from jax import lax
from jax.experimental import pallas as pl
from jax.experimental.pallas import tpu as pltpu

F32 = jnp.float32
BF16 = jnp.bfloat16
U32 = jnp.uint32
I32 = jnp.int32

D_MODEL = 1024
CTX_LEN = 256
GRID_W = 64
HEAD_DIM = 64
ROPE_THETA = 10000.0
NORM_EPS = 1e-6
NEG_INF = -1e30
A_WINDOW = 128
NA_ROWS = 8
NA_COLS = 16
C_Q_LORA = 384
C_KV_LORA = 256
C_NOPE = 64
C_ROPE = 32
C_V = 64
N_EXPERTS = 64
TOP_K = 6
EXPERT_FF = 256
SHARED_FF = 256
ROUTED_SCALE = 2.5

LANES = 128
TOK_TILE = 256
SLOT_TILE = 512
KPAD = 8
VMEM_LIMIT = 48 * 1024 * 1024


def _cparams(sem):
    return pltpu.CompilerParams(dimension_semantics=sem, vmem_limit_bytes=VMEM_LIMIT)


def _rms(x, g):
    return x * lax.rsqrt(jnp.mean(x * x, axis=-1, keepdims=True) + NORM_EPS) * g


def _norm_mod(x, g, shift, scale):
    return _rms(x, g) * (1.0 + scale) + shift


def _rope128(x, cos, sin):
    lane = lax.broadcasted_iota(I32, x.shape, 1)
    swapped = jnp.where(lane % 2 == 0, pltpu.roll(x, LANES - 1, 1), pltpu.roll(x, 1, 1))
    return x * cos + swapped * sin


def _group_sumsq(x, ones_bd):
    sq = x * x
    hi = sq.astype(BF16)
    lo = (sq - hi.astype(F32)).astype(BF16)
    return (jnp.dot(hi, ones_bd, preferred_element_type=F32) + jnp.dot(lo, ones_bd, preferred_element_type=F32))


def _head_norm(x, ones_bd, g):
    return x * lax.rsqrt(_group_sumsq(x, ones_bd) * (1.0 / HEAD_DIM) + NORM_EPS) * g


def _pack_bf16_pair(lo, hi):
    lo_bits = lax.bitcast_convert_type(lo.astype(BF16).astype(F32), U32)
    hi_bits = lax.bitcast_convert_type(hi.astype(BF16).astype(F32), U32)
    return (hi_bits & jnp.uint32(0xFFFF0000)) | (lo_bits >> 16)


def _unpack_bf16_pair(u):
    lo = lax.bitcast_convert_type(u << 16, F32)
    hi = lax.bitcast_convert_type(u & jnp.uint32(0xFFFF0000), F32)
    return lo, hi


def _keep_half(q128, half):
    lane = lax.broadcasted_iota(I32, q128.shape, 1)
    keep = (lane < HEAD_DIM) if half == 0 else (lane >= HEAD_DIM)
    return jnp.where(keep, q128, jnp.zeros_like(q128))


def _merge_halves(o_even, o_odd):
    lane = lax.broadcasted_iota(I32, o_even.shape, 1)
    return jnp.where(lane < HEAD_DIM, o_even, o_odd)


def _qk(q, k):
    return lax.dot_general(q, k, (((1,), (1,)), ((), ())), preferred_element_type=F32)


def _softmax_pv(s, v, sink=None):
    m = jnp.max(s, axis=-1, keepdims=True)
    if sink is not None:
        m = jnp.maximum(m, sink)
    e = jnp.exp(s - m)
    den = jnp.sum(e, axis=-1, keepdims=True)
    if sink is not None:
        den = den + jnp.exp(sink - m)
    return jnp.dot(e.astype(BF16), v, preferred_element_type=F32) / den


def _ada_kernel(c_ref, w_ref, b_ref, o_ref):
    c = c_ref[...]
    a = c * jax.nn.sigmoid(c)
    o_ref[...] = jnp.dot(a, w_ref[...], precision=lax.Precision.HIGHEST, preferred_element_type=F32) + b_ref[...]


def _ada(cond, w_ada, b_ada):
    n, d = cond.shape
    nout = w_ada.shape[1]
    bn = 512
    return pl.pallas_call(
        _ada_kernel,
        grid=(nout // bn,),
        in_specs=[pl.BlockSpec((n, d), lambda j: (0, 0)),
                  pl.BlockSpec((d, bn), lambda j: (0, j)),
                  pl.BlockSpec((1, bn), lambda j: (0, j))],
        out_specs=pl.BlockSpec((n, bn), lambda j: (0, j)),
        out_shape=jax.ShapeDtypeStruct((n, nout), F32),
        compiler_params=_cparams(("arbitrary",)),
        name="ada",
    )(cond, w_ada, b_ada.reshape(1, nout))


def _mods(c, c_ctx, w_ada, b_ada):
    b = c.shape[0]
    rows = ((b + 1 + 7) // 8) * 8
    cond = jnp.zeros((rows, D_MODEL), F32).at[:b].set(c).at[b].set(c_ctx)
    out = _ada(cond, w_ada, b_ada)
    lat = out[:b].reshape(b, 1, 6, D_MODEL)
    cx = jnp.broadcast_to(out[b].reshape(1, 1, 6, D_MODEL), (b, 1, 6, D_MODEL))
    return jnp.concatenate([cx, lat], axis=1)


def _axial_angles(n_tokens, rot_dim):
    t = np.arange(n_tokens)
    row = (t // GRID_W).astype(np.float32)
    col = (t % GRID_W).astype(np.float32)
    n_axis = rot_dim // 4
    inv_freq = (np.float32(ROPE_THETA) ** (-np.arange(n_axis, dtype=np.float32) / n_axis)).astype(np.float32)
    return jnp.concatenate([jnp.asarray(row[:, None] * inv_freq), jnp.asarray(col[:, None] * inv_freq)], axis=-1)


def _rope_tables(seq, rot_dim, lane_start, period, scale):
    ang = _axial_angles(seq, rot_dim)
    cos = jnp.repeat(jnp.cos(ang), 2, axis=-1)
    sin = jnp.repeat(jnp.sin(ang), 2, axis=-1) * jnp.tile(jnp.asarray([-1.0, 1.0], F32), rot_dim // 2)
    cos_p = jnp.ones((seq, period), F32).at[:, lane_start:lane_start + rot_dim].set(cos)
    sin_p = jnp.zeros((seq, period), F32).at[:, lane_start:lane_start + rot_dim].set(sin)
    cos_f = jnp.concatenate([jnp.ones((CTX_LEN, period), F32), cos_p], axis=0)
    sin_f = jnp.concatenate([jnp.zeros((CTX_LEN, period), F32), sin_p], axis=0)
    reps = LANES // period
    return jnp.tile(cos_f, (1, reps)) * scale, jnp.tile(sin_f, (1, reps)) * scale


def _na_bias_table(rpb):
    h = rpb.shape[0]
    qc = np.arange(GRID_W)
    kc = np.arange(GRID_W)
    cstart = np.clip(qc - NA_COLS // 2, 0, GRID_W - NA_COLS)
    valid = (kc[None, :] >= cstart[:, None]) & (kc[None, :] < cstart[:, None] + NA_COLS)
    dc = np.clip(kc[None, :] - qc[:, None] + NA_COLS - 1, 0, 2 * NA_COLS - 2)
    tabs = []
    for v in range(NA_ROWS):
        dr = np.clip(np.arange(NA_ROWS) - v + NA_ROWS - 1, 0, 2 * NA_ROWS - 2)
        b = rpb[:, dr][:, :, dc]
        b = jnp.where(jnp.asarray(valid)[None, None], b.astype(F32), NEG_INF)
        tabs.append(jnp.transpose(b, (0, 2, 1, 3)).reshape(h, GRID_W, NA_ROWS * GRID_W))
    tabs.append(jnp.full((h, GRID_W, NA_ROWS * GRID_W), NEG_INF, F32))
    loc = jnp.stack(tabs, axis=0)
    return jnp.concatenate([jnp.zeros(loc.shape[:3] + (CTX_LEN,), F32), loc], axis=-1)


def _dup_heads(w, n_heads):
    d = w.shape[0]
    w = w.reshape(d, n_heads, 1, HEAD_DIM)
    return jnp.broadcast_to(w, (d, n_heads, 2, HEAD_DIM)).reshape(d, n_heads * 2 * HEAD_DIM)


def _proj0_kernel(x_ref, mod_ref, g_ref, w_ref, cos_ref, sin_ref, qa_ref, kva_ref, qb_ref, kb_ref, vb_ref):
    h = _norm_mod(x_ref[0], g_ref[...], mod_ref[0, 0, 0:1, :], mod_ref[0, 0, 1:2, :])
    r = jnp.dot(h.astype(BF16), w_ref[...], preferred_element_type=F32)
    cos, sin = cos_ref[...], sin_ref[...]
    roped = [_rope128(r[:, i * LANES:(i + 1) * LANES], cos, sin) for i in range(6)]
    qa_ref[0] = jnp.concatenate(roped[0:4], axis=1).astype(BF16)
    kva_ref[0] = jnp.concatenate(roped[4:6] + [r[:, 768:1024]], axis=1).astype(BF16)
    qb_ref[0] = r[:, 1024:1536].astype(BF16)
    kb_ref[0] = r[:, 1536:2048].astype(BF16)
    vb_ref[0] = r[:, 2048:2560].astype(BF16)


def _proj0(xa, mods, g_mix, w_in, cos, sin):
    b, u, d = xa.shape
    tm = TOK_TILE
    s = 1.0 / 8.0
    w = jnp.concatenate([w_in[:, 0:512] * s, _dup_heads(w_in[:, 512:640], 2), _dup_heads(w_in[:, 640:768], 2),
                         w_in[:, 768:1280] * s, w_in[:, 1280:1792], w_in[:, 1792:2304]], axis=1).astype(BF16)
    nw = w.shape[1]
    tok = lambda n: pl.BlockSpec((1, tm, n), lambda i, t: (i, t, 0))
    return pl.pallas_call(
        _proj0_kernel,
        grid=(b, u // tm),
        in_specs=[tok(d),
                  pl.BlockSpec((1, 1, 6, d), lambda i, t: (i, jnp.minimum(t, 1), 0, 0)),
                  pl.BlockSpec((1, d), lambda i, t: (0, 0)),
                  pl.BlockSpec((d, nw), lambda i, t: (0, 0)),
                  pl.BlockSpec((tm, LANES), lambda i, t: (t, 0)),
                  pl.BlockSpec((tm, LANES), lambda i, t: (t, 0))],
        out_specs=[tok(512)] * 5,
        out_shape=[jax.ShapeDtypeStruct((b, u, 512), BF16)] * 5,
        compiler_params=_cparams(("arbitrary", "arbitrary")),
        name="proj0",
    )(xa, mods, g_mix.reshape(1, d), w, cos, sin)


def _window_kernel(sink_ref, q_ref, kv_ref, o_ref, *, tq, span):
    n = pl.program_id(1)
    u = kv_ref.shape[1]
    ls = pl.multiple_of(jnp.clip((n - 1) * tq, CTX_LEN, u - span), tq)
    qpos = n * tq + lax.broadcasted_iota(I32, (tq, span), 0)
    kpos = ls + lax.broadcasted_iota(I32, (tq, span), 1)
    valid = (kpos >= CTX_LEN) & (jnp.abs(qpos - kpos) <= A_WINDOW) & (qpos >= CTX_LEN)
    kv_c = kv_ref[0, 0:CTX_LEN, :]
    kv_l = kv_ref[0, pl.ds(ls, span), :]
    outs = [None] * 8
    for g in range(2):
        kc = kv_c[:, g * LANES:(g + 1) * LANES]
        kl = kv_l[:, g * LANES:(g + 1) * LANES]
        v_all = jnp.concatenate([kv_c[:, 256 + g * LANES:256 + (g + 1) * LANES],
                                 kv_l[:, 256 + g * LANES:256 + (g + 1) * LANES]], axis=0)
        for hh in range(4):
            h = 4 * g + hh
            q = _keep_half(q_ref[0, :, (h // 2) * LANES:(h // 2 + 1) * LANES], h % 2)
            s_l = jnp.where(valid, _qk(q, kl), NEG_INF)
            s = jnp.concatenate([_qk(q, kc), s_l], axis=1)
            outs[h] = _softmax_pv(s, v_all, sink_ref[h])
    o_ref[0] = jnp.concatenate([_merge_halves(outs[2 * p], outs[2 * p + 1]) for p in range(4)], axis=1).astype(BF16)


def _window_attention(qa, kva, sink):
    b, u, _ = qa.shape
    tq = A_WINDOW
    span = 3 * A_WINDOW
    return pl.pallas_call(
        functools.partial(_window_kernel, tq=tq, span=span),
        grid=(b, u // tq),
        in_specs=[pl.BlockSpec(memory_space=pltpu.SMEM),
                  pl.BlockSpec((1, tq, 512), lambda i, n: (i, n, 0)),
                  pl.BlockSpec((1, u, 512), lambda i, n: (i, 0, 0))],
        out_specs=pl.BlockSpec((1, tq, 512), lambda i, n: (i, n, 0)),
        out_shape=jax.ShapeDtypeStruct((b, u, 512), BF16),
        compiler_params=_cparams(("arbitrary", "arbitrary")),
        name="window_attn",
    )(sink.astype(F32), qa, kva)


def _na_variant(j, rows):
    r = jnp.maximum(j - CTX_LEN // GRID_W, 0)
    rs = jnp.clip(r - NA_ROWS // 2, 0, rows - NA_ROWS)
    return r, rs


def _na_kernel(q_ref, k_ref, v_ref, bias_ref, o_ref, *, rows):
    j = pl.program_id(1)
    _, rs = _na_variant(j, rows)
    band = NA_ROWS * GRID_W
    start = pl.multiple_of(CTX_LEN + rs * GRID_W, GRID_W)
    k_all = jnp.concatenate([k_ref[0, 0:CTX_LEN, :], k_ref[0, pl.ds(start, band), :]], axis=0)
    v_all = jnp.concatenate([v_ref[0, 0:CTX_LEN, :], v_ref[0, pl.ds(start, band), :]], axis=0)
    outs = []
    for p in range(4):
        q128 = q_ref[0, :, p * LANES:(p + 1) * LANES]
        k128 = k_all[:, p * LANES:(p + 1) * LANES]
        v128 = v_all[:, p * LANES:(p + 1) * LANES]
        pair = []
        for half in range(2):
            s = _qk(_keep_half(q128, half), k128) + bias_ref[0, 2 * p + half]
            pair.append(_softmax_pv(s, v128))
        outs.append(_merge_halves(pair[0], pair[1]))
    o_ref[0] = jnp.concatenate(outs, axis=1).astype(BF16)


def _na_attention(qb, kb, vb, bias_tab):
    b, u, _ = qb.shape
    rows = (u - CTX_LEN) // GRID_W
    assert rows >= NA_ROWS
    nk = bias_tab.shape[-1]

    def bias_map(i, j):
        r, rs = _na_variant(j, rows)
        return (jnp.where(j < CTX_LEN // GRID_W, NA_ROWS, r - rs), 0, 0, 0)

    return pl.pallas_call(
        functools.partial(_na_kernel, rows=rows),
        grid=(b, u // GRID_W),
        in_specs=[pl.BlockSpec((1, GRID_W, 512), lambda i, j: (i, j, 0)),
                  pl.BlockSpec((1, u, 512), lambda i, j: (i, 0, 0)),
                  pl.BlockSpec((1, u, 512), lambda i, j: (i, 0, 0)),
                  pl.BlockSpec((1, 8, GRID_W, nk), bias_map)],
        out_specs=pl.BlockSpec((1, GRID_W, 512), lambda i, j: (i, j, 0)),
        out_shape=jax.ShapeDtypeStruct((b, u, 512), BF16),
        compiler_params=_cparams(("arbitrary", "arbitrary")),
        name="na_attn",
    )(qb, kb, vb, bias_tab)


def _proj1_kernel(x_ref, mod_ref, g_ref, w_ref, gq_ref, wuq_ref, gkv_ref, wkv_ref, gqn_ref, gkn_ref, ones_ref,
                  cosa_ref, sina_ref, cosq_ref, sinq_ref, cosk_ref, sink_ref,
                  qm_ref, qd_ref, km_ref, vm_ref, kvd_ref):
    h = _norm_mod(x_ref[0], g_ref[...], mod_ref[0, 0, 0:1, :], mod_ref[0, 0, 1:2, :])
    r = jnp.dot(h.astype(BF16), w_ref[...], preferred_element_type=F32)
    ones_bd = ones_ref[...]
    cosa, sina = cosa_ref[...], sina_ref[...]
    cq = _rms(r[:, 0:C_Q_LORA], gq_ref[...])
    qm = jnp.dot(cq.astype(BF16), wuq_ref[...], preferred_element_type=F32)
    cosq, sinq = cosq_ref[...], sinq_ref[...]
    qm_ref[0] = jnp.concatenate([_rope128(qm[:, i * LANES:(i + 1) * LANES], cosq, sinq) for i in range(8)],
                                axis=1).astype(BF16)
    gqn = gqn_ref[...]
    qd_ref[0] = jnp.concatenate(
        [_rope128(_head_norm(r[:, 384 + i * LANES:384 + (i + 1) * LANES], ones_bd, gqn), cosa, sina)
         for i in range(4)], axis=1).astype(BF16)
    ckv = _rms(r[:, 896:1152], gkv_ref[...])
    kr = _rope128(r[:, 1152:1280], cosk_ref[...], sink_ref[...])
    kv = jnp.dot(jnp.concatenate([ckv, kr], axis=1).astype(BF16), wkv_ref[...], preferred_element_type=F32)
    km_ref[0] = kv[:, 0:1024].astype(BF16)
    vm_ref[0] = kv[:, 1024:1536].astype(BF16)
    gkn = gkn_ref[...]
    kd = [_rope128(_head_norm(r[:, 1280 + i * LANES:1280 + (i + 1) * LANES], ones_bd, gkn), cosa, sina)
          for i in range(2)]
    kvd_ref[0] = jnp.concatenate(kd + [r[:, 1536:1792]], axis=1).astype(BF16)


def _proj1(xa, mods, g_mix, w_in, g_q_lora, w_uq, g_kv_lora, w_ukv, g_qn, g_kn, tabs):
    b, u, d = xa.shape
    tm = TOK_TILE
    zpad = jnp.zeros((d, LANES - C_ROPE), F32)
    w = jnp.concatenate([w_in[:, 0:896], w_in[:, 896:1152], w_in[:, 1152:1184], zpad,
                         _dup_heads(w_in[:, 1184:1312], 2), _dup_heads(w_in[:, 1312:1440], 2)], axis=1).astype(BF16)
    nw = w.shape[1]
    wuq = w_uq.reshape(C_Q_LORA, 8, C_NOPE + C_ROPE)
    wuq = jnp.concatenate([wuq, jnp.zeros((C_Q_LORA, 8, LANES - C_NOPE - C_ROPE), F32)], axis=-1)
    wuq = wuq.reshape(C_Q_LORA, 8 * LANES).astype(BF16)
    wukv = w_ukv.reshape(C_KV_LORA, 8, C_NOPE + C_V)
    wk = jnp.concatenate([wukv[:, :, :C_NOPE], jnp.zeros((C_KV_LORA, 8, LANES - C_NOPE), F32)], axis=-1)
    wk = wk.reshape(C_KV_LORA, 8 * LANES)
    wv = wukv[:, :, C_NOPE:].reshape(C_KV_LORA, 8 * C_V)
    place = np.zeros((LANES, 8, LANES), np.float32)
    for j in range(C_ROPE):
        place[j, :, C_NOPE + j] = 1.0
    place = jnp.asarray(place.reshape(LANES, 8 * LANES))
    wkv = jnp.concatenate([jnp.concatenate([wk, wv], axis=1),
                           jnp.concatenate([place, jnp.zeros((LANES, 8 * C_V), F32)], axis=1)], axis=0).astype(BF16)
    ones_bd = jnp.asarray(np.kron(np.eye(2, dtype=np.float32), np.ones((HEAD_DIM, HEAD_DIM), np.float32))).astype(BF16)
    gqn = jnp.tile(g_qn.astype(F32) * (1.0 / 8.0), 2).reshape(1, LANES)
    gkn = jnp.tile(g_kn.astype(F32), 2).reshape(1, LANES)
    cosa, sina, cosq, sinq, cosk, sink = tabs
    tok = lambda n: pl.BlockSpec((1, tm, n), lambda i, t: (i, t, 0))
    whole = lambda a: pl.BlockSpec(a.shape, lambda i, t: (0,) * a.ndim)
    tab = pl.BlockSpec((tm, LANES), lambda i, t: (t, 0))
    small = [g_mix.reshape(1, d), w, g_q_lora.reshape(1, -1).astype(F32), wuq, g_kv_lora.reshape(1, -1).astype(F32),
             wkv, gqn, gkn, ones_bd]
    return pl.pallas_call(
        _proj1_kernel,
        grid=(b, u // tm),
        in_specs=[tok(d), pl.BlockSpec((1, 1, 6, d), lambda i, t: (i, jnp.minimum(t, 1), 0, 0))]
                 + [whole(a) for a in small] + [tab] * 6,
        out_specs=[tok(1024), tok(512), tok(1024), tok(512), tok(512)],
        out_shape=[jax.ShapeDtypeStruct((b, u, n), BF16) for n in (1024, 512, 1024, 512, 512)],
        compiler_params=_cparams(("arbitrary", "arbitrary")),
        name="proj1",
    )(xa, mods, *small, cosa, sina, cosq, sinq, cosk, sink)


def _mla_kernel(q_ref, k_ref, v_ref, o_ref):
    outs = []
    for h in range(8):
        s = _qk(q_ref[0, :, h * LANES:(h + 1) * LANES], k_ref[0, :, h * LANES:(h + 1) * LANES])
        outs.append(_softmax_pv(s, v_ref[0, :, (h // 2) * LANES:(h // 2 + 1) * LANES]))
    o_ref[0] = jnp.concatenate([_merge_halves(outs[2 * p], outs[2 * p + 1]) for p in range(4)], axis=1).astype(BF16)


def _mla_attention(qm, km, vm):
    b, u, _ = qm.shape
    tq = 256
    nq = (u - CTX_LEN) // tq
    off = CTX_LEN // tq
    return pl.pallas_call(
        _mla_kernel,
        grid=(b, nq),
        in_specs=[pl.BlockSpec((1, tq, 1024), lambda i, n: (i, n + off, 0)),
                  pl.BlockSpec((1, u, 1024), lambda i, n: (i, 0, 0)),
                  pl.BlockSpec((1, u, 512), lambda i, n: (i, 0, 0))],
        out_specs=pl.BlockSpec((1, tq, 512), lambda i, n: (i, n, 0)),
        out_shape=jax.ShapeDtypeStruct((b, u - CTX_LEN, 512), BF16),
        compiler_params=_cparams(("arbitrary", "arbitrary")),
        name="mla_attn",
    )(qm, km, vm)


def _gqa_kernel(q_ref, kv_ref, o_ref):
    outs = [None] * 8
    for g in range(2):
        k = kv_ref[0, :, g * LANES:(g + 1) * LANES]
        v = kv_ref[0, :, 256 + g * LANES:256 + (g + 1) * LANES]
        for hh in range(4):
            h = 4 * g + hh
            q = _keep_half(q_ref[0, :, (h // 2) * LANES:(h // 2 + 1) * LANES], h % 2)
            outs[h] = _softmax_pv(_qk(q, k), v)
    o_ref[0] = jnp.concatenate([_merge_halves(outs[2 * p], outs[2 * p + 1]) for p in range(4)], axis=1).astype(BF16)


def _gqa_attention(qd, kvd):
    b, u, _ = qd.shape
    tq = 256
    nq = (u - CTX_LEN) // tq
    off = CTX_LEN // tq
    return pl.pallas_call(
        _gqa_kernel,
        grid=(b, nq),
        in_specs=[pl.BlockSpec((1, tq, 512), lambda i, n: (i, n + off, 0)),
                  pl.BlockSpec((1, u, 512), lambda i, n: (i, 0, 0))],
        out_specs=pl.BlockSpec((1, tq, 512), lambda i, n: (i, n, 0)),
        out_shape=jax.ShapeDtypeStruct((b, u - CTX_LEN, 512), BF16),
        compiler_params=_cparams(("arbitrary", "arbitrary")),
        name="gqa_attn",
    )(qd, kvd)


def _post_attn_kernel(ya_ref, yb_ref, w_ref, x_ref, mod_ref, g_ref, wr_ref, br_ref,
                      xmid_ref, h2p_ref, idx_ref, wts_ref, rank_ref, cnt_ref, run_ref):
    first = (pl.program_id(0) == 0) & (pl.program_id(1) == 0)

    @pl.when(first)
    def _():
        run_ref[...] = jnp.zeros_like(run_ref)

    y = (jnp.dot(ya_ref[0], w_ref[0:512, :], preferred_element_type=F32)
         + jnp.dot(yb_ref[0], w_ref[512:1024, :], preferred_element_type=F32))
    x1 = x_ref[0] + mod_ref[0, 0, 2:3, :] * y
    xmid_ref[0] = x1
    h2 = _norm_mod(x1, g_ref[...], mod_ref[0, 0, 3:4, :], mod_ref[0, 0, 4:5, :])
    half = h2.shape[1] // 2
    h2p_ref[...] = _pack_bf16_pair(h2[:, :half], h2[:, half:])

    logits = jnp.dot(h2, wr_ref[...], precision=lax.Precision.HIGHEST, preferred_element_type=F32)
    scores = jax.nn.sigmoid(logits)
    tm = scores.shape[0]
    lane = lax.broadcasted_iota(I32, (tm, LANES), 1).astype(F32)
    biased = jnp.where(lane < N_EXPERTS, scores + br_ref[...], -jnp.inf)
    picked = jnp.zeros((tm, LANES), F32)
    sel_idx, sel_val = [], []
    for _k in range(TOP_K):
        m = jnp.max(biased, axis=-1, keepdims=True)
        i_k = jnp.min(jnp.where(biased == m, lane, float(LANES)), axis=-1, keepdims=True)
        hit = lane == i_k
        sel_idx.append(i_k)
        sel_val.append(jnp.sum(jnp.where(hit, scores, 0.0), axis=-1, keepdims=True))
        picked = jnp.where(hit, 1.0, picked)
        biased = jnp.where(hit, -jnp.inf, biased)
    total = sel_val[0]
    for v in sel_val[1:]:
        total = total + v
    r_i = lax.broadcasted_iota(I32, (tm, tm), 0)
    c_i = lax.broadcasted_iota(I32, (tm, tm), 1)
    before = jnp.dot(jnp.where(c_i < r_i, 1.0, 0.0).astype(BF16), picked.astype(BF16), preferred_element_type=F32)
    base = before + run_ref[...]
    idx_o = jnp.zeros((tm, LANES), F32)
    wts_o = jnp.zeros((tm, LANES), F32)
    rank_o = jnp.zeros((tm, LANES), F32)
    for k in range(TOP_K):
        idx_o = jnp.where(lane == float(k), sel_idx[k], idx_o)
        wts_o = jnp.where(lane == float(k), ROUTED_SCALE * sel_val[k] / total, wts_o)
        rank_k = jnp.sum(jnp.where(lane == sel_idx[k], base, 0.0), axis=-1, keepdims=True)
        rank_o = jnp.where(lane == float(k), rank_k, rank_o)
    idx_ref[...] = idx_o[:, :KPAD].astype(I32)
    wts_ref[...] = wts_o[:, :KPAD]
    rank_ref[...] = rank_o[:, :KPAD].astype(I32)
    new_run = run_ref[...] + jnp.sum(picked, axis=0, keepdims=True)
    run_ref[...] = new_run
    cnt_ref[...] = new_run


def _post_attn(ya, yb, w_out, x, x_tile_off, mods, mod_sel, g_moe, w_router, b_router):
    b, n, _ = ya.shape
    d = D_MODEL
    tm = TOK_TILE
    nt = n // tm
    t_tot = b * n
    wr = jnp.zeros((d, LANES), F32).at[:, :N_EXPERTS].set(w_router.astype(F32))
    br = jnp.zeros((1, LANES), F32).at[0, :N_EXPERTS].set(b_router.astype(F32))
    flat = lambda c: pl.BlockSpec((tm, c), lambda i, t: (i * nt + t, 0))
    outs = pl.pallas_call(
        _post_attn_kernel,
        grid=(b, nt),
        in_specs=[pl.BlockSpec((1, tm, 512), lambda i, t: (i, t, 0)),
                  pl.BlockSpec((1, tm, 512), lambda i, t: (i, t, 0)),
                  pl.BlockSpec((d, d), lambda i, t: (0, 0)),
                  pl.BlockSpec((1, tm, d), lambda i, t: (i, t + x_tile_off, 0)),
                  pl.BlockSpec((1, 1, 6, d), lambda i, t: (i, mod_sel(t), 0, 0)),
                  pl.BlockSpec((1, d), lambda i, t: (0, 0)),
                  pl.BlockSpec((d, LANES), lambda i, t: (0, 0)),
                  pl.BlockSpec((1, LANES), lambda i, t: (0, 0))],
        out_specs=[pl.BlockSpec((1, tm, d), lambda i, t: (i, t, 0)),
                   flat(d // 2), flat(KPAD), flat(KPAD), flat(KPAD),
                   pl.BlockSpec((1, LANES), lambda i, t: (0, 0))],
        out_shape=[jax.ShapeDtypeStruct((b, n, d), F32),
                   jax.ShapeDtypeStruct((t_tot, d // 2), U32),
                   jax.ShapeDtypeStruct((t_tot, KPAD), I32),
                   jax.ShapeDtypeStruct((t_tot, KPAD), F32),
                   jax.ShapeDtypeStruct((t_tot, KPAD), I32),
                   jax.ShapeDtypeStruct((1, LANES), F32)],
        scratch_shapes=[pltpu.VMEM((1, LANES), F32)],
        compiler_params=_cparams(("arbitrary", "arbitrary")),
        name="post_attn_route",
    )(ya, yb, w_out.astype(BF16), x, mods, g_moe.reshape(1, d), wr, br)
    return outs


def _row_copy(src, src_row, dst, dst_row, sem):
    return pltpu.make_async_copy(src.at[pl.ds(src_row, 1), :], dst.at[pl.ds(dst_row, 1), :], sem)


def _dispatch_kernel(poff_ref, idx_ref, rank_ref, h_ref, xs_ref, sem, *, tm):
    def issue(t, carry):
        for k in range(TOP_K):
            dst = poff_ref[idx_ref[0, 0, t * KPAD + k]] + rank_ref[0, 0, t * KPAD + k]
            _row_copy(h_ref, t, xs_ref, dst, sem).start()
        return carry

    lax.fori_loop(0, tm, issue, 0)

    def drain(t, carry):
        for k in range(TOP_K):
            _row_copy(h_ref, 0, xs_ref, 0, sem).wait()
        return carry

    lax.fori_loop(0, tm, drain, 0)


def _dispatch(h2p, idx3, rank3, poff, n_slots):
    t_tot, w = h2p.shape
    tm = TOK_TILE
    nt = t_tot // tm
    smem_tile = pl.BlockSpec((1, 1, tm * KPAD), lambda i: (i, 0, 0), memory_space=pltpu.SMEM)
    return pl.pallas_call(
        functools.partial(_dispatch_kernel, tm=tm),
        grid=(nt,),
        in_specs=[pl.BlockSpec(memory_space=pltpu.SMEM), smem_tile, smem_tile,
                  pl.BlockSpec((tm, w), lambda i: (i, 0))],
        out_specs=pl.BlockSpec(memory_space=pl.ANY),
        out_shape=jax.ShapeDtypeStruct((n_slots, w), U32),
        scratch_shapes=[pltpu.SemaphoreType.DMA],
        compiler_params=pltpu.CompilerParams(dimension_semantics=("arbitrary",), vmem_limit_bytes=VMEM_LIMIT,
                                             has_side_effects=True),
        name="moe_dispatch",
    )(poff, idx3, rank3, h2p)


def _expert_kernel(te_ref, nused_ref, xs_ref, wg_ref, wu_ref, wd_ref, ys_ref, wgu_s, wd_s):
    i = pl.program_id(0)
    changed = (i == 0) | (te_ref[i] != te_ref[jnp.maximum(i - 1, 0)])

    @pl.when(changed)
    def _():
        wgu_s[:, 0:EXPERT_FF] = wg_ref[0].astype(BF16)
        wgu_s[:, EXPERT_FF:2 * EXPERT_FF] = wu_ref[0].astype(BF16)
        wd_s[...] = wd_ref[0].astype(BF16)

    @pl.when(i < nused_ref[0])
    def _():
        lo, hi = _unpack_bf16_pair(xs_ref[...])
        x = jnp.concatenate([lo.astype(BF16), hi.astype(BF16)], axis=1)
        gu = jnp.dot(x, wgu_s[...], preferred_element_type=F32)
        gate, up = gu[:, :EXPERT_FF], gu[:, EXPERT_FF:]
        act = gate * jax.nn.sigmoid(gate) * up
        y = jnp.dot(act.astype(BF16), wd_s[...], preferred_element_type=F32)
        half = y.shape[1] // 2
        ys_ref[...] = _pack_bf16_pair(y[:, :half], y[:, half:])


def _expert_ffn(xs, tile_expert, n_used, w_gate, w_up, w_down):
    n_slots, w = xs.shape
    ts = SLOT_TILE
    d, f = w_gate.shape[1], w_gate.shape[2]
    grid_spec = pltpu.PrefetchScalarGridSpec(
        num_scalar_prefetch=2,
        grid=(n_slots // ts,),
        in_specs=[pl.BlockSpec((ts, w), lambda i, te, nu: (i, 0)),
                  pl.BlockSpec((1, d, f), lambda i, te, nu: (te[i], 0, 0)),
                  pl.BlockSpec((1, d, f), lambda i, te, nu: (te[i], 0, 0)),
                  pl.BlockSpec((1, f, d), lambda i, te, nu: (te[i], 0, 0))],
        out_specs=pl.BlockSpec((ts, w), lambda i, te, nu: (i, 0)),
        scratch_shapes=[pltpu.VMEM((d, 2 * f), BF16), pltpu.VMEM((f, d), BF16)],
    )
    return pl.pallas_call(
        _expert_kernel,
        grid_spec=grid_spec,
        out_shape=jax.ShapeDtypeStruct((n_slots, w), U32),
        compiler_params=_cparams(("arbitrary",)),
        name="moe_experts",
    )(tile_expert, n_used, xs, w_gate, w_up, w_down)


def _combine_kernel(poff_ref, idx_ref, rank_ref, ys_ref, wts_ref, h_ref, x_ref, mod_ref, wsgu_ref, wsd_ref, gf_ref,
                    o_ref, buf, sem, *, tm, final_norm):
    def issue(t, carry):
        for k in range(TOP_K):
            src = poff_ref[idx_ref[0, 0, t * KPAD + k]] + rank_ref[0, 0, t * KPAD + k]
            _row_copy(ys_ref, src, buf.at[k], t, sem).start()
        return carry

    lax.fori_loop(0, tm, issue, 0)

    lo, hi = _unpack_bf16_pair(h_ref[...])
    h = jnp.concatenate([lo.astype(BF16), hi.astype(BF16)], axis=1)
    gu = jnp.dot(h, wsgu_ref[...], preferred_element_type=F32)
    gate, up = gu[:, :SHARED_FF], gu[:, SHARED_FF:]
    acc = jnp.dot((gate * jax.nn.sigmoid(gate) * up).astype(BF16), wsd_ref[...], preferred_element_type=F32)

    def drain(t, carry):
        for k in range(TOP_K):
            _row_copy(ys_ref, 0, buf.at[k], 0, sem).wait()
        return carry

    lax.fori_loop(0, tm, drain, 0)

    wts = wts_ref[...]
    r_lo = jnp.zeros((tm, buf.shape[2]), F32)
    r_hi = jnp.zeros((tm, buf.shape[2]), F32)
    for k in range(TOP_K):
        y_lo, y_hi = _unpack_bf16_pair(buf[k])
        r_lo = r_lo + wts[:, k:k + 1] * y_lo
        r_hi = r_hi + wts[:, k:k + 1] * y_hi
    out = x_ref[...] + mod_ref[0, 0, 5:6, :] * (jnp.concatenate([r_lo, r_hi], axis=1) + acc)
    if final_norm:
        out = _rms(out, gf_ref[...])
    o_ref[...] = out


def _combine(ys, idx3, rank3, poff, wts, h2p, xmid, mods, mod_map, ws_gate, ws_up, ws_down, g_final, final_norm):
    t_tot, w = h2p.shape
    d = D_MODEL
    tm = TOK_TILE
    nt = t_tot // tm
    wsgu = jnp.concatenate([ws_gate, ws_up], axis=1).astype(BF16)
    smem_tile = pl.BlockSpec((1, 1, tm * KPAD), lambda i: (i, 0, 0), memory_space=pltpu.SMEM)
    return pl.pallas_call(
        functools.partial(_combine_kernel, tm=tm, final_norm=final_norm),
        grid=(nt,),
        in_specs=[pl.BlockSpec(memory_space=pltpu.SMEM), smem_tile, smem_tile,
                  pl.BlockSpec(memory_space=pl.ANY),
                  pl.BlockSpec((tm, KPAD), lambda i: (i, 0)),
                  pl.BlockSpec((tm, w), lambda i: (i, 0)),
                  pl.BlockSpec((tm, d), lambda i: (i, 0)),
                  pl.BlockSpec((1, 1, 6, d), mod_map),
                  pl.BlockSpec((d, 2 * SHARED_FF), lambda i: (0, 0)),
                  pl.BlockSpec((SHARED_FF, d), lambda i: (0, 0)),
                  pl.BlockSpec((1, d), lambda i: (0, 0))],
        out_specs=pl.BlockSpec((tm, d), lambda i: (i, 0)),
        out_shape=jax.ShapeDtypeStruct((t_tot, d), F32),
        scratch_shapes=[pltpu.VMEM((TOP_K, tm, w), U32), pltpu.SemaphoreType.DMA],
        compiler_params=_cparams(("arbitrary",)),
        name="moe_combine",
    )(poff, idx3, rank3, ys, wts, h2p, xmid.reshape(t_tot, d), mods, wsgu, ws_down.astype(BF16),
      g_final.reshape(1, d).astype(F32))


def _moe(h2p, idx, wts, rank, counts, xmid, mods, mod_map, moe_w, g_final, final_norm):
    (w_gate, w_up, w_down, ws_gate, ws_up, ws_down) = moe_w
    t_tot = h2p.shape[0]
    ts = SLOT_TILE
    nt = t_tot // TOK_TILE
    n_tiles = (t_tot * TOP_K) // ts + N_EXPERTS
    cnt = counts[0, :N_EXPERTS].astype(I32)
    tiles_e = (cnt + ts - 1) // ts
    ends = jnp.cumsum(tiles_e)
    poff = jnp.zeros((LANES,), I32).at[:N_EXPERTS].set((ends - tiles_e) * ts)
    n_used = ends[-1:]
    tile_ids = jnp.minimum(jnp.arange(n_tiles, dtype=I32), n_used[0] - 1)
    tile_expert = jnp.sum((ends[None, :] <= tile_ids[:, None]).astype(I32), axis=1)
    tile_expert = jnp.minimum(tile_expert, N_EXPERTS - 1)
    idx3 = idx.reshape(nt, 1, TOK_TILE * KPAD)
    rank3 = rank.reshape(nt, 1, TOK_TILE * KPAD)
    xs = _dispatch(h2p, idx3, rank3, poff, n_tiles * ts)
    ys = _expert_ffn(xs, tile_expert, n_used.astype(I32), w_gate, w_up, w_down)
    return _combine(ys, idx3, rank3, poff, wts, h2p, xmid, mods, mod_map, ws_gate, ws_up, ws_down, g_final,
                    final_norm)


def kernel(x, c, ctx, c_ctx, l0_w_ada, l0_b_ada, l0_g_mix, l0_w_in, l0_sink, l0_rpb, l0_w_out, l0_g_moe, l0_w_router, l0_b_router, l0_w_gate, l0_w_up, l0_w_down, l0_ws_gate, l0_ws_up, l0_ws_down, l1_w_ada, l1_b_ada, l1_g_mix, l1_w_in, l1_g_q_lora, l1_w_uq, l1_g_kv_lora, l1_w_ukv, l1_g_qn, l1_g_kn, l1_w_out, l1_g_moe, l1_w_router, l1_b_router, l1_w_gate, l1_w_up, l1_w_down, l1_ws_gate, l1_ws_up, l1_ws_down, g_final):
    b, seq, d = x.shape
    assert d == D_MODEL and ctx.shape[1] == CTX_LEN and seq % TOK_TILE == 0
    u = CTX_LEN + seq
    tiles_u = u // TOK_TILE
    tiles_s = seq // TOK_TILE
    xa = jnp.concatenate([ctx, x], axis=1).astype(F32)

    cos_a, sin_a = _rope_tables(seq, HEAD_DIM, 0, HEAD_DIM, 1.0)
    cos_q, sin_q = _rope_tables(seq, C_ROPE, C_NOPE, LANES, float((C_NOPE + C_ROPE) ** -0.5))
    cos_k, sin_k = _rope_tables(seq, C_ROPE, 0, LANES, 1.0)

    mods0 = _mods(c, c_ctx, l0_w_ada, l0_b_ada)
    qa, kva, qb, kb, vb = _proj0(xa, mods0, l0_g_mix, l0_w_in, cos_a, sin_a)
    ya = _window_attention(qa, kva, l0_sink)
    yb = _na_attention(qb, kb, vb, _na_bias_table(l0_rpb))
    xmid, h2p, idx, wts, rank, counts = _post_attn(ya, yb, l0_w_out, xa, 0, mods0, lambda t: jnp.minimum(t, 1),
                                                   l0_g_moe, l0_w_router, l0_b_router)
    xa = _moe(h2p, idx, wts, rank, counts, xmid, mods0,
              lambda i: (i // tiles_u, jnp.minimum(i % tiles_u, 1), 0, 0),
              (l0_w_gate, l0_w_up, l0_w_down, l0_ws_gate, l0_ws_up, l0_ws_down), g_final, False).reshape(b, u, d)

    mods1 = _mods(c, c_ctx, l1_w_ada, l1_b_ada)
    qm, qd, km, vm, kvd = _proj1(xa, mods1, l1_g_mix, l1_w_in, l1_g_q_lora, l1_w_uq, l1_g_kv_lora, l1_w_ukv,
                                 l1_g_qn, l1_g_kn, (cos_a, sin_a, cos_q, sin_q, cos_k, sin_k))
    ym = _mla_attention(qm, km, vm)
    yd = _gqa_attention(qd, kvd)
    xmid, h2p, idx, wts, rank, counts = _post_attn(ym, yd, l1_w_out, xa, CTX_LEN // TOK_TILE, mods1, lambda t: 1,
                                                   l1_g_moe, l1_w_router, l1_b_router)
    out = _moe(h2p, idx, wts, rank, counts, xmid, mods1, lambda i: (i // tiles_s, 1, 0, 0),
               (l1_w_gate, l1_w_up, l1_w_down, l1_ws_gate, l1_ws_up, l1_ws_down), g_final, True)
    return out.reshape(b, seq, d)
```

```python
import functools

import numpy as np
import jax
import jax.numpy as jnp
from jax import lax
from jax.experimental import pallas as pl
from jax.experimental.pallas import tpu as pltpu

F32 = jnp.float32
BF16 = jnp.bfloat16
U32 = jnp.uint32
I32 = jnp.int32

D_MODEL = 1024
CTX_LEN = 256
GRID_W = 64
HEAD_DIM = 64
ROPE_THETA = 10000.0
NORM_EPS = 1e-6
NEG_INF = -1e30
A_WINDOW = 128
NA_ROWS = 8
NA_COLS = 16
C_Q_LORA = 384
C_KV_LORA = 256
C_NOPE = 64
C_ROPE = 32
C_V = 64
N_EXPERTS = 64
TOP_K = 6
EXPERT_FF = 256
SHARED_FF = 256
ROUTED_SCALE = 2.5

LANES = 128
TOK_TILE = 256
SLOT_TILE = 512
KPAD = 8
GRAN = 8
CHUNK = 256
SORT_ROWS = 2048
NGRAN = SORT_ROWS // GRAN
VMEM_LIMIT = 48 * 1024 * 1024


def _cparams(sem):
    return pltpu.CompilerParams(dimension_semantics=sem, vmem_limit_bytes=VMEM_LIMIT)


def _rms(x, g):
    return x * lax.rsqrt(jnp.mean(x * x, axis=-1, keepdims=True) + NORM_EPS) * g


def _norm_mod(x, g, shift, scale):
    return _rms(x, g) * (1.0 + scale) + shift


def _rope128(x, cos, sin):
    lane = lax.broadcasted_iota(I32, x.shape, 1)
    swapped = jnp.where(lane % 2 == 0, pltpu.roll(x, LANES - 1, 1), pltpu.roll(x, 1, 1))
    return x * cos + swapped * sin


def _group_sumsq(x, ones_bd):
    sq = x * x
    hi = sq.astype(BF16)
    lo = (sq - hi.astype(F32)).astype(BF16)
    return (jnp.dot(hi, ones_bd, preferred_element_type=F32) + jnp.dot(lo, ones_bd, preferred_element_type=F32))


def _head_norm(x, ones_bd, g):
    return x * lax.rsqrt(_group_sumsq(x, ones_bd) * (1.0 / HEAD_DIM) + NORM_EPS) * g


def _pack_bf16_pair(lo, hi):
    lo_bits = lax.bitcast_convert_type(lo.astype(BF16).astype(F32), U32)
    hi_bits = lax.bitcast_convert_type(hi.astype(BF16).astype(F32), U32)
    return (hi_bits & jnp.uint32(0xFFFF0000)) | (lo_bits >> 16)


def _unpack_bf16_pair(u):
    lo = lax.bitcast_convert_type(u << 16, F32)
    hi = lax.bitcast_convert_type(u & jnp.uint32(0xFFFF0000), F32)
    return lo, hi


def _keep_half(q128, half):
    lane = lax.broadcasted_iota(I32, q128.shape, 1)
    keep = (lane < HEAD_DIM) if half == 0 else (lane >= HEAD_DIM)
    return jnp.where(keep, q128, jnp.zeros_like(q128))


def _merge_halves(o_even, o_odd):
    lane = lax.broadcasted_iota(I32, o_even.shape, 1)
    return jnp.where(lane < HEAD_DIM, o_even, o_odd)


def _qk(q, k):
    return lax.dot_general(q, k, (((1,), (1,)), ((), ())), preferred_element_type=F32)


def _softmax_pv(s, v, sink=None):
    m = jnp.max(s, axis=-1, keepdims=True)
    if sink is not None:
        m = jnp.maximum(m, sink)
    e = jnp.exp(s - m)
    den = jnp.sum(e, axis=-1, keepdims=True)
    if sink is not None:
        den = den + jnp.exp(sink - m)
    return jnp.dot(e.astype(BF16), v, preferred_element_type=F32) / den


def _ada_kernel(c_ref, w_ref, b_ref, o_ref):
    c = c_ref[...]
    a = c * jax.nn.sigmoid(c)
    o_ref[...] = jnp.dot(a, w_ref[...], precision=lax.Precision.HIGHEST, preferred_element_type=F32) + b_ref[...]


def _ada(cond, w_ada, b_ada):
    n, d = cond.shape
    nout = w_ada.shape[1]
    bn = 512
    return pl.pallas_call(
        _ada_kernel,
        grid=(nout // bn,),
        in_specs=[pl.BlockSpec((n, d), lambda j: (0, 0)),
                  pl.BlockSpec((d, bn), lambda j: (0, j)),
                  pl.BlockSpec((1, bn), lambda j: (0, j))],
        out_specs=pl.BlockSpec((n, bn), lambda j: (0, j)),
        out_shape=jax.ShapeDtypeStruct((n, nout), F32),
        compiler_params=_cparams(("arbitrary",)),
        name="ada",
    )(cond, w_ada, b_ada.reshape(1, nout))


def _mods(c, c_ctx, w_ada, b_ada):
    b = c.shape[0]
    rows = ((b + 1 + 7) // 8) * 8
    cond = jnp.zeros((rows, D_MODEL), F32).at[:b].set(c).at[b].set(c_ctx)
    out = _ada(cond, w_ada, b_ada)
    lat = out[:b].reshape(b, 1, 6, D_MODEL)
    cx = jnp.broadcast_to(out[b].reshape(1, 1, 6, D_MODEL), (b, 1, 6, D_MODEL))
    return jnp.concatenate([cx, lat], axis=1)


def _axial_angles(n_tokens, rot_dim):
    t = np.arange(n_tokens)
    row = (t // GRID_W).astype(np.float32)
    col = (t % GRID_W).astype(np.float32)
    n_axis = rot_dim // 4
    inv_freq = (np.float32(ROPE_THETA) ** (-np.arange(n_axis, dtype=np.float32) / n_axis)).astype(np.float32)
    return jnp.concatenate([jnp.asarray(row[:, None] * inv_freq), jnp.asarray(col[:, None] * inv_freq)], axis=-1)


def _rope_tables(seq, rot_dim, lane_start, period, scale):
    ang = _axial_angles(seq, rot_dim)
    cos = jnp.repeat(jnp.cos(ang), 2, axis=-1)
    sin = jnp.repeat(jnp.sin(ang), 2, axis=-1) * jnp.tile(jnp.asarray([-1.0, 1.0], F32), rot_dim // 2)
    cos_p = jnp.ones((seq, period), F32).at[:, lane_start:lane_start + rot_dim].set(cos)
    sin_p = jnp.zeros((seq, period), F32).at[:, lane_start:lane_start + rot_dim].set(sin)
    cos_f = jnp.concatenate([jnp.ones((CTX_LEN, period), F32), cos_p], axis=0)
    sin_f = jnp.concatenate([jnp.zeros((CTX_LEN, period), F32), sin_p], axis=0)
    reps = LANES // period
    return jnp.tile(cos_f, (1, reps)) * scale, jnp.tile(sin_f, (1, reps)) * scale


def _na_bias_table(rpb):
    h = rpb.shape[0]
    qc = np.arange(GRID_W)
    kc = np.arange(GRID_W)
    cstart = np.clip(qc - NA_COLS // 2, 0, GRID_W - NA_COLS)
    valid = (kc[None, :] >= cstart[:, None]) & (kc[None, :] < cstart[:, None] + NA_COLS)
    dc = np.clip(kc[None, :] - qc[:, None] + NA_COLS - 1, 0, 2 * NA_COLS - 2)
    tabs = []
    for v in range(NA_ROWS):
        dr = np.clip(np.arange(NA_ROWS) - v + NA_ROWS - 1, 0, 2 * NA_ROWS - 2)
        b = rpb[:, dr][:, :, dc]
        b = jnp.where(jnp.asarray(valid)[None, None], b.astype(F32), NEG_INF)
        tabs.append(jnp.transpose(b, (0, 2, 1, 3)).reshape(h, GRID_W, NA_ROWS * GRID_W))
    tabs.append(jnp.full((h, GRID_W, NA_ROWS * GRID_W), NEG_INF, F32))
    loc = jnp.stack(tabs, axis=0)
    return jnp.concatenate([jnp.zeros(loc.shape[:3] + (CTX_LEN,), F32), loc], axis=-1)


def _dup_heads(w, n_heads):
    d = w.shape[0]
    w = w.reshape(d, n_heads, 1, HEAD_DIM)
    return jnp.broadcast_to(w, (d, n_heads, 2, HEAD_DIM)).reshape(d, n_heads * 2 * HEAD_DIM)


def _proj0_kernel(x_ref, mod_ref, g_ref, w_ref, cos_ref, sin_ref, qa_ref, kva_ref, qb_ref, kb_ref, vb_ref):
    h = _norm_mod(x_ref[0], g_ref[...], mod_ref[0, 0, 0:1, :], mod_ref[0, 0, 1:2, :])
    r = jnp.dot(h.astype(BF16), w_ref[...], preferred_element_type=F32)
    cos, sin = cos_ref[...], sin_ref[...]
    roped = [_rope128(r[:, i * LANES:(i + 1) * LANES], cos, sin) for i in range(6)]
    qa_ref[0] = jnp.concatenate(roped[0:4], axis=1).astype(BF16)
    kva_ref[0] = jnp.concatenate(roped[4:6] + [r[:, 768:1024]], axis=1).astype(BF16)
    qb_ref[0] = r[:, 1024:1536].astype(BF16)
    kb_ref[0] = r[:, 1536:2048].astype(BF16)
    vb_ref[0] = r[:, 2048:2560].astype(BF16)


def _proj0(xa, mods, g_mix, w_in, cos, sin):
    b, u, d = xa.shape
    tm = TOK_TILE
    s = 1.0 / 8.0
    w = jnp.concatenate([w_in[:, 0:512] * s, _dup_heads(w_in[:, 512:640], 2), _dup_heads(w_in[:, 640:768], 2),
                         w_in[:, 768:1280] * s, w_in[:, 1280:1792], w_in[:, 1792:2304]], axis=1).astype(BF16)
    nw = w.shape[1]
    tok = lambda n: pl.BlockSpec((1, tm, n), lambda i, t: (i, t, 0))
    return pl.pallas_call(
        _proj0_kernel,
        grid=(b, u // tm),
        in_specs=[tok(d),
                  pl.BlockSpec((1, 1, 6, d), lambda i, t: (i, jnp.minimum(t, 1), 0, 0)),
                  pl.BlockSpec((1, d), lambda i, t: (0, 0)),
                  pl.BlockSpec((d, nw), lambda i, t: (0, 0)),
                  pl.BlockSpec((tm, LANES), lambda i, t: (t, 0)),
                  pl.BlockSpec((tm, LANES), lambda i, t: (t, 0))],
        out_specs=[tok(512)] * 5,
        out_shape=[jax.ShapeDtypeStruct((b, u, 512), BF16)] * 5,
        compiler_params=_cparams(("arbitrary", "arbitrary")),
        name="proj0",
    )(xa, mods, g_mix.reshape(1, d), w, cos, sin)


def _window_kernel(sink_ref, q_ref, kv_ref, o_ref, *, tq, span):
    n = pl.program_id(1)
    u = kv_ref.shape[1]
    ls = pl.multiple_of(jnp.clip((n - 1) * tq, CTX_LEN, u - span), tq)
    qpos = n * tq + lax.broadcasted_iota(I32, (tq, span), 0)
    kpos = ls + lax.broadcasted_iota(I32, (tq, span), 1)
    valid = (kpos >= CTX_LEN) & (jnp.abs(qpos - kpos) <= A_WINDOW) & (qpos >= CTX_LEN)
    kv_c = kv_ref[0, 0:CTX_LEN, :]
    kv_l = kv_ref[0, pl.ds(ls, span), :]
    outs = [None] * 8
    for g in range(2):
        kc = kv_c[:, g * LANES:(g + 1) * LANES]
        kl = kv_l[:, g * LANES:(g + 1) * LANES]
        v_all = jnp.concatenate([kv_c[:, 256 + g * LANES:256 + (g + 1) * LANES],
                                 kv_l[:, 256 + g * LANES:256 + (g + 1) * LANES]], axis=0)
        for hh in range(4):
            h = 4 * g + hh
            q = _keep_half(q_ref[0, :, (h // 2) * LANES:(h // 2 + 1) * LANES], h % 2)
            s_l = jnp.where(valid, _qk(q, kl), NEG_INF)
            s = jnp.concatenate([_qk(q, kc), s_l], axis=1)
            outs[h] = _softmax_pv(s, v_all, sink_ref[h])
    o_ref[0] = jnp.concatenate([_merge_halves(outs[2 * p], outs[2 * p + 1]) for p in range(4)], axis=1).astype(BF16)


def _window_attention(qa, kva, sink):
    b, u, _ = qa.shape
    tq = A_WINDOW
    span = 3 * A_WINDOW
    return pl.pallas_call(
        functools.partial(_window_kernel, tq=tq, span=span),
        grid=(b, u // tq),
        in_specs=[pl.BlockSpec(memory_space=pltpu.SMEM),
                  pl.BlockSpec((1, tq, 512), lambda i, n: (i, n, 0)),
                  pl.BlockSpec((1, u, 512), lambda i, n: (i, 0, 0))],
        out_specs=pl.BlockSpec((1, tq, 512), lambda i, n: (i, n, 0)),
        out_shape=jax.ShapeDtypeStruct((b, u, 512), BF16),
        compiler_params=_cparams(("arbitrary", "arbitrary")),
        name="window_attn",
    )(sink.astype(F32), qa, kva)


def _na_variant(j, rows):
    r = jnp.maximum(j - CTX_LEN // GRID_W, 0)
    rs = jnp.clip(r - NA_ROWS // 2, 0, rows - NA_ROWS)
    return r, rs


def _na_kernel(q_ref, k_ref, v_ref, bias_ref, o_ref, *, rows):
    j = pl.program_id(1)
    _, rs = _na_variant(j, rows)
    band = NA_ROWS * GRID_W
    start = pl.multiple_of(CTX_LEN + rs * GRID_W, GRID_W)
    k_all = jnp.concatenate([k_ref[0, 0:CTX_LEN, :], k_ref[0, pl.ds(start, band), :]], axis=0)
    v_all = jnp.concatenate([v_ref[0, 0:CTX_LEN, :], v_ref[0, pl.ds(start, band), :]], axis=0)
    outs = []
    for p in range(4):
        q128 = q_ref[0, :, p * LANES:(p + 1) * LANES]
        k128 = k_all[:, p * LANES:(p + 1) * LANES]
        v128 = v_all[:, p * LANES:(p + 1) * LANES]
        pair = []
        for half in range(2):
            s = _qk(_keep_half(q128, half), k128) + bias_ref[0, 2 * p + half]
            pair.append(_softmax_pv(s, v128))
        outs.append(_merge_halves(pair[0], pair[1]))
    o_ref[0] = jnp.concatenate(outs, axis=1).astype(BF16)


def _na_attention(qb, kb, vb, bias_tab):
    b, u, _ = qb.shape
    rows = (u - CTX_LEN) // GRID_W
    assert rows >= NA_ROWS
    nk = bias_tab.shape[-1]

    def bias_map(i, j):
        r, rs = _na_variant(j, rows)
        return (jnp.where(j < CTX_LEN // GRID_W, NA_ROWS, r - rs), 0, 0, 0)

    return pl.pallas_call(
        functools.partial(_na_kernel, rows=rows),
        grid=(b, u // GRID_W),
        in_specs=[pl.BlockSpec((1, GRID_W, 512), lambda i, j: (i, j, 0)),
                  pl.BlockSpec((1, u, 512), lambda i, j: (i, 0, 0)),
                  pl.BlockSpec((1, u, 512), lambda i, j: (i, 0, 0)),
                  pl.BlockSpec((1, 8, GRID_W, nk), bias_map)],
        out_specs=pl.BlockSpec((1, GRID_W, 512), lambda i, j: (i, j, 0)),
        out_shape=jax.ShapeDtypeStruct((b, u, 512), BF16),
        compiler_params=_cparams(("arbitrary", "arbitrary")),
        name="na_attn",
    )(qb, kb, vb, bias_tab)


def _proj1_kernel(x_ref, mod_ref, g_ref, w_ref, gq_ref, wuq_ref, gkv_ref, wkv_ref, gqn_ref, gkn_ref, ones_ref,
                  cosa_ref, sina_ref, cosq_ref, sinq_ref, cosk_ref, sink_ref,
                  qm_ref, qd_ref, km_ref, vm_ref, kvd_ref):
    h = _norm_mod(x_ref[0], g_ref[...], mod_ref[0, 0, 0:1, :], mod_ref[0, 0, 1:2, :])
    r = jnp.dot(h.astype(BF16), w_ref[...], preferred_element_type=F32)
    ones_bd = ones_ref[...]
    cosa, sina = cosa_ref[...], sina_ref[...]
    cq = _rms(r[:, 0:C_Q_LORA], gq_ref[...])
    qm = jnp.dot(cq.astype(BF16), wuq_ref[...], preferred_element_type=F32)
    cosq, sinq = cosq_ref[...], sinq_ref[...]
    qm_ref[0] = jnp.concatenate([_rope128(qm[:, i * LANES:(i + 1) * LANES], cosq, sinq) for i in range(8)],
                                axis=1).astype(BF16)
    gqn = gqn_ref[...]
    qd_ref[0] = jnp.concatenate(
        [_rope128(_head_norm(r[:, 384 + i * LANES:384 + (i + 1) * LANES], ones_bd, gqn), cosa, sina)
         for i in range(4)], axis=1).astype(BF16)
    ckv = _rms(r[:, 896:1152], gkv_ref[...])
    kr = _rope128(r[:, 1152:1280], cosk_ref[...], sink_ref[...])
    kv = jnp.dot(jnp.concatenate([ckv, kr], axis=1).astype(BF16), wkv_ref[...], preferred_element_type=F32)
    km_ref[0] = kv[:, 0:1024].astype(BF16)
    vm_ref[0] = kv[:, 1024:1536].astype(BF16)
    gkn = gkn_ref[...]
    kd = [_rope128(_head_norm(r[:, 1280 + i * LANES:1280 + (i + 1) * LANES], ones_bd, gkn), cosa, sina)
          for i in range(2)]
    kvd_ref[0] = jnp.concatenate(kd + [r[:, 1536:1792]], axis=1).astype(BF16)


def _proj1(xa, mods, g_mix, w_in, g_q_lora, w_uq, g_kv_lora, w_ukv, g_qn, g_kn, tabs):
    b, u, d = xa.shape
    tm = TOK_TILE
    zpad = jnp.zeros((d, LANES - C_ROPE), F32)
    w = jnp.concatenate([w_in[:, 0:896], w_in[:, 896:1152], w_in[:, 1152:1184], zpad,
                         _dup_heads(w_in[:, 1184:1312], 2), _dup_heads(w_in[:, 1312:1440], 2)], axis=1).astype(BF16)
    nw = w.shape[1]
    wuq = w_uq.reshape(C_Q_LORA, 8, C_NOPE + C_ROPE)
    wuq = jnp.concatenate([wuq, jnp.zeros((C_Q_LORA, 8, LANES - C_NOPE - C_ROPE), F32)], axis=-1)
    wuq = wuq.reshape(C_Q_LORA, 8 * LANES).astype(BF16)
    wukv = w_ukv.reshape(C_KV_LORA, 8, C_NOPE + C_V)
    wk = jnp.concatenate([wukv[:, :, :C_NOPE], jnp.zeros((C_KV_LORA, 8, LANES - C_NOPE), F32)], axis=-1)
    wk = wk.reshape(C_KV_LORA, 8 * LANES)
    wv = wukv[:, :, C_NOPE:].reshape(C_KV_LORA, 8 * C_V)
    place = np.zeros((LANES, 8, LANES), np.float32)
    for j in range(C_ROPE):
        place[j, :, C_NOPE + j] = 1.0
    place = jnp.asarray(place.reshape(LANES, 8 * LANES))
    wkv = jnp.concatenate([jnp.concatenate([wk, wv], axis=1),
                           jnp.concatenate([place, jnp.zeros((LANES, 8 * C_V), F32)], axis=1)], axis=0).astype(BF16)
    ones_bd = jnp.asarray(np.kron(np.eye(2, dtype=np.float32), np.ones((HEAD_DIM, HEAD_DIM), np.float32))).astype(BF16)
    gqn = jnp.tile(g_qn.astype(F32) * (1.0 / 8.0), 2).reshape(1, LANES)
    gkn = jnp.tile(g_kn.astype(F32), 2).reshape(1, LANES)
    cosa, sina, cosq, sinq, cosk, sink = tabs
    tok = lambda n: pl.BlockSpec((1, tm, n), lambda i, t: (i, t, 0))
    whole = lambda a: pl.BlockSpec(a.shape, lambda i, t: (0,) * a.ndim)
    tab = pl.BlockSpec((tm, LANES), lambda i, t: (t, 0))
    small = [g_mix.reshape(1, d), w, g_q_lora.reshape(1, -1).astype(F32), wuq, g_kv_lora.reshape(1, -1).astype(F32),
             wkv, gqn, gkn, ones_bd]
    return pl.pallas_call(
        _proj1_kernel,
        grid=(b, u // tm),
        in_specs=[tok(d), pl.BlockSpec((1, 1, 6, d), lambda i, t: (i, jnp.minimum(t, 1), 0, 0))]
                 + [whole(a) for a in small] + [tab] * 6,
        out_specs=[tok(1024), tok(512), tok(1024), tok(512), tok(512)],
        out_shape=[jax.ShapeDtypeStruct((b, u, n), BF16) for n in (1024, 512, 1024, 512, 512)],
        compiler_params=_cparams(("arbitrary", "arbitrary")),
        name="proj1",
    )(xa, mods, *small, cosa, sina, cosq, sinq, cosk, sink)


def _mla_kernel(q_ref, k_ref, v_ref, o_ref):
    outs = []
    for h in range(8):
        s = _qk(q_ref[0, :, h * LANES:(h + 1) * LANES], k_ref[0, :, h * LANES:(h + 1) * LANES])
        outs.append(_softmax_pv(s, v_ref[0, :, (h // 2) * LANES:(h // 2 + 1) * LANES]))
    o_ref[0] = jnp.concatenate([_merge_halves(outs[2 * p], outs[2 * p + 1]) for p in range(4)], axis=1).astype(BF16)


def _mla_attention(qm, km, vm):
    b, u, _ = qm.shape
    tq = 256
    nq = (u - CTX_LEN) // tq
    off = CTX_LEN // tq
    return pl.pallas_call(
        _mla_kernel,
        grid=(b, nq),
        in_specs=[pl.BlockSpec((1, tq, 1024), lambda i, n: (i, n + off, 0)),
                  pl.BlockSpec((1, u, 1024), lambda i, n: (i, 0, 0)),
                  pl.BlockSpec((1, u, 512), lambda i, n: (i, 0, 0))],
        out_specs=pl.BlockSpec((1, tq, 512), lambda i, n: (i, n, 0)),
        out_shape=jax.ShapeDtypeStruct((b, u - CTX_LEN, 512), BF16),
        compiler_params=_cparams(("arbitrary", "arbitrary")),
        name="mla_attn",
    )(qm, km, vm)


def _gqa_kernel(q_ref, kv_ref, o_ref):
    outs = [None] * 8
    for g in range(2):
        k = kv_ref[0, :, g * LANES:(g + 1) * LANES]
        v = kv_ref[0, :, 256 + g * LANES:256 + (g + 1) * LANES]
        for hh in range(4):
            h = 4 * g + hh
            q = _keep_half(q_ref[0, :, (h // 2) * LANES:(h // 2 + 1) * LANES], h % 2)
            outs[h] = _softmax_pv(_qk(q, k), v)
    o_ref[0] = jnp.concatenate([_merge_halves(outs[2 * p], outs[2 * p + 1]) for p in range(4)], axis=1).astype(BF16)


def _gqa_attention(qd, kvd):
    b, u, _ = qd.shape
    tq = 256
    nq = (u - CTX_LEN) // tq
    off = CTX_LEN // tq
    return pl.pallas_call(
        _gqa_kernel,
        grid=(b, nq),
        in_specs=[pl.BlockSpec((1, tq, 512), lambda i, n: (i, n + off, 0)),
                  pl.BlockSpec((1, u, 512), lambda i, n: (i, 0, 0))],
        out_specs=pl.BlockSpec((1, tq, 512), lambda i, n: (i, n, 0)),
        out_shape=jax.ShapeDtypeStruct((b, u - CTX_LEN, 512), BF16),
        compiler_params=_cparams(("arbitrary", "arbitrary")),
        name="gqa_attn",
    )(qd, kvd)


def _post_attn_kernel(ya_ref, yb_ref, w_ref, x_ref, mod_ref, g_ref, wr_ref, br_ref,
                      xmid_ref, h2_ref, slot_ref, wts_ref, slott_ref, gtab_ref, tot_ref, run_ref):
    first = (pl.program_id(0) == 0) & (pl.program_id(1) == 0)

    @pl.when(first)
    def _():
        run_ref[...] = jnp.zeros_like(run_ref)

    y = (jnp.dot(ya_ref[0], w_ref[0:512, :], preferred_element_type=F32)
         + jnp.dot(yb_ref[0], w_ref[512:1024, :], preferred_element_type=F32))
    x1 = x_ref[0] + mod_ref[0, 0, 2:3, :] * y
    xmid_ref[0] = x1
    h2 = _norm_mod(x1, g_ref[...], mod_ref[0, 0, 3:4, :], mod_ref[0, 0, 4:5, :])
    h2_ref[...] = h2.astype(BF16)

    logits = jnp.dot(h2, wr_ref[...], precision=lax.Precision.HIGHEST, preferred_element_type=F32)
    scores = jax.nn.sigmoid(logits)
    tm = scores.shape[0]
    lane = lax.broadcasted_iota(I32, (tm, LANES), 1).astype(F32)
    biased = jnp.where(lane < N_EXPERTS, scores + br_ref[...], -jnp.inf)
    picked = jnp.zeros((tm, LANES), F32)
    sel_idx, sel_val = [], []
    for _k in range(TOP_K):
        m = jnp.max(biased, axis=-1, keepdims=True)
        i_k = jnp.min(jnp.where(biased == m, lane, float(LANES)), axis=-1, keepdims=True)
        hit = lane == i_k
        sel_idx.append(i_k)
        sel_val.append(jnp.sum(jnp.where(hit, scores, 0.0), axis=-1, keepdims=True))
        picked = jnp.where(hit, 1.0, picked)
        biased = jnp.where(hit, -jnp.inf, biased)
    total = sel_val[0]
    for v in sel_val[1:]:
        total = total + v
    cnt = jnp.sum(picked, axis=0, keepdims=True)
    ng = jnp.floor((cnt + float(GRAN - 1)) * (1.0 / GRAN))
    e_i = lax.broadcasted_iota(I32, (LANES, LANES), 0)
    e_j = lax.broadcasted_iota(I32, (LANES, LANES), 1)
    upper = jnp.where(e_i < e_j, 1.0, 0.0).astype(BF16)
    loffg = jnp.dot(jnp.broadcast_to(ng, (8, LANES)).astype(BF16), upper, preferred_element_type=F32)[0:1]
    r_i = lax.broadcasted_iota(I32, (tm, tm), 0)
    c_i = lax.broadcasted_iota(I32, (tm, tm), 1)
    before = jnp.dot(jnp.where(c_i < r_i, 1.0, 0.0).astype(BF16), picked.astype(BF16), preferred_element_type=F32)
    base = before + float(GRAN) * loffg
    slot_o = jnp.zeros((tm, LANES), F32)
    wts_o = jnp.zeros((tm, LANES), F32)
    for k in range(TOP_K):
        slot_k = jnp.sum(jnp.where(lane == sel_idx[k], base, 0.0), axis=-1, keepdims=True)
        slot_o = jnp.where(lane == float(k), slot_k, slot_o)
        wts_o = jnp.where(lane == float(k), ROUTED_SCALE * sel_val[k] / total, wts_o)
    slot_ref[...] = slot_o[:, :KPAD].astype(I32)
    wts_ref[...] = wts_o[:, :KPAD]
    slott_ref[0] = slot_o.T[:KPAD, :].astype(I32)

    run = run_ref[...]

    def as_col(v):
        return jnp.sum(jnp.where(e_i == e_j, jnp.broadcast_to(v, (LANES, LANES)), 0.0), axis=1, keepdims=True)

    end_c = as_col(loffg + ng)
    val_c = as_col(run - loffg)
    jj = lax.broadcasted_iota(I32, (LANES, NGRAN), 1).astype(F32)
    ee = lax.broadcasted_iota(I32, (LANES, NGRAN), 0).astype(F32)
    e_of_j = jnp.sum(jnp.where(end_c <= jj, 1.0, 0.0), axis=0, keepdims=True)
    dst = jnp.sum(jnp.where(ee == e_of_j, val_c + jj, 0.0), axis=0, keepdims=True)
    used = jnp.broadcast_to(jnp.sum(ng, axis=-1, keepdims=True), (1, NGRAN))
    gtab_ref[0] = jnp.concatenate([e_of_j, dst, used, jnp.zeros((5, NGRAN), F32)], axis=0).astype(I32)
    new_run = run + ng
    run_ref[...] = new_run
    tot_ref[...] = new_run


def _post_attn(ya, yb, w_out, x, x_tile_off, mods, mod_sel, g_moe, w_router, b_router):
    b, n, _ = ya.shape
    d = D_MODEL
    tm = TOK_TILE
    nt = n // tm
    t_tot = b * n
    wr = jnp.zeros((d, LANES), F32).at[:, :N_EXPERTS].set(w_router.astype(F32))
    br = jnp.zeros((1, LANES), F32).at[0, :N_EXPERTS].set(b_router.astype(F32))
    flat = lambda c: pl.BlockSpec((tm, c), lambda i, t: (i * nt + t, 0))
    outs = pl.pallas_call(
        _post_attn_kernel,
        grid=(b, nt),
        in_specs=[pl.BlockSpec((1, tm, 512), lambda i, t: (i, t, 0)),
                  pl.BlockSpec((1, tm, 512), lambda i, t: (i, t, 0)),
                  pl.BlockSpec((d, d), lambda i, t: (0, 0)),
                  pl.BlockSpec((1, tm, d), lambda i, t: (i, t + x_tile_off, 0)),
                  pl.BlockSpec((1, 1, 6, d), lambda i, t: (i, mod_sel(t), 0, 0)),
                  pl.BlockSpec((1, d), lambda i, t: (0, 0)),
                  pl.BlockSpec((d, LANES), lambda i, t: (0, 0)),
                  pl.BlockSpec((1, LANES), lambda i, t: (0, 0))],
        out_specs=[pl.BlockSpec((1, tm, d), lambda i, t: (i, t, 0)),
                   flat(d), flat(KPAD), flat(KPAD),
                   pl.BlockSpec((1, KPAD, tm), lambda i, t: (i * nt + t, 0, 0)),
                   pl.BlockSpec((1, 8, NGRAN), lambda i, t: (i * nt + t, 0, 0)),
                   pl.BlockSpec((1, LANES), lambda i, t: (0, 0))],
        out_shape=[jax.ShapeDtypeStruct((b, n, d), F32),
                   jax.ShapeDtypeStruct((t_tot, d), BF16),
                   jax.ShapeDtypeStruct((t_tot, KPAD), I32),
                   jax.ShapeDtypeStruct((t_tot, KPAD), F32),
                   jax.ShapeDtypeStruct((b * nt, KPAD, tm), I32),
                   jax.ShapeDtypeStruct((b * nt, 8, NGRAN), I32),
                   jax.ShapeDtypeStruct((1, LANES), F32)],
        scratch_shapes=[pltpu.VMEM((1, LANES), F32)],
        compiler_params=_cparams(("arbitrary", "arbitrary")),
        name="post_attn_route",
    )(ya, yb, w_out.astype(BF16), x, mods, g_moe.reshape(1, d), wr, br)
    return outs


def _granule_copy(src, src_g, dst, dst_g, sem):
    return pltpu.make_async_copy(src.at[pl.ds(pl.multiple_of(src_g * GRAN, GRAN), GRAN), :],
                                 dst.at[pl.ds(pl.multiple_of(dst_g * GRAN, GRAN), GRAN), :], sem)


def _drain(src, dst, sem, n):
    def body(j, carry):
        _granule_copy(src, 0, dst, 0, sem).wait()
        return carry

    lax.fori_loop(0, n, body, 0)


def _dispatch_kernel(poffg_ref, gtab_ref, h_ref, slott_ref, xs_ref, xloc, gprev, sems, *, nt):
    i = pl.program_id(0)
    cur = i % 2
    used = gtab_ref[0, 2, 0]
    slott = slott_ref[0]
    h = h_ref[...]
    tm = h.shape[0]
    half = h.shape[1] // 2
    for c in range(SORT_ROWS // CHUNK):
        @pl.when(c * CHUNK < used * GRAN)
        def _():
            rows = c * CHUNK + lax.broadcasted_iota(I32, (CHUNK, tm), 0)
            p = jnp.zeros((CHUNK, tm), F32)
            for k in range(TOP_K):
                p = jnp.where(rows == slott[k:k + 1, :], 1.0, p)
            xc = jnp.dot(p.astype(BF16), h, preferred_element_type=F32)
            lo = lax.bitcast_convert_type(xc[:, :half], U32)
            hi = lax.bitcast_convert_type(xc[:, half:], U32)
            xloc[cur, c * CHUNK:(c + 1) * CHUNK, :] = (hi & jnp.uint32(0xFFFF0000)) | (lo >> 16)

    @pl.when(i > 0)
    def _():
        _drain(xloc.at[1 - cur], xs_ref, sems.at[1 - cur], gprev[0])

    def issue(j, carry):
        dst_g = poffg_ref[gtab_ref[0, 0, j]] + gtab_ref[0, 1, j]
        _granule_copy(xloc.at[cur], j, xs_ref, dst_g, sems.at[cur]).start()
        return carry

    lax.fori_loop(0, used, issue, 0)
    gprev[0] = used

    @pl.when(i == nt - 1)
    def _():
        _drain(xloc.at[cur], xs_ref, sems.at[cur], used)


def _dispatch(h2, slott, gtab, poffg, n_slots):
    t_tot, d = h2.shape
    tm = TOK_TILE
    nt = t_tot // tm
    return pl.pallas_call(
        functools.partial(_dispatch_kernel, nt=nt),
        grid=(nt,),
        in_specs=[pl.BlockSpec(memory_space=pltpu.SMEM),
                  pl.BlockSpec((1, 8, NGRAN), lambda i: (i, 0, 0), memory_space=pltpu.SMEM),
                  pl.BlockSpec((tm, d), lambda i: (i, 0)),
                  pl.BlockSpec((1, KPAD, tm), lambda i: (i, 0, 0))],
        out_specs=pl.BlockSpec(memory_space=pl.ANY),
        out_shape=jax.ShapeDtypeStruct((n_slots, d // 2), U32),
        scratch_shapes=[pltpu.VMEM((2, SORT_ROWS, d // 2), U32), pltpu.SMEM((1,), I32),
                        pltpu.SemaphoreType.DMA((2,))],
        compiler_params=_cparams(("arbitrary",)),
        name="moe_dispatch",
    )(poffg, gtab, h2, slott)


def _expert_kernel(te_ref, nused_ref, xs_ref, wg_ref, wu_ref, wd_ref, ys_ref, wgu_s, wd_s):
    i = pl.program_id(0)
    changed = (i == 0) | (te_ref[i] != te_ref[jnp.maximum(i - 1, 0)])

    @pl.when(changed)
    def _():
        wgu_s[:, 0:EXPERT_FF] = wg_ref[0].astype(BF16)
        wgu_s[:, EXPERT_FF:2 * EXPERT_FF] = wu_ref[0].astype(BF16)
        wd_s[...] = wd_ref[0].astype(BF16)

    @pl.when(i < nused_ref[0])
    def _():
        lo, hi = _unpack_bf16_pair(xs_ref[...])
        x = jnp.concatenate([lo.astype(BF16), hi.astype(BF16)], axis=1)
        gu = jnp.dot(x, wgu_s[...], preferred_element_type=F32)
        gate, up = gu[:, :EXPERT_FF], gu[:, EXPERT_FF:]
        act = gate * jax.nn.sigmoid(gate) * up
        y = jnp.dot(act.astype(BF16), wd_s[...], preferred_element_type=F32)
        half = y.shape[1] // 2
        ys_ref[...] = _pack_bf16_pair(y[:, :half], y[:, half:])


def _expert_ffn(xs, tile_expert, n_used, w_gate, w_up, w_down):
    n_slots, w = xs.shape
    ts = SLOT_TILE
    d, f = w_gate.shape[1], w_gate.shape[2]
    grid_spec = pltpu.PrefetchScalarGridSpec(
        num_scalar_prefetch=2,
        grid=(n_slots // ts,),
        in_specs=[pl.BlockSpec((ts, w), lambda i, te, nu: (i, 0)),
                  pl.BlockSpec((1, d, f), lambda i, te, nu: (te[i], 0, 0)),
                  pl.BlockSpec((1, d, f), lambda i, te, nu: (te[i], 0, 0)),
                  pl.BlockSpec((1, f, d), lambda i, te, nu: (te[i], 0, 0))],
        out_specs=pl.BlockSpec((ts, w), lambda i, te, nu: (i, 0)),
        scratch_shapes=[pltpu.VMEM((d, 2 * f), BF16), pltpu.VMEM((f, d), BF16)],
    )
    return pl.pallas_call(
        _expert_kernel,
        grid_spec=grid_spec,
        out_shape=jax.ShapeDtypeStruct((n_slots, w), U32),
        compiler_params=_cparams(("arbitrary",)),
        name="moe_experts",
    )(tile_expert, n_used, xs, w_gate, w_up, w_down)


def _combine_kernel(poffg_ref, gtab_ref, gnext_ref, ys_ref, slot_ref, wts_ref, h_ref, x_ref, mod_ref, wsgu_ref,
                    wsd_ref, gf_ref, o_ref, yloc, acc_ref, sems, *, nt, final_norm):
    i = pl.program_id(0)
    cur = i % 2

    def fetch(tab_ref, buf):
        def issue(j, carry):
            src_g = poffg_ref[tab_ref[0, 0, j]] + tab_ref[0, 1, j]
            _granule_copy(ys_ref, src_g, yloc.at[buf], j, sems.at[buf]).start()
            return carry

        lax.fori_loop(0, tab_ref[0, 2, 0], issue, 0)

    @pl.when(i == 0)
    def _():
        fetch(gtab_ref, 0)

    @pl.when(i + 1 < nt)
    def _():
        fetch(gnext_ref, 1 - cur)

    h = h_ref[...]
    tm = h.shape[0]
    gu = jnp.dot(h, wsgu_ref[...], preferred_element_type=F32)
    gate, up = gu[:, :SHARED_FF], gu[:, SHARED_FF:]
    acc_ref[...] = jnp.dot((gate * jax.nn.sigmoid(gate) * up).astype(BF16), wsd_ref[...],
                           preferred_element_type=F32)

    used = gtab_ref[0, 2, 0]
    _drain(ys_ref, yloc.at[cur], sems.at[cur], used)

    slot = slot_ref[...]
    wts = wts_ref[...]
    w = yloc.shape[2]
    for c in range(SORT_ROWS // CHUNK):
        @pl.when(c * CHUNK < used * GRAN)
        def _():
            cols = c * CHUNK + lax.broadcasted_iota(I32, (tm, CHUNK), 1)
            pw = jnp.zeros((tm, CHUNK), F32)
            for k in range(TOP_K):
                pw = jnp.where(cols == slot[:, k:k + 1], wts[:, k:k + 1], pw)
            rows = c * CHUNK + lax.broadcasted_iota(I32, (CHUNK, w), 0)
            packed = jnp.where(rows < used * GRAN, yloc[cur, c * CHUNK:(c + 1) * CHUNK, :], jnp.uint32(0))
            lo, hi = _unpack_bf16_pair(packed)
            y = jnp.concatenate([lo.astype(BF16), hi.astype(BF16)], axis=1)
            acc_ref[...] += jnp.dot(pw.astype(BF16), y, preferred_element_type=F32)

    out = x_ref[...] + mod_ref[0, 0, 5:6, :] * acc_ref[...]
    if final_norm:
        out = _rms(out, gf_ref[...])
    o_ref[...] = out


def _combine(ys, slot, wts, gtab, poffg, h2, xmid, mods, mod_map, ws_gate, ws_up, ws_down, g_final, final_norm):
    t_tot, d = h2.shape
    tm = TOK_TILE
    nt = t_tot // tm
    wsgu = jnp.concatenate([ws_gate, ws_up], axis=1).astype(BF16)
    return pl.pallas_call(
        functools.partial(_combine_kernel, nt=nt, final_norm=final_norm),
        grid=(nt,),
        in_specs=[pl.BlockSpec(memory_space=pltpu.SMEM),
                  pl.BlockSpec((1, 8, NGRAN), lambda i: (i, 0, 0), memory_space=pltpu.SMEM),
                  pl.BlockSpec((1, 8, NGRAN), lambda i: (jnp.minimum(i + 1, nt - 1), 0, 0), memory_space=pltpu.SMEM),
                  pl.BlockSpec(memory_space=pl.ANY),
                  pl.BlockSpec((tm, KPAD), lambda i: (i, 0)),
                  pl.BlockSpec((tm, KPAD), lambda i: (i, 0)),
                  pl.BlockSpec((tm, d), lambda i: (i, 0)),
                  pl.BlockSpec((tm, d), lambda i: (i, 0)),
                  pl.BlockSpec((1, 1, 6, d), mod_map),
                  pl.BlockSpec((d, 2 * SHARED_FF), lambda i: (0, 0)),
                  pl.BlockSpec((SHARED_FF, d), lambda i: (0, 0)),
                  pl.BlockSpec((1, d), lambda i: (0, 0))],
        out_specs=pl.BlockSpec((tm, d), lambda i: (i, 0)),
        out_shape=jax.ShapeDtypeStruct((t_tot, d), F32),
        scratch_shapes=[pltpu.VMEM((2, SORT_ROWS, d // 2), U32), pltpu.VMEM((tm, d), F32),
                        pltpu.SemaphoreType.DMA((2,))],
        compiler_params=_cparams(("arbitrary",)),
        name="moe_combine",
    )(poffg, gtab, gtab, ys, slot, wts, h2, xmid.reshape(t_tot, d), mods, wsgu, ws_down.astype(BF16),
      g_final.reshape(1, d).astype(F32))


def _moe(h2, slot, wts, slott, gtab, tot, xmid, mods, mod_map, moe_w, g_final, final_norm):
    (w_gate, w_up, w_down, ws_gate, ws_up, ws_down) = moe_w
    t_tot = h2.shape[0]
    ts = SLOT_TILE
    gpt = ts // GRAN
    nt = t_tot // TOK_TILE
    max_rows = t_tot * TOP_K + nt * N_EXPERTS * (GRAN - 1)
    n_tiles = -(-max_rows // ts) + N_EXPERTS
    totg = tot[0, :N_EXPERTS].astype(I32)
    tiles_e = (totg + gpt - 1) // gpt
    ends = jnp.cumsum(tiles_e)
    poffg = jnp.zeros((LANES,), I32).at[:N_EXPERTS].set((ends - tiles_e) * gpt)
    n_used = ends[-1:]
    tile_ids = jnp.minimum(jnp.arange(n_tiles, dtype=I32), n_used[0] - 1)
    tile_expert = jnp.sum((ends[None, :] <= tile_ids[:, None]).astype(I32), axis=1)
    tile_expert = jnp.minimum(tile_expert, N_EXPERTS - 1)
    xs = _dispatch(h2, slott, gtab, poffg, n_tiles * ts)
    ys = _expert_ffn(xs, tile_expert, n_used.astype(I32), w_gate, w_up, w_down)
    return _combine(ys, slot, wts, gtab, poffg, h2, xmid, mods, mod_map, ws_gate, ws_up, ws_down, g_final,
                    final_norm)


def kernel(x, c, ctx, c_ctx, l0_w_ada, l0_b_ada, l0_g_mix, l0_w_in, l0_sink, l0_rpb, l0_w_out, l0_g_moe, l0_w_router, l0_b_router, l0_w_gate, l0_w_up, l0_w_down, l0_ws_gate, l0_ws_up, l0_ws_down, l1_w_ada, l1_b_ada, l1_g_mix, l1_w_in, l1_g_q_lora, l1_w_uq, l1_g_kv_lora, l1_w_ukv, l1_g_qn, l1_g_kn, l1_w_out, l1_g_moe, l1_w_router, l1_b_router, l1_w_gate, l1_w_up, l1_w_down, l1_ws_gate, l1_ws_up, l1_ws_down, g_final):
    b, seq, d = x.shape
    assert d == D_MODEL and ctx.shape[1] == CTX_LEN and seq % TOK_TILE == 0
    u = CTX_LEN + seq
    tiles_u = u // TOK_TILE
    tiles_s = seq // TOK_TILE
    xa = jnp.concatenate([ctx, x], axis=1).astype(F32)

    cos_a, sin_a = _rope_tables(seq, HEAD_DIM, 0, HEAD_DIM, 1.0)
    cos_q, sin_q = _rope_tables(seq, C_ROPE, C_NOPE, LANES, float((C_NOPE + C_ROPE) ** -0.5))
    cos_k, sin_k = _rope_tables(seq, C_ROPE, 0, LANES, 1.0)

    mods0 = _mods(c, c_ctx, l0_w_ada, l0_b_ada)
    qa, kva, qb, kb, vb = _proj0(xa, mods0, l0_g_mix, l0_w_in, cos_a, sin_a)
    ya = _window_attention(qa, kva, l0_sink)
    yb = _na_attention(qb, kb, vb, _na_bias_table(l0_rpb))
    xmid, h2, slot, wts, slott, gtab, tot = _post_attn(ya, yb, l0_w_out, xa, 0, mods0, lambda t: jnp.minimum(t, 1),
                                                   l0_g_moe, l0_w_router, l0_b_router)
    xa = _moe(h2, slot, wts, slott, gtab, tot, xmid, mods0,
              lambda i: (i // tiles_u, jnp.minimum(i % tiles_u, 1), 0, 0),
              (l0_w_gate, l0_w_up, l0_w_down, l0_ws_gate, l0_ws_up, l0_ws_down), g_final, False).reshape(b, u, d)

    mods1 = _mods(c, c_ctx, l1_w_ada, l1_b_ada)
    qm, qd, km, vm, kvd = _proj1(xa, mods1, l1_g_mix, l1_w_in, l1_g_q_lora, l1_w_uq, l1_g_kv_lora, l1_w_ukv,
                                 l1_g_qn, l1_g_kn, (cos_a, sin_a, cos_q, sin_q, cos_k, sin_k))
    ym = _mla_attention(qm, km, vm)
    yd = _gqa_attention(qd, kvd)
    xmid, h2, slot, wts, slott, gtab, tot = _post_attn(ym, yd, l1_w_out, xa, CTX_LEN // TOK_TILE, mods1, lambda t: 1,
                                                   l1_g_moe, l1_w_router, l1_b_router)
    out = _moe(h2, slot, wts, slott, gtab, tot, xmid, mods1, lambda i: (i // tiles_s, 1, 0, 0),
               (l1_w_gate, l1_w_up, l1_w_down, l1_ws_gate, l1_ws_up, l1_ws_down), g_final, True)
    return out.reshape(b, seq, d)
```

```python
import functools

import numpy as np
import jax
import jax.numpy as jnp
from jax import lax
from jax.experimental import pallas as pl
from jax.experimental.pallas import tpu as pltpu

F32 = jnp.float32
BF16 = jnp.bfloat16
U32 = jnp.uint32
I32 = jnp.int32

D_MODEL = 1024
CTX_LEN = 256
GRID_W = 64
HEAD_DIM = 64
ROPE_THETA = 10000.0
NORM_EPS = 1e-6
NEG_INF = -1e30
A_WINDOW = 128
NA_ROWS = 8
NA_COLS = 16
C_Q_LORA = 384
C_KV_LORA = 256
C_NOPE = 64
C_ROPE = 32
C_V = 64
N_EXPERTS = 64
TOP_K = 6
EXPERT_FF = 256
SHARED_FF = 256
ROUTED_SCALE = 2.5

LANES = 128
TOK_TILE = 256
SLOT_TILE = 512
KPAD = 8
GRAN = 8
CHUNK = 256
SORT_ROWS = 2048
NGRAN = SORT_ROWS // GRAN
ALWAYS_ROWS = TOK_TILE * TOP_K
WAIT_BATCH = 16
ISSUE_UNROLL = 4
VMEM_LIMIT = 48 * 1024 * 1024


def _cparams(sem):
    return pltpu.CompilerParams(dimension_semantics=sem, vmem_limit_bytes=VMEM_LIMIT)


def _rms(x, g):
    return x * lax.rsqrt(jnp.mean(x * x, axis=-1, keepdims=True) + NORM_EPS) * g


def _norm_mod(x, g, shift, scale):
    return _rms(x, g) * (1.0 + scale) + shift


def _rope128(x, cos, sin):
    lane = lax.broadcasted_iota(I32, x.shape, 1)
    swapped = jnp.where(lane % 2 == 0, pltpu.roll(x, LANES - 1, 1), pltpu.roll(x, 1, 1))
    return x * cos + swapped * sin


def _group_sumsq(x, ones_bd):
    sq = x * x
    hi = sq.astype(BF16)
    lo = (sq - hi.astype(F32)).astype(BF16)
    return (jnp.dot(hi, ones_bd, preferred_element_type=F32) + jnp.dot(lo, ones_bd, preferred_element_type=F32))


def _head_norm(x, ones_bd, g):
    return x * lax.rsqrt(_group_sumsq(x, ones_bd) * (1.0 / HEAD_DIM) + NORM_EPS) * g


def _pack_bf16_pair(lo, hi):
    lo_bits = lax.bitcast_convert_type(lo.astype(BF16).astype(F32), U32)
    hi_bits = lax.bitcast_convert_type(hi.astype(BF16).astype(F32), U32)
    return (hi_bits & jnp.uint32(0xFFFF0000)) | (lo_bits >> 16)


def _unpack_bf16_pair(u):
    lo = lax.bitcast_convert_type(u << 16, F32)
    hi = lax.bitcast_convert_type(u & jnp.uint32(0xFFFF0000), F32)
    return lo, hi


def _keep_half(q128, half):
    lane = lax.broadcasted_iota(I32, q128.shape, 1)
    keep = (lane < HEAD_DIM) if half == 0 else (lane >= HEAD_DIM)
    return jnp.where(keep, q128, jnp.zeros_like(q128))


def _merge_halves(o_even, o_odd):
    lane = lax.broadcasted_iota(I32, o_even.shape, 1)
    return jnp.where(lane < HEAD_DIM, o_even, o_odd)


def _qk(q, k):
    return lax.dot_general(q, k, (((1,), (1,)), ((), ())), preferred_element_type=F32)


def _softmax_pv(s, v, sink=None):
    m = jnp.max(s, axis=-1, keepdims=True)
    if sink is not None:
        m = jnp.maximum(m, sink)
    e = jnp.exp(s - m)
    den = jnp.sum(e, axis=-1, keepdims=True)
    if sink is not None:
        den = den + jnp.exp(sink - m)
    return jnp.dot(e.astype(BF16), v, preferred_element_type=F32) / den


def _ada_kernel(c_ref, w_ref, b_ref, o_ref):
    c = c_ref[...]
    a = c * jax.nn.sigmoid(c)
    o_ref[...] = jnp.dot(a, w_ref[...], precision=lax.Precision.HIGHEST, preferred_element_type=F32) + b_ref[...]


def _ada(cond, w_ada, b_ada):
    n, d = cond.shape
    nout = w_ada.shape[1]
    bn = 512
    return pl.pallas_call(
        _ada_kernel,
        grid=(nout // bn,),
        in_specs=[pl.BlockSpec((n, d), lambda j: (0, 0)),
                  pl.BlockSpec((d, bn), lambda j: (0, j)),
                  pl.BlockSpec((1, bn), lambda j: (0, j))],
        out_specs=pl.BlockSpec((n, bn), lambda j: (0, j)),
        out_shape=jax.ShapeDtypeStruct((n, nout), F32),
        compiler_params=_cparams(("arbitrary",)),
        name="ada",
    )(cond, w_ada, b_ada.reshape(1, nout))


def _mods(c, c_ctx, w_ada, b_ada):
    b = c.shape[0]
    rows = ((b + 1 + 7) // 8) * 8
    cond = jnp.zeros((rows, D_MODEL), F32).at[:b].set(c).at[b].set(c_ctx)
    out = _ada(cond, w_ada, b_ada)
    lat = out[:b].reshape(b, 1, 6, D_MODEL)
    cx = jnp.broadcast_to(out[b].reshape(1, 1, 6, D_MODEL), (b, 1, 6, D_MODEL))
    return jnp.concatenate([cx, lat], axis=1)


def _axial_angles(n_tokens, rot_dim):
    t = np.arange(n_tokens)
    row = (t // GRID_W).astype(np.float32)
    col = (t % GRID_W).astype(np.float32)
    n_axis = rot_dim // 4
    inv_freq = (np.float32(ROPE_THETA) ** (-np.arange(n_axis, dtype=np.float32) / n_axis)).astype(np.float32)
    return jnp.concatenate([jnp.asarray(row[:, None] * inv_freq), jnp.asarray(col[:, None] * inv_freq)], axis=-1)


def _rope_tables(seq, rot_dim, lane_start, period, scale):
    ang = _axial_angles(seq, rot_dim)
    cos = jnp.repeat(jnp.cos(ang), 2, axis=-1)
    sin = jnp.repeat(jnp.sin(ang), 2, axis=-1) * jnp.tile(jnp.asarray([-1.0, 1.0], F32), rot_dim // 2)
    cos_p = jnp.ones((seq, period), F32).at[:, lane_start:lane_start + rot_dim].set(cos)
    sin_p = jnp.zeros((seq, period), F32).at[:, lane_start:lane_start + rot_dim].set(sin)
    cos_f = jnp.concatenate([jnp.ones((CTX_LEN, period), F32), cos_p], axis=0)
    sin_f = jnp.concatenate([jnp.zeros((CTX_LEN, period), F32), sin_p], axis=0)
    reps = LANES // period
    return jnp.tile(cos_f, (1, reps)) * scale, jnp.tile(sin_f, (1, reps)) * scale


def _na_bias_table(rpb):
    h = rpb.shape[0]
    qc = np.arange(GRID_W)
    kc = np.arange(GRID_W)
    cstart = np.clip(qc - NA_COLS // 2, 0, GRID_W - NA_COLS)
    valid = (kc[None, :] >= cstart[:, None]) & (kc[None, :] < cstart[:, None] + NA_COLS)
    dc = np.clip(kc[None, :] - qc[:, None] + NA_COLS - 1, 0, 2 * NA_COLS - 2)
    tabs = []
    for v in range(NA_ROWS):
        dr = np.clip(np.arange(NA_ROWS) - v + NA_ROWS - 1, 0, 2 * NA_ROWS - 2)
        b = rpb[:, dr][:, :, dc]
        b = jnp.where(jnp.asarray(valid)[None, None], b.astype(F32), NEG_INF)
        tabs.append(jnp.transpose(b, (0, 2, 1, 3)).reshape(h, GRID_W, NA_ROWS * GRID_W))
    tabs.append(jnp.full((h, GRID_W, NA_ROWS * GRID_W), NEG_INF, F32))
    loc = jnp.stack(tabs, axis=0)
    return jnp.concatenate([jnp.zeros(loc.shape[:3] + (CTX_LEN,), F32), loc], axis=-1)


def _dup_heads(w, n_heads):
    d = w.shape[0]
    w = w.reshape(d, n_heads, 1, HEAD_DIM)
    return jnp.broadcast_to(w, (d, n_heads, 2, HEAD_DIM)).reshape(d, n_heads * 2 * HEAD_DIM)


def _proj0_kernel(x_ref, mod_ref, g_ref, w_ref, cos_ref, sin_ref, qa_ref, kva_ref, qb_ref, kb_ref, vb_ref):
    h = _norm_mod(x_ref[0], g_ref[...], mod_ref[0, 0, 0:1, :], mod_ref[0, 0, 1:2, :])
    r = jnp.dot(h.astype(BF16), w_ref[...], preferred_element_type=F32)
    cos, sin = cos_ref[...], sin_ref[...]
    roped = [_rope128(r[:, i * LANES:(i + 1) * LANES], cos, sin) for i in range(6)]
    qa_ref[0] = jnp.concatenate(roped[0:4], axis=1).astype(BF16)
    kva_ref[0] = jnp.concatenate(roped[4:6] + [r[:, 768:1024]], axis=1).astype(BF16)
    qb_ref[0] = r[:, 1024:1536].astype(BF16)
    kb_ref[0] = r[:, 1536:2048].astype(BF16)
    vb_ref[0] = r[:, 2048:2560].astype(BF16)


def _proj0(xa, mods, g_mix, w_in, cos, sin):
    b, u, d = xa.shape
    tm = TOK_TILE
    s = 1.0 / 8.0
    w = jnp.concatenate([w_in[:, 0:512] * s, _dup_heads(w_in[:, 512:640], 2), _dup_heads(w_in[:, 640:768], 2),
                         w_in[:, 768:1280] * s, w_in[:, 1280:1792], w_in[:, 1792:2304]], axis=1).astype(BF16)
    nw = w.shape[1]
    tok = lambda n: pl.BlockSpec((1, tm, n), lambda i, t: (i, t, 0))
    return pl.pallas_call(
        _proj0_kernel,
        grid=(b, u // tm),
        in_specs=[tok(d),
                  pl.BlockSpec((1, 1, 6, d), lambda i, t: (i, jnp.minimum(t, 1), 0, 0)),
                  pl.BlockSpec((1, d), lambda i, t: (0, 0)),
                  pl.BlockSpec((d, nw), lambda i, t: (0, 0)),
                  pl.BlockSpec((tm, LANES), lambda i, t: (t, 0)),
                  pl.BlockSpec((tm, LANES), lambda i, t: (t, 0))],
        out_specs=[tok(512)] * 5,
        out_shape=[jax.ShapeDtypeStruct((b, u, 512), BF16)] * 5,
        compiler_params=_cparams(("arbitrary", "arbitrary")),
        name="proj0",
    )(xa, mods, g_mix.reshape(1, d), w, cos, sin)


def _window_kernel(sink_ref, q_ref, kv_ref, o_ref, *, tq, span):
    n = pl.program_id(1)
    u = kv_ref.shape[1]
    ls = pl.multiple_of(jnp.clip((n - 1) * tq, CTX_LEN, u - span), tq)
    qpos = n * tq + lax.broadcasted_iota(I32, (tq, span), 0)
    kpos = ls + lax.broadcasted_iota(I32, (tq, span), 1)
    valid = (kpos >= CTX_LEN) & (jnp.abs(qpos - kpos) <= A_WINDOW) & (qpos >= CTX_LEN)
    kv_c = kv_ref[0, 0:CTX_LEN, :]
    kv_l = kv_ref[0, pl.ds(ls, span), :]
    outs = [None] * 8
    for g in range(2):
        kc = kv_c[:, g * LANES:(g + 1) * LANES]
        kl = kv_l[:, g * LANES:(g + 1) * LANES]
        v_all = jnp.concatenate([kv_c[:, 256 + g * LANES:256 + (g + 1) * LANES],
                                 kv_l[:, 256 + g * LANES:256 + (g + 1) * LANES]], axis=0)
        for hh in range(4):
            h = 4 * g + hh
            q = _keep_half(q_ref[0, :, (h // 2) * LANES:(h // 2 + 1) * LANES], h % 2)
            s_l = jnp.where(valid, _qk(q, kl), NEG_INF)
            s = jnp.concatenate([_qk(q, kc), s_l], axis=1)
            outs[h] = _softmax_pv(s, v_all, sink_ref[h])
    o_ref[0] = jnp.concatenate([_merge_halves(outs[2 * p], outs[2 * p + 1]) for p in range(4)], axis=1).astype(BF16)


def _window_attention(qa, kva, sink):
    b, u, _ = qa.shape
    tq = A_WINDOW
    span = 3 * A_WINDOW
    return pl.pallas_call(
        functools.partial(_window_kernel, tq=tq, span=span),
        grid=(b, u // tq),
        in_specs=[pl.BlockSpec(memory_space=pltpu.SMEM),
                  pl.BlockSpec((1, tq, 512), lambda i, n: (i, n, 0)),
                  pl.BlockSpec((1, u, 512), lambda i, n: (i, 0, 0))],
        out_specs=pl.BlockSpec((1, tq, 512), lambda i, n: (i, n, 0)),
        out_shape=jax.ShapeDtypeStruct((b, u, 512), BF16),
        compiler_params=_cparams(("arbitrary", "arbitrary")),
        name="window_attn",
    )(sink.astype(F32), qa, kva)


def _na_variant(j, rows):
    r = jnp.maximum(j - CTX_LEN // GRID_W, 0)
    rs = jnp.clip(r - NA_ROWS // 2, 0, rows - NA_ROWS)
    return r, rs


def _na_kernel(q_ref, k_ref, v_ref, bias_ref, o_ref, *, rows):
    j = pl.program_id(1)
    _, rs = _na_variant(j, rows)
    band = NA_ROWS * GRID_W
    start = pl.multiple_of(CTX_LEN + rs * GRID_W, GRID_W)
    k_all = jnp.concatenate([k_ref[0, 0:CTX_LEN, :], k_ref[0, pl.ds(start, band), :]], axis=0)
    v_all = jnp.concatenate([v_ref[0, 0:CTX_LEN, :], v_ref[0, pl.ds(start, band), :]], axis=0)
    outs = []
    for p in range(4):
        q128 = q_ref[0, :, p * LANES:(p + 1) * LANES]
        k128 = k_all[:, p * LANES:(p + 1) * LANES]
        v128 = v_all[:, p * LANES:(p + 1) * LANES]
        pair = []
        for half in range(2):
            s = _qk(_keep_half(q128, half), k128) + bias_ref[0, 2 * p + half]
            pair.append(_softmax_pv(s, v128))
        outs.append(_merge_halves(pair[0], pair[1]))
    o_ref[0] = jnp.concatenate(outs, axis=1).astype(BF16)


def _na_attention(qb, kb, vb, bias_tab):
    b, u, _ = qb.shape
    rows = (u - CTX_LEN) // GRID_W
    assert rows >= NA_ROWS
    nk = bias_tab.shape[-1]

    def bias_map(i, j):
        r, rs = _na_variant(j, rows)
        return (jnp.where(j < CTX_LEN // GRID_W, NA_ROWS, r - rs), 0, 0, 0)

    return pl.pallas_call(
        functools.partial(_na_kernel, rows=rows),
        grid=(b, u // GRID_W),
        in_specs=[pl.BlockSpec((1, GRID_W, 512), lambda i, j: (i, j, 0)),
                  pl.BlockSpec((1, u, 512), lambda i, j: (i, 0, 0)),
                  pl.BlockSpec((1, u, 512), lambda i, j: (i, 0, 0)),
                  pl.BlockSpec((1, 8, GRID_W, nk), bias_map)],
        out_specs=pl.BlockSpec((1, GRID_W, 512), lambda i, j: (i, j, 0)),
        out_shape=jax.ShapeDtypeStruct((b, u, 512), BF16),
        compiler_params=_cparams(("arbitrary", "arbitrary")),
        name="na_attn",
    )(qb, kb, vb, bias_tab)


def _proj1_kernel(x_ref, mod_ref, g_ref, w_ref, gq_ref, wuq_ref, gkv_ref, wkv_ref, gqn_ref, gkn_ref, ones_ref,
                  cosa_ref, sina_ref, cosq_ref, sinq_ref, cosk_ref, sink_ref,
                  qm_ref, qd_ref, km_ref, vm_ref, kvd_ref):
    h = _norm_mod(x_ref[0], g_ref[...], mod_ref[0, 0, 0:1, :], mod_ref[0, 0, 1:2, :])
    r = jnp.dot(h.astype(BF16), w_ref[...], preferred_element_type=F32)
    ones_bd = ones_ref[...]
    cosa, sina = cosa_ref[...], sina_ref[...]
    cq = _rms(r[:, 0:C_Q_LORA], gq_ref[...])
    qm = jnp.dot(cq.astype(BF16), wuq_ref[...], preferred_element_type=F32)
    cosq, sinq = cosq_ref[...], sinq_ref[...]
    qm_ref[0] = jnp.concatenate([_rope128(qm[:, i * LANES:(i + 1) * LANES], cosq, sinq) for i in range(8)],
                                axis=1).astype(BF16)
    gqn = gqn_ref[...]
    qd_ref[0] = jnp.concatenate(
        [_rope128(_head_norm(r[:, 384 + i * LANES:384 + (i + 1) * LANES], ones_bd, gqn), cosa, sina)
         for i in range(4)], axis=1).astype(BF16)
    ckv = _rms(r[:, 896:1152], gkv_ref[...])
    kr = _rope128(r[:, 1152:1280], cosk_ref[...], sink_ref[...])
    kv = jnp.dot(jnp.concatenate([ckv, kr], axis=1).astype(BF16), wkv_ref[...], preferred_element_type=F32)
    km_ref[0] = kv[:, 0:1024].astype(BF16)
    vm_ref[0] = kv[:, 1024:1536].astype(BF16)
    gkn = gkn_ref[...]
    kd = [_rope128(_head_norm(r[:, 1280 + i * LANES:1280 + (i + 1) * LANES], ones_bd, gkn), cosa, sina)
          for i in range(2)]
    kvd_ref[0] = jnp.concatenate(kd + [r[:, 1536:1792]], axis=1).astype(BF16)


def _proj1(xa, mods, g_mix, w_in, g_q_lora, w_uq, g_kv_lora, w_ukv, g_qn, g_kn, tabs):
    b, u, d = xa.shape
    tm = TOK_TILE
    zpad = jnp.zeros((d, LANES - C_ROPE), F32)
    w = jnp.concatenate([w_in[:, 0:896], w_in[:, 896:1152], w_in[:, 1152:1184], zpad,
                         _dup_heads(w_in[:, 1184:1312], 2), _dup_heads(w_in[:, 1312:1440], 2)], axis=1).astype(BF16)
    nw = w.shape[1]
    wuq = w_uq.reshape(C_Q_LORA, 8, C_NOPE + C_ROPE)
    wuq = jnp.concatenate([wuq, jnp.zeros((C_Q_LORA, 8, LANES - C_NOPE - C_ROPE), F32)], axis=-1)
    wuq = wuq.reshape(C_Q_LORA, 8 * LANES).astype(BF16)
    wukv = w_ukv.reshape(C_KV_LORA, 8, C_NOPE + C_V)
    wk = jnp.concatenate([wukv[:, :, :C_NOPE], jnp.zeros((C_KV_LORA, 8, LANES - C_NOPE), F32)], axis=-1)
    wk = wk.reshape(C_KV_LORA, 8 * LANES)
    wv = wukv[:, :, C_NOPE:].reshape(C_KV_LORA, 8 * C_V)
    place = np.zeros((LANES, 8, LANES), np.float32)
    for j in range(C_ROPE):
        place[j, :, C_NOPE + j] = 1.0
    place = jnp.asarray(place.reshape(LANES, 8 * LANES))
    wkv = jnp.concatenate([jnp.concatenate([wk, wv], axis=1),
                           jnp.concatenate([place, jnp.zeros((LANES, 8 * C_V), F32)], axis=1)], axis=0).astype(BF16)
    ones_bd = jnp.asarray(np.kron(np.eye(2, dtype=np.float32), np.ones((HEAD_DIM, HEAD_DIM), np.float32))).astype(BF16)
    gqn = jnp.tile(g_qn.astype(F32) * (1.0 / 8.0), 2).reshape(1, LANES)
    gkn = jnp.tile(g_kn.astype(F32), 2).reshape(1, LANES)
    cosa, sina, cosq, sinq, cosk, sink = tabs
    tok = lambda n: pl.BlockSpec((1, tm, n), lambda i, t: (i, t, 0))
    whole = lambda a: pl.BlockSpec(a.shape, lambda i, t: (0,) * a.ndim)
    tab = pl.BlockSpec((tm, LANES), lambda i, t: (t, 0))
    small = [g_mix.reshape(1, d), w, g_q_lora.reshape(1, -1).astype(F32), wuq, g_kv_lora.reshape(1, -1).astype(F32),
             wkv, gqn, gkn, ones_bd]
    return pl.pallas_call(
        _proj1_kernel,
        grid=(b, u // tm),
        in_specs=[tok(d), pl.BlockSpec((1, 1, 6, d), lambda i, t: (i, jnp.minimum(t, 1), 0, 0))]
                 + [whole(a) for a in small] + [tab] * 6,
        out_specs=[tok(1024), tok(512), tok(1024), tok(512), tok(512)],
        out_shape=[jax.ShapeDtypeStruct((b, u, n), BF16) for n in (1024, 512, 1024, 512, 512)],
        compiler_params=_cparams(("arbitrary", "arbitrary")),
        name="proj1",
    )(xa, mods, *small, cosa, sina, cosq, sinq, cosk, sink)


def _mla_kernel(q_ref, k_ref, v_ref, o_ref):
    outs = []
    for h in range(8):
        s = _qk(q_ref[0, :, h * LANES:(h + 1) * LANES], k_ref[0, :, h * LANES:(h + 1) * LANES])
        outs.append(_softmax_pv(s, v_ref[0, :, (h // 2) * LANES:(h // 2 + 1) * LANES]))
    o_ref[0] = jnp.concatenate([_merge_halves(outs[2 * p], outs[2 * p + 1]) for p in range(4)], axis=1).astype(BF16)


def _mla_attention(qm, km, vm):
    b, u, _ = qm.shape
    tq = 256
    nq = (u - CTX_LEN) // tq
    off = CTX_LEN // tq
    return pl.pallas_call(
        _mla_kernel,
        grid=(b, nq),
        in_specs=[pl.BlockSpec((1, tq, 1024), lambda i, n: (i, n + off, 0)),
                  pl.BlockSpec((1, u, 1024), lambda i, n: (i, 0, 0)),
                  pl.BlockSpec((1, u, 512), lambda i, n: (i, 0, 0))],
        out_specs=pl.BlockSpec((1, tq, 512), lambda i, n: (i, n, 0)),
        out_shape=jax.ShapeDtypeStruct((b, u - CTX_LEN, 512), BF16),
        compiler_params=_cparams(("arbitrary", "arbitrary")),
        name="mla_attn",
    )(qm, km, vm)


def _gqa_kernel(q_ref, kv_ref, o_ref):
    outs = [None] * 8
    for g in range(2):
        k = kv_ref[0, :, g * LANES:(g + 1) * LANES]
        v = kv_ref[0, :, 256 + g * LANES:256 + (g + 1) * LANES]
        for hh in range(4):
            h = 4 * g + hh
            q = _keep_half(q_ref[0, :, (h // 2) * LANES:(h // 2 + 1) * LANES], h % 2)
            outs[h] = _softmax_pv(_qk(q, k), v)
    o_ref[0] = jnp.concatenate([_merge_halves(outs[2 * p], outs[2 * p + 1]) for p in range(4)], axis=1).astype(BF16)


def _gqa_attention(qd, kvd):
    b, u, _ = qd.shape
    tq = 256
    nq = (u - CTX_LEN) // tq
    off = CTX_LEN // tq
    return pl.pallas_call(
        _gqa_kernel,
        grid=(b, nq),
        in_specs=[pl.BlockSpec((1, tq, 512), lambda i, n: (i, n + off, 0)),
                  pl.BlockSpec((1, u, 512), lambda i, n: (i, 0, 0))],
        out_specs=pl.BlockSpec((1, tq, 512), lambda i, n: (i, n, 0)),
        out_shape=jax.ShapeDtypeStruct((b, u - CTX_LEN, 512), BF16),
        compiler_params=_cparams(("arbitrary", "arbitrary")),
        name="gqa_attn",
    )(qd, kvd)


def _post_attn_kernel(ya_ref, yb_ref, w_ref, x_ref, mod_ref, g_ref, wr_ref, br_ref,
                      xmid_ref, h2_ref, slot_ref, wts_ref, slott_ref, gtab_ref, tot_ref, run_ref):
    first = (pl.program_id(0) == 0) & (pl.program_id(1) == 0)

    @pl.when(first)
    def _():
        run_ref[...] = jnp.zeros_like(run_ref)

    y = (jnp.dot(ya_ref[0], w_ref[0:512, :], preferred_element_type=F32)
         + jnp.dot(yb_ref[0], w_ref[512:1024, :], preferred_element_type=F32))
    x1 = x_ref[0] + mod_ref[0, 0, 2:3, :] * y
    xmid_ref[0] = x1
    h2 = _norm_mod(x1, g_ref[...], mod_ref[0, 0, 3:4, :], mod_ref[0, 0, 4:5, :])
    h2_ref[...] = h2.astype(BF16)

    logits = jnp.dot(h2, wr_ref[...], precision=lax.Precision.HIGHEST, preferred_element_type=F32)
    scores = jax.nn.sigmoid(logits)
    tm = scores.shape[0]
    lane = lax.broadcasted_iota(I32, (tm, LANES), 1).astype(F32)
    biased = jnp.where(lane < N_EXPERTS, scores + br_ref[...], -jnp.inf)
    picked = jnp.zeros((tm, LANES), F32)
    sel_idx, sel_val = [], []
    for _k in range(TOP_K):
        m = jnp.max(biased, axis=-1, keepdims=True)
        i_k = jnp.min(jnp.where(biased == m, lane, float(LANES)), axis=-1, keepdims=True)
        hit = lane == i_k
        sel_idx.append(i_k)
        sel_val.append(jnp.sum(jnp.where(hit, scores, 0.0), axis=-1, keepdims=True))
        picked = jnp.where(hit, 1.0, picked)
        biased = jnp.where(hit, -jnp.inf, biased)
    total = sel_val[0]
    for v in sel_val[1:]:
        total = total + v
    cnt = jnp.sum(picked, axis=0, keepdims=True)
    ng = jnp.floor((cnt + float(GRAN - 1)) * (1.0 / GRAN))
    e_i = lax.broadcasted_iota(I32, (LANES, LANES), 0)
    e_j = lax.broadcasted_iota(I32, (LANES, LANES), 1)
    upper = jnp.where(e_i < e_j, 1.0, 0.0).astype(BF16)
    loffg = jnp.dot(jnp.broadcast_to(ng, (8, LANES)).astype(BF16), upper, preferred_element_type=F32)[0:1]
    r_i = lax.broadcasted_iota(I32, (tm, tm), 0)
    c_i = lax.broadcasted_iota(I32, (tm, tm), 1)
    before = jnp.dot(jnp.where(c_i < r_i, 1.0, 0.0).astype(BF16), picked.astype(BF16), preferred_element_type=F32)
    base = before + float(GRAN) * loffg
    slot_o = jnp.zeros((tm, LANES), F32)
    wts_o = jnp.zeros((tm, LANES), F32)
    for k in range(TOP_K):
        slot_k = jnp.sum(jnp.where(lane == sel_idx[k], base, 0.0), axis=-1, keepdims=True)
        slot_o = jnp.where(lane == float(k), slot_k, slot_o)
        wts_o = jnp.where(lane == float(k), ROUTED_SCALE * sel_val[k] / total, wts_o)
    slot_ref[...] = slot_o[:, :KPAD].astype(I32)
    wts_ref[...] = wts_o[:, :KPAD]
    slott_ref[0] = slot_o.T[:KPAD, :].astype(I32)

    run = run_ref[...]

    def as_col(v):
        return jnp.sum(jnp.where(e_i == e_j, jnp.broadcast_to(v, (LANES, LANES)), 0.0), axis=1, keepdims=True)

    end_c = as_col(loffg + ng)
    val_c = as_col(run - loffg)
    jj = lax.broadcasted_iota(I32, (LANES, NGRAN), 1).astype(F32)
    ee = lax.broadcasted_iota(I32, (LANES, NGRAN), 0).astype(F32)
    e_of_j = jnp.sum(jnp.where(end_c <= jj, 1.0, 0.0), axis=0, keepdims=True)
    dst = jnp.sum(jnp.where(ee == e_of_j, val_c + jj, 0.0), axis=0, keepdims=True)
    used = jnp.broadcast_to(jnp.sum(ng, axis=-1, keepdims=True), (1, NGRAN))
    gtab_ref[0] = jnp.concatenate([e_of_j, dst, used, jnp.zeros((5, NGRAN), F32)], axis=0).astype(I32)
    new_run = run + ng
    run_ref[...] = new_run
    tot_ref[...] = new_run


def _post_attn(ya, yb, w_out, x, x_tile_off, mods, mod_sel, g_moe, w_router, b_router):
    b, n, _ = ya.shape
    d = D_MODEL
    tm = TOK_TILE
    nt = n // tm
    t_tot = b * n
    wr = jnp.zeros((d, LANES), F32).at[:, :N_EXPERTS].set(w_router.astype(F32))
    br = jnp.zeros((1, LANES), F32).at[0, :N_EXPERTS].set(b_router.astype(F32))
    flat = lambda c: pl.BlockSpec((tm, c), lambda i, t: (i * nt + t, 0))
    outs = pl.pallas_call(
        _post_attn_kernel,
        grid=(b, nt),
        in_specs=[pl.BlockSpec((1, tm, 512), lambda i, t: (i, t, 0)),
                  pl.BlockSpec((1, tm, 512), lambda i, t: (i, t, 0)),
                  pl.BlockSpec((d, d), lambda i, t: (0, 0)),
                  pl.BlockSpec((1, tm, d), lambda i, t: (i, t + x_tile_off, 0)),
                  pl.BlockSpec((1, 1, 6, d), lambda i, t: (i, mod_sel(t), 0, 0)),
                  pl.BlockSpec((1, d), lambda i, t: (0, 0)),
                  pl.BlockSpec((d, LANES), lambda i, t: (0, 0)),
                  pl.BlockSpec((1, LANES), lambda i, t: (0, 0))],
        out_specs=[pl.BlockSpec((1, tm, d), lambda i, t: (i, t, 0)),
                   flat(d), flat(KPAD), flat(KPAD),
                   pl.BlockSpec((1, KPAD, tm), lambda i, t: (i * nt + t, 0, 0)),
                   pl.BlockSpec((1, 8, NGRAN), lambda i, t: (i * nt + t, 0, 0)),
                   pl.BlockSpec((1, LANES), lambda i, t: (0, 0))],
        out_shape=[jax.ShapeDtypeStruct((b, n, d), F32),
                   jax.ShapeDtypeStruct((t_tot, d), BF16),
                   jax.ShapeDtypeStruct((t_tot, KPAD), I32),
                   jax.ShapeDtypeStruct((t_tot, KPAD), F32),
                   jax.ShapeDtypeStruct((b * nt, KPAD, tm), I32),
                   jax.ShapeDtypeStruct((b * nt, 8, NGRAN), I32),
                   jax.ShapeDtypeStruct((1, LANES), F32)],
        scratch_shapes=[pltpu.VMEM((1, LANES), F32)],
        compiler_params=_cparams(("arbitrary", "arbitrary")),
        name="post_attn_route",
    )(ya, yb, w_out.astype(BF16), x, mods, g_moe.reshape(1, d), wr, br)
    return outs


def _granule_copy(src, src_g, dst, dst_g, sem):
    return pltpu.make_async_copy(src.at[pl.ds(pl.multiple_of(src_g * GRAN, GRAN), GRAN), :],
                                 dst.at[pl.ds(pl.multiple_of(dst_g * GRAN, GRAN), GRAN), :], sem)


def _for_each(n, body):
    trips = lax.shift_right_logical(n, jnp.int32(ISSUE_UNROLL.bit_length() - 1))

    def block(b, carry):
        for u in range(ISSUE_UNROLL):
            body(b * ISSUE_UNROLL + u)
        return carry

    def single(j, carry):
        body(j)
        return carry

    lax.fori_loop(0, trips, block, 0)
    lax.fori_loop(trips * ISSUE_UNROLL, n, single, 0)


def _drain(src, dst, sem, n):
    rows = WAIT_BATCH * GRAN
    batches = lax.shift_right_logical(n, jnp.int32(WAIT_BATCH.bit_length() - 1))

    def big(j, carry):
        pltpu.make_async_copy(src.at[pl.ds(0, rows), :], dst.at[pl.ds(0, rows), :], sem).wait()
        return carry

    def small(j, carry):
        _granule_copy(src, 0, dst, 0, sem).wait()
        return carry

    lax.fori_loop(0, batches, big, 0)
    lax.fori_loop(0, n - batches * WAIT_BATCH, small, 0)


def _dispatch_kernel(gtab_ref, h_ref, slott_ref, xs_ref, xloc, gprev, sems, *, nt):
    i = pl.program_id(0)
    cur = i % 2
    used = gtab_ref[0, 2, 0]
    slott = slott_ref[0]
    h = h_ref[...]
    tm = h.shape[0]
    half = h.shape[1] // 2
    def sort_chunk(c):
        rows = c * CHUNK + lax.broadcasted_iota(I32, (CHUNK, tm), 0)
        p = jnp.zeros((CHUNK, tm), F32)
        for k in range(TOP_K):
            p = jnp.where(rows == slott[k:k + 1, :], 1.0, p)
        xc = jnp.dot(p.astype(BF16), h, preferred_element_type=F32)
        lo = lax.bitcast_convert_type(xc[:, :half], U32)
        hi = lax.bitcast_convert_type(xc[:, half:], U32)
        xloc[cur, c * CHUNK:(c + 1) * CHUNK, :] = (hi & jnp.uint32(0xFFFF0000)) | (lo >> 16)

    for c in range(SORT_ROWS // CHUNK):
        if (c + 1) * CHUNK <= ALWAYS_ROWS:
            sort_chunk(c)
        else:
            pl.when(c * CHUNK < used * GRAN)(functools.partial(sort_chunk, c))

    @pl.when(i > 0)
    def _():
        _drain(xloc.at[1 - cur], xs_ref, sems.at[1 - cur], gprev[0])

    _for_each(used, lambda j: _granule_copy(xloc.at[cur], j, xs_ref, gtab_ref[0, 0, j], sems.at[cur]).start())
    gprev[0] = used

    @pl.when(i == nt - 1)
    def _():
        _drain(xloc.at[cur], xs_ref, sems.at[cur], used)


def _dispatch(h2, slott, gtab, n_slots):
    t_tot, d = h2.shape
    tm = TOK_TILE
    nt = t_tot // tm
    return pl.pallas_call(
        functools.partial(_dispatch_kernel, nt=nt),
        grid=(nt,),
        in_specs=[pl.BlockSpec((1, 8, NGRAN), lambda i: (i, 0, 0), memory_space=pltpu.SMEM),
                  pl.BlockSpec((tm, d), lambda i: (i, 0)),
                  pl.BlockSpec((1, KPAD, tm), lambda i: (i, 0, 0))],
        out_specs=pl.BlockSpec(memory_space=pl.ANY),
        out_shape=jax.ShapeDtypeStruct((n_slots, d // 2), U32),
        scratch_shapes=[pltpu.VMEM((2, SORT_ROWS, d // 2), U32), pltpu.SMEM((1,), I32),
                        pltpu.SemaphoreType.DMA((2,))],
        compiler_params=_cparams(("arbitrary",)),
        name="moe_dispatch",
    )(gtab, h2, slott)


def _expert_kernel(te_ref, nused_ref, xs_ref, wg_ref, wu_ref, wd_ref, ys_ref, wgu_s, wd_s):
    i = pl.program_id(0)
    changed = (i == 0) | (te_ref[i] != te_ref[jnp.maximum(i - 1, 0)])

    @pl.when(changed)
    def _():
        wgu_s[:, 0:EXPERT_FF] = wg_ref[0].astype(BF16)
        wgu_s[:, EXPERT_FF:2 * EXPERT_FF] = wu_ref[0].astype(BF16)
        wd_s[...] = wd_ref[0].astype(BF16)

    @pl.when(i < nused_ref[0])
    def _():
        lo, hi = _unpack_bf16_pair(xs_ref[...])
        x = jnp.concatenate([lo.astype(BF16), hi.astype(BF16)], axis=1)
        gu = jnp.dot(x, wgu_s[...], preferred_element_type=F32)
        gate, up = gu[:, :EXPERT_FF], gu[:, EXPERT_FF:]
        act = gate * jax.nn.sigmoid(gate) * up
        y = jnp.dot(act.astype(BF16), wd_s[...], preferred_element_type=F32)
        half = y.shape[1] // 2
        ys_ref[...] = _pack_bf16_pair(y[:, :half], y[:, half:])


def _expert_ffn(xs, tile_expert, n_used, w_gate, w_up, w_down):
    n_slots, w = xs.shape
    ts = SLOT_TILE
    d, f = w_gate.shape[1], w_gate.shape[2]
    grid_spec = pltpu.PrefetchScalarGridSpec(
        num_scalar_prefetch=2,
        grid=(n_slots // ts,),
        in_specs=[pl.BlockSpec((ts, w), lambda i, te, nu: (i, 0)),
                  pl.BlockSpec((1, d, f), lambda i, te, nu: (te[i], 0, 0)),
                  pl.BlockSpec((1, d, f), lambda i, te, nu: (te[i], 0, 0)),
                  pl.BlockSpec((1, f, d), lambda i, te, nu: (te[i], 0, 0))],
        out_specs=pl.BlockSpec((ts, w), lambda i, te, nu: (i, 0)),
        scratch_shapes=[pltpu.VMEM((d, 2 * f), BF16), pltpu.VMEM((f, d), BF16)],
    )
    return pl.pallas_call(
        _expert_kernel,
        grid_spec=grid_spec,
        out_shape=jax.ShapeDtypeStruct((n_slots, w), U32),
        compiler_params=_cparams(("arbitrary",)),
        name="moe_experts",
    )(tile_expert, n_used, xs, w_gate, w_up, w_down)


def _combine_kernel(gtab_ref, gnext_ref, ys_ref, slot_ref, wts_ref, h_ref, x_ref, mod_ref, wsgu_ref,
                    wsd_ref, gf_ref, o_ref, yloc, acc_ref, sems, *, nt, final_norm):
    i = pl.program_id(0)
    cur = i % 2

    def fetch(tab_ref, buf):
        base_g = buf * NGRAN
        sem = sems.at[buf]

        _for_each(tab_ref[0, 2, 0], lambda j: _granule_copy(ys_ref, tab_ref[0, 0, j], yloc, base_g + j, sem).start())

    @pl.when(i == 0)
    def _():
        fetch(gtab_ref, 0)

    @pl.when(i + 1 < nt)
    def _():
        fetch(gnext_ref, 1 - cur)

    h = h_ref[...]
    tm = h.shape[0]
    gu = jnp.dot(h, wsgu_ref[...], preferred_element_type=F32)
    gate, up = gu[:, :SHARED_FF], gu[:, SHARED_FF:]
    shared = jnp.dot((gate * jax.nn.sigmoid(gate) * up).astype(BF16), wsd_ref[...], preferred_element_type=F32)

    used = gtab_ref[0, 2, 0]
    _drain(ys_ref, yloc, sems.at[cur], used)
    row0 = cur * SORT_ROWS

    slot_b = [jnp.broadcast_to(slot_ref[:, k:k + 1], (tm, LANES)) for k in range(TOP_K)]
    wts_b = [jnp.broadcast_to(wts_ref[:, k:k + 1], (tm, LANES)) for k in range(TOP_K)]
    lane = lax.broadcasted_iota(I32, (tm, LANES), 1)
    w = yloc.shape[1]

    def weights(c):
        blocks = []
        for j in range(CHUNK // LANES):
            cols = lane + (c * CHUNK + j * LANES)
            pw = jnp.zeros((tm, LANES), F32)
            for k in range(TOP_K):
                pw = jnp.where(cols == slot_b[k], wts_b[k], pw)
            blocks.append(pw.astype(BF16))
        return jnp.concatenate(blocks, axis=1)

    def values(c, masked):
        packed = yloc[pl.ds(pl.multiple_of(row0 + c * CHUNK, CHUNK), CHUNK), :]
        if masked:
            rows = c * CHUNK + lax.broadcasted_iota(I32, (CHUNK, w), 0)
            packed = jnp.where(rows < used * GRAN, packed, jnp.uint32(0))
        lo, hi = _unpack_bf16_pair(packed)
        return jnp.concatenate([lo.astype(BF16), hi.astype(BF16)], axis=1)

    n_always = ALWAYS_ROWS // CHUNK
    acc = shared
    for c in range(n_always):
        acc = acc + jnp.dot(weights(c), values(c, False), preferred_element_type=F32)
    acc_ref[...] = acc
    for c in range(n_always, SORT_ROWS // CHUNK):
        @pl.when(c * CHUNK < used * GRAN)
        def _():
            acc_ref[...] += jnp.dot(weights(c), values(c, True), preferred_element_type=F32)

    out = x_ref[...] + mod_ref[0, 0, 5:6, :] * acc_ref[...]
    if final_norm:
        out = _rms(out, gf_ref[...])
    o_ref[...] = out


def _combine(ys, slot, wts, gtab, h2, xmid, mods, mod_map, ws_gate, ws_up, ws_down, g_final, final_norm):
    t_tot, d = h2.shape
    tm = TOK_TILE
    nt = t_tot // tm
    wsgu = jnp.concatenate([ws_gate, ws_up], axis=1).astype(BF16)
    return pl.pallas_call(
        functools.partial(_combine_kernel, nt=nt, final_norm=final_norm),
        grid=(nt,),
        in_specs=[pl.BlockSpec((1, 8, NGRAN), lambda i: (i, 0, 0), memory_space=pltpu.SMEM),
                  pl.BlockSpec((1, 8, NGRAN), lambda i: (jnp.minimum(i + 1, nt - 1), 0, 0), memory_space=pltpu.SMEM),
                  pl.BlockSpec(memory_space=pl.ANY),
                  pl.BlockSpec((tm, KPAD), lambda i: (i, 0)),
                  pl.BlockSpec((tm, KPAD), lambda i: (i, 0)),
                  pl.BlockSpec((tm, d), lambda i: (i, 0)),
                  pl.BlockSpec((tm, d), lambda i: (i, 0)),
                  pl.BlockSpec((1, 1, 6, d), mod_map),
                  pl.BlockSpec((d, 2 * SHARED_FF), lambda i: (0, 0)),
                  pl.BlockSpec((SHARED_FF, d), lambda i: (0, 0)),
                  pl.BlockSpec((1, d), lambda i: (0, 0))],
        out_specs=pl.BlockSpec((tm, d), lambda i: (i, 0)),
        out_shape=jax.ShapeDtypeStruct((t_tot, d), F32),
        scratch_shapes=[pltpu.VMEM((2 * SORT_ROWS, d // 2), U32), pltpu.VMEM((tm, d), F32),
                        pltpu.SemaphoreType.DMA((2,))],
        compiler_params=_cparams(("arbitrary",)),
        name="moe_combine",
    )(gtab, gtab, ys, slot, wts, h2, xmid.reshape(t_tot, d), mods, wsgu, ws_down.astype(BF16),
      g_final.reshape(1, d).astype(F32))


def _moe(h2, slot, wts, slott, gtab, tot, xmid, mods, mod_map, moe_w, g_final, final_norm):
    (w_gate, w_up, w_down, ws_gate, ws_up, ws_down) = moe_w
    t_tot = h2.shape[0]
    ts = SLOT_TILE
    gpt = ts // GRAN
    nt = t_tot // TOK_TILE
    max_rows = t_tot * TOP_K + nt * N_EXPERTS * (GRAN - 1)
    n_tiles = -(-max_rows // ts) + N_EXPERTS
    totg = tot[0, :N_EXPERTS].astype(I32)
    tiles_e = (totg + gpt - 1) // gpt
    ends = jnp.cumsum(tiles_e)
    poffg = jnp.zeros((LANES,), I32).at[:N_EXPERTS].set((ends - tiles_e) * gpt)
    n_used = ends[-1:]
    tile_ids = jnp.minimum(jnp.arange(n_tiles, dtype=I32), n_used[0] - 1)
    tile_expert = jnp.sum((ends[None, :] <= tile_ids[:, None]).astype(I32), axis=1)
    tile_expert = jnp.minimum(tile_expert, N_EXPERTS - 1)
    gtab = gtab.at[:, 0, :].set(poffg[jnp.minimum(gtab[:, 0, :], LANES - 1)] + gtab[:, 1, :])
    xs = _dispatch(h2, slott, gtab, n_tiles * ts)
    ys = _expert_ffn(xs, tile_expert, n_used.astype(I32), w_gate, w_up, w_down)
    return _combine(ys, slot, wts, gtab, h2, xmid, mods, mod_map, ws_gate, ws_up, ws_down, g_final, final_norm)


def kernel(x, c, ctx, c_ctx, l0_w_ada, l0_b_ada, l0_g_mix, l0_w_in, l0_sink, l0_rpb, l0_w_out, l0_g_moe, l0_w_router, l0_b_router, l0_w_gate, l0_w_up, l0_w_down, l0_ws_gate, l0_ws_up, l0_ws_down, l1_w_ada, l1_b_ada, l1_g_mix, l1_w_in, l1_g_q_lora, l1_w_uq, l1_g_kv_lora, l1_w_ukv, l1_g_qn, l1_g_kn, l1_w_out, l1_g_moe, l1_w_router, l1_b_router, l1_w_gate, l1_w_up, l1_w_down, l1_ws_gate, l1_ws_up, l1_ws_down, g_final):
    b, seq, d = x.shape
    assert d == D_MODEL and ctx.shape[1] == CTX_LEN and seq % TOK_TILE == 0
    u = CTX_LEN + seq
    tiles_u = u // TOK_TILE
    tiles_s = seq // TOK_TILE
    xa = jnp.concatenate([ctx, x], axis=1).astype(F32)

    cos_a, sin_a = _rope_tables(seq, HEAD_DIM, 0, HEAD_DIM, 1.0)
    cos_q, sin_q = _rope_tables(seq, C_ROPE, C_NOPE, LANES, float((C_NOPE + C_ROPE) ** -0.5))
    cos_k, sin_k = _rope_tables(seq, C_ROPE, 0, LANES, 1.0)

    mods0 = _mods(c, c_ctx, l0_w_ada, l0_b_ada)
    qa, kva, qb, kb, vb = _proj0(xa, mods0, l0_g_mix, l0_w_in, cos_a, sin_a)
    ya = _window_attention(qa, kva, l0_sink)
    yb = _na_attention(qb, kb, vb, _na_bias_table(l0_rpb))
    xmid, h2, slot, wts, slott, gtab, tot = _post_attn(ya, yb, l0_w_out, xa, 0, mods0, lambda t: jnp.minimum(t, 1),
                                                   l0_g_moe, l0_w_router, l0_b_router)
    xa = _moe(h2, slot, wts, slott, gtab, tot, xmid, mods0,
              lambda i: (i // tiles_u, jnp.minimum(i % tiles_u, 1), 0, 0),
              (l0_w_gate, l0_w_up, l0_w_down, l0_ws_gate, l0_ws_up, l0_ws_down), g_final, False).reshape(b, u, d)

    mods1 = _mods(c, c_ctx, l1_w_ada, l1_b_ada)
    qm, qd, km, vm, kvd = _proj1(xa, mods1, l1_g_mix, l1_w_in, l1_g_q_lora, l1_w_uq, l1_g_kv_lora, l1_w_ukv,
                                 l1_g_qn, l1_g_kn, (cos_a, sin_a, cos_q, sin_q, cos_k, sin_k))
    ym = _mla_attention(qm, km, vm)
    yd = _gqa_attention(qd, kvd)
    xmid, h2, slot, wts, slott, gtab, tot = _post_attn(ym, yd, l1_w_out, xa, CTX_LEN // TOK_TILE, mods1, lambda t: 1,
                                                   l1_g_moe, l1_w_router, l1_b_router)
    out = _moe(h2, slot, wts, slott, gtab, tot, xmid, mods1, lambda i: (i // tiles_s, 1, 0, 0),
               (l1_w_gate, l1_w_up, l1_w_down, l1_ws_gate, l1_ws_up, l1_ws_down), g_final, True)
    return out.reshape(b, seq, d)
```

```python
import functools

import numpy as np
import jax
import jax.numpy as jnp
from jax import lax
from jax.experimental import pallas as pl
from jax.experimental.pallas import tpu as pltpu

F32 = jnp.float32
BF16 = jnp.bfloat16
U32 = jnp.uint32
I32 = jnp.int32

D_MODEL = 1024
CTX_LEN = 256
GRID_W = 64
HEAD_DIM = 64
ROPE_THETA = 10000.0
NORM_EPS = 1e-6
NEG_INF = -1e30
A_WINDOW = 128
NA_ROWS = 8
NA_COLS = 16
NA_QROWS = 4
NA_BAND = NA_ROWS + NA_QROWS - 1
C_Q_LORA = 384
C_KV_LORA = 256
C_NOPE = 64
C_ROPE = 32
C_V = 64
N_EXPERTS = 64
TOP_K = 6
EXPERT_FF = 256
SHARED_FF = 256
ROUTED_SCALE = 2.5

LANES = 128
TOK_TILE = 256
SLOT_TILE = 512
KPAD = 8
GRAN = 8
CHUNK = 256
SORT_ROWS = 2048
NGRAN = SORT_ROWS // GRAN
ALWAYS_ROWS = TOK_TILE * TOP_K
WAIT_BATCH = 16
ISSUE_UNROLL = 4
ROUTE_ROWS = 128
VMEM_LIMIT = 48 * 1024 * 1024


def _cparams(sem):
    return pltpu.CompilerParams(dimension_semantics=sem, vmem_limit_bytes=VMEM_LIMIT)


def _rms(x, g):
    return x * lax.rsqrt(jnp.mean(x * x, axis=-1, keepdims=True) + NORM_EPS) * g


def _norm_mod(x, g, shift, scale):
    return _rms(x, g) * (1.0 + scale) + shift


def _rope128(x, cos, sin):
    lane = lax.broadcasted_iota(I32, x.shape, 1)
    swapped = jnp.where(lane % 2 == 0, pltpu.roll(x, LANES - 1, 1), pltpu.roll(x, 1, 1))
    return x * cos + swapped * sin


def _group_sumsq(x, ones_bd):
    sq = x * x
    hi = sq.astype(BF16)
    lo = (sq - hi.astype(F32)).astype(BF16)
    return (jnp.dot(hi, ones_bd, preferred_element_type=F32) + jnp.dot(lo, ones_bd, preferred_element_type=F32))


def _head_norm(x, ones_bd, g):
    return x * lax.rsqrt(_group_sumsq(x, ones_bd) * (1.0 / HEAD_DIM) + NORM_EPS) * g


def _pack_bf16_pair(lo, hi):
    lo_bits = lax.bitcast_convert_type(lo.astype(BF16).astype(F32), U32)
    hi_bits = lax.bitcast_convert_type(hi.astype(BF16).astype(F32), U32)
    return (hi_bits & jnp.uint32(0xFFFF0000)) | (lo_bits >> 16)


def _unpack_bf16_pair(u):
    lo = lax.bitcast_convert_type(u << 16, F32)
    hi = lax.bitcast_convert_type(u & jnp.uint32(0xFFFF0000), F32)
    return lo, hi


def _keep_half(q128, half):
    lane = lax.broadcasted_iota(I32, q128.shape, 1)
    keep = (lane < HEAD_DIM) if half == 0 else (lane >= HEAD_DIM)
    return jnp.where(keep, q128, jnp.zeros_like(q128))


def _merge_halves(o_even, o_odd):
    lane = lax.broadcasted_iota(I32, o_even.shape, 1)
    return jnp.where(lane < HEAD_DIM, o_even, o_odd)


def _qk(q, k):
    return lax.dot_general(q, k, (((1,), (1,)), ((), ())), preferred_element_type=F32)


def _softmax_pv(s, v, sinks=None):
    m = jnp.max(s, axis=-1, keepdims=True)
    if sinks is not None:
        r = s.shape[0] // len(sinks)
        m = jnp.concatenate([jnp.maximum(m[i * r:(i + 1) * r], sinks[i]) for i in range(len(sinks))], axis=0)
    e = jnp.exp(s - m)
    den = jnp.sum(e, axis=-1, keepdims=True)
    if sinks is not None:
        den = den + jnp.concatenate([jnp.exp(sinks[i] - m[i * r:(i + 1) * r]) for i in range(len(sinks))], axis=0)
    return jnp.dot(e.astype(BF16), v, preferred_element_type=F32) / den


def _ada_kernel(c_ref, w_ref, b_ref, o_ref):
    c = c_ref[...]
    a = c * jax.nn.sigmoid(c)
    o_ref[...] = jnp.dot(a, w_ref[...], precision=lax.Precision.HIGHEST, preferred_element_type=F32) + b_ref[...]


def _ada(cond, w_ada, b_ada):
    n, d = cond.shape
    nout = w_ada.shape[1]
    bn = 512
    return pl.pallas_call(
        _ada_kernel,
        grid=(nout // bn,),
        in_specs=[pl.BlockSpec((n, d), lambda j: (0, 0)),
                  pl.BlockSpec((d, bn), lambda j: (0, j)),
                  pl.BlockSpec((1, bn), lambda j: (0, j))],
        out_specs=pl.BlockSpec((n, bn), lambda j: (0, j)),
        out_shape=jax.ShapeDtypeStruct((n, nout), F32),
        compiler_params=_cparams(("arbitrary",)),
        name="ada",
    )(cond, w_ada, b_ada.reshape(1, nout))


def _mods(c, c_ctx, w_ada, b_ada):
    b = c.shape[0]
    rows = ((b + 1 + 7) // 8) * 8
    cond = jnp.zeros((rows, D_MODEL), F32).at[:b].set(c).at[b].set(c_ctx)
    out = _ada(cond, w_ada, b_ada)
    lat = out[:b].reshape(b, 1, 6, D_MODEL)
    cx = jnp.broadcast_to(out[b].reshape(1, 1, 6, D_MODEL), (b, 1, 6, D_MODEL))
    return jnp.concatenate([cx, lat], axis=1)


def _axial_angles(n_tokens, rot_dim):
    t = np.arange(n_tokens)
    row = (t // GRID_W).astype(np.float32)
    col = (t % GRID_W).astype(np.float32)
    n_axis = rot_dim // 4
    inv_freq = (np.float32(ROPE_THETA) ** (-np.arange(n_axis, dtype=np.float32) / n_axis)).astype(np.float32)
    return jnp.concatenate([jnp.asarray(row[:, None] * inv_freq), jnp.asarray(col[:, None] * inv_freq)], axis=-1)


def _rope_tables(seq, rot_dim, lane_start, period, scale):
    ang = _axial_angles(seq, rot_dim)
    cos = jnp.repeat(jnp.cos(ang), 2, axis=-1)
    sin = jnp.repeat(jnp.sin(ang), 2, axis=-1) * jnp.tile(jnp.asarray([-1.0, 1.0], F32), rot_dim // 2)
    cos_p = jnp.ones((seq, period), F32).at[:, lane_start:lane_start + rot_dim].set(cos)
    sin_p = jnp.zeros((seq, period), F32).at[:, lane_start:lane_start + rot_dim].set(sin)
    cos_f = jnp.concatenate([jnp.ones((CTX_LEN, period), F32), cos_p], axis=0)
    sin_f = jnp.concatenate([jnp.zeros((CTX_LEN, period), F32), sin_p], axis=0)
    reps = LANES // period
    return jnp.tile(cos_f, (1, reps)) * scale, jnp.tile(sin_f, (1, reps)) * scale


def _na_band_start(block, rows):
    return np.clip(block * NA_QROWS - NA_ROWS // 2, 0, rows - NA_BAND)


def _na_bias_table(rpb, rows):
    h = rpb.shape[0]
    qc = np.arange(GRID_W)
    kc = np.arange(GRID_W)
    cstart = np.clip(qc - NA_COLS // 2, 0, GRID_W - NA_COLS)
    col_ok = (kc[None, :] >= cstart[:, None]) & (kc[None, :] < cstart[:, None] + NA_COLS)
    dc = np.clip(kc[None, :] - qc[:, None] + NA_COLS - 1, 0, 2 * NA_COLS - 2)
    col_sel = np.eye(2 * NA_COLS - 1, dtype=np.float32)[dc]
    variants, ids = [], [None]
    for blk in range(rows // NA_QROWS):
        rs0 = _na_band_start(blk, rows)
        r = blk * NA_QROWS + np.arange(NA_QROWS)
        rs = np.clip(r - NA_ROWS // 2, 0, rows - NA_ROWS)
        krow = rs0 + np.arange(NA_BAND)
        row_ok = (krow[None, :] >= rs[:, None]) & (krow[None, :] < rs[:, None] + NA_ROWS)
        dr = np.clip(krow[None, :] - r[:, None] + NA_ROWS - 1, 0, 2 * NA_ROWS - 2)
        key = (row_ok.tobytes(), dr.tobytes())
        if key not in [v[0] for v in variants]:
            variants.append((key, row_ok, dr))
        ids.append([v[0] for v in variants].index(key))
    ids[0] = len(variants)
    tabs = []
    for _, row_ok, dr in variants:
        row_sel = np.eye(2 * NA_ROWS - 1, dtype=np.float32)[dr]
        b = jnp.einsum('hrc,ijr,qkc->hiqjk', rpb.astype(F32), jnp.asarray(row_sel), jnp.asarray(col_sel),
                       precision=lax.Precision.HIGHEST)
        ok = row_ok[:, None, :, None] & col_ok[None, :, None, :]
        b = jnp.where(jnp.asarray(ok)[None], b, NEG_INF)
        tabs.append(b.reshape(h, NA_QROWS * GRID_W, NA_BAND * GRID_W))
    tabs.append(jnp.full((h, NA_QROWS * GRID_W, NA_BAND * GRID_W), NEG_INF, F32))
    loc = jnp.stack(tabs, axis=0)
    tab = jnp.concatenate([jnp.zeros(loc.shape[:3] + (CTX_LEN,), F32), loc], axis=-1)
    return tab, jnp.asarray(ids, I32)


def _dup_heads(w, n_heads):
    d = w.shape[0]
    w = w.reshape(d, n_heads, 1, HEAD_DIM)
    return jnp.broadcast_to(w, (d, n_heads, 2, HEAD_DIM)).reshape(d, n_heads * 2 * HEAD_DIM)


def _proj0_kernel(x_ref, mod_ref, g_ref, w_ref, cos_ref, sin_ref, qa_ref, kva_ref, qb_ref, kb_ref, vb_ref):
    h = _norm_mod(x_ref[0], g_ref[...], mod_ref[0, 0, 0:1, :], mod_ref[0, 0, 1:2, :])
    r = jnp.dot(h.astype(BF16), w_ref[...], preferred_element_type=F32)
    cos, sin = cos_ref[...], sin_ref[...]
    roped = [_rope128(r[:, i * LANES:(i + 1) * LANES], cos, sin) for i in range(6)]
    qa_ref[0] = jnp.concatenate(roped[0:4], axis=1).astype(BF16)
    kva_ref[0] = jnp.concatenate(roped[4:6] + [r[:, 768:1024]], axis=1).astype(BF16)
    qb_ref[0] = r[:, 1024:1536].astype(BF16)
    kb_ref[0] = r[:, 1536:2048].astype(BF16)
    vb_ref[0] = r[:, 2048:2560].astype(BF16)


def _proj0(xa, mods, g_mix, w_in, cos, sin):
    b, u, d = xa.shape
    tm = TOK_TILE
    s = 1.0 / 8.0
    w = jnp.concatenate([w_in[:, 0:512] * s, _dup_heads(w_in[:, 512:640], 2), _dup_heads(w_in[:, 640:768], 2),
                         w_in[:, 768:1280] * s, w_in[:, 1280:1792], w_in[:, 1792:2304]], axis=1).astype(BF16)
    nw = w.shape[1]
    tok = lambda n: pl.BlockSpec((1, tm, n), lambda i, t: (i, t, 0))
    return pl.pallas_call(
        _proj0_kernel,
        grid=(b, u // tm),
        in_specs=[tok(d),
                  pl.BlockSpec((1, 1, 6, d), lambda i, t: (i, jnp.minimum(t, 1), 0, 0)),
                  pl.BlockSpec((1, d), lambda i, t: (0, 0)),
                  pl.BlockSpec((d, nw), lambda i, t: (0, 0)),
                  pl.BlockSpec((tm, LANES), lambda i, t: (t, 0)),
                  pl.BlockSpec((tm, LANES), lambda i, t: (t, 0))],
        out_specs=[tok(512)] * 5,
        out_shape=[jax.ShapeDtypeStruct((b, u, 512), BF16)] * 5,
        compiler_params=_cparams(("arbitrary", "arbitrary")),
        name="proj0",
    )(xa, mods, g_mix.reshape(1, d), w, cos, sin)


def _window_kernel(sink_ref, q_ref, kv_ref, o_ref, *, tq, span):
    n = pl.program_id(1)
    u = kv_ref.shape[1]
    ls = pl.multiple_of(jnp.clip((n - 1) * tq, CTX_LEN, u - span), tq)
    nk = CTX_LEN + span
    row = lax.broadcasted_iota(I32, (4 * tq, nk), 0)
    col = lax.broadcasted_iota(I32, (4 * tq, nk), 1)
    qpos = n * tq + (row & (tq - 1))
    kpos = ls + col - CTX_LEN
    valid = (col < CTX_LEN) | ((kpos >= CTX_LEN) & (jnp.abs(qpos - kpos) <= A_WINDOW) & (qpos >= CTX_LEN))
    kv_all = jnp.concatenate([kv_ref[0, 0:CTX_LEN, :], kv_ref[0, pl.ds(ls, span), :]], axis=0)
    outs = [None] * 8
    for g in range(2):
        q4 = jnp.concatenate([_keep_half(q_ref[0, :, ((4 * g + hh) // 2) * LANES:((4 * g + hh) // 2 + 1) * LANES],
                                         hh % 2) for hh in range(4)], axis=0)
        s = jnp.where(valid, _qk(q4, kv_all[:, g * LANES:(g + 1) * LANES]), NEG_INF)
        o = _softmax_pv(s, kv_all[:, 256 + g * LANES:256 + (g + 1) * LANES],
                        [sink_ref[4 * g + hh] for hh in range(4)])
        for hh in range(4):
            outs[4 * g + hh] = o[hh * tq:(hh + 1) * tq]
    o_ref[0] = jnp.concatenate([_merge_halves(outs[2 * p], outs[2 * p + 1]) for p in range(4)], axis=1).astype(BF16)


def _window_attention(qa, kva, sink):
    b, u, _ = qa.shape
    tq = A_WINDOW
    span = 3 * A_WINDOW
    return pl.pallas_call(
        functools.partial(_window_kernel, tq=tq, span=span),
        grid=(b, u // tq),
        in_specs=[pl.BlockSpec(memory_space=pltpu.SMEM),
                  pl.BlockSpec((1, tq, 512), lambda i, n: (i, n, 0)),
                  pl.BlockSpec((1, u, 512), lambda i, n: (i, 0, 0))],
        out_specs=pl.BlockSpec((1, tq, 512), lambda i, n: (i, n, 0)),
        out_shape=jax.ShapeDtypeStruct((b, u, 512), BF16),
        compiler_params=_cparams(("arbitrary", "arbitrary")),
        name="window_attn",
    )(sink.astype(F32), qa, kva)


def _na_kernel(var_ref, q_ref, k_ref, v_ref, bias_ref, o_ref, *, rows):
    j = pl.program_id(1)
    rs0 = jnp.clip((j - 1) * NA_QROWS - NA_ROWS // 2, 0, rows - NA_BAND)
    band = NA_BAND * GRID_W
    start = pl.multiple_of(CTX_LEN + rs0 * GRID_W, GRID_W)
    k_all = jnp.concatenate([k_ref[0, 0:CTX_LEN, :], k_ref[0, pl.ds(start, band), :]], axis=0)
    v_all = jnp.concatenate([v_ref[0, 0:CTX_LEN, :], v_ref[0, pl.ds(start, band), :]], axis=0)
    tq = q_ref.shape[1]
    outs = []
    for p in range(4):
        q128 = q_ref[0, :, p * LANES:(p + 1) * LANES]
        q2 = jnp.concatenate([_keep_half(q128, 0), _keep_half(q128, 1)], axis=0)
        s = _qk(q2, k_all[:, p * LANES:(p + 1) * LANES])
        s = s + jnp.concatenate([bias_ref[0, 2 * p], bias_ref[0, 2 * p + 1]], axis=0)
        o = _softmax_pv(s, v_all[:, p * LANES:(p + 1) * LANES])
        outs.append(_merge_halves(o[:tq], o[tq:]))
    o_ref[0] = jnp.concatenate(outs, axis=1).astype(BF16)


def _na_attention(qb, kb, vb, rpb):
    b, u, _ = qb.shape
    rows = (u - CTX_LEN) // GRID_W
    assert rows >= NA_BAND and rows % NA_QROWS == 0
    bias_tab, variant = _na_bias_table(rpb, rows)
    tq = NA_QROWS * GRID_W
    assert tq == CTX_LEN
    nk = bias_tab.shape[-1]
    grid_spec = pltpu.PrefetchScalarGridSpec(
        num_scalar_prefetch=1,
        grid=(b, u // tq),
        in_specs=[pl.BlockSpec((1, tq, 512), lambda i, j, var: (i, j, 0)),
                  pl.BlockSpec((1, u, 512), lambda i, j, var: (i, 0, 0)),
                  pl.BlockSpec((1, u, 512), lambda i, j, var: (i, 0, 0)),
                  pl.BlockSpec((1, 8, tq, nk), lambda i, j, var: (var[j], 0, 0, 0))],
        out_specs=pl.BlockSpec((1, tq, 512), lambda i, j, var: (i, j, 0)),
    )
    return pl.pallas_call(
        functools.partial(_na_kernel, rows=rows),
        grid_spec=grid_spec,
        out_shape=jax.ShapeDtypeStruct((b, u, 512), BF16),
        compiler_params=_cparams(("arbitrary", "arbitrary")),
        name="na_attn",
    )(variant, qb, kb, vb, bias_tab)


def _proj1_kernel(x_ref, mod_ref, g_ref, w_ref, gq_ref, wuq_ref, gkv_ref, wkv_ref, gqn_ref, gkn_ref, ones_ref,
                  cosa_ref, sina_ref, cosq_ref, sinq_ref, cosk_ref, sink_ref,
                  qm_ref, qd_ref, km_ref, vm_ref, kvd_ref):
    h = _norm_mod(x_ref[0], g_ref[...], mod_ref[0, 0, 0:1, :], mod_ref[0, 0, 1:2, :])
    r = jnp.dot(h.astype(BF16), w_ref[...], preferred_element_type=F32)
    ones_bd = ones_ref[...]
    cosa, sina = cosa_ref[...], sina_ref[...]
    cq = _rms(r[:, 0:C_Q_LORA], gq_ref[...])
    qm = jnp.dot(cq.astype(BF16), wuq_ref[...], preferred_element_type=F32)
    cosq, sinq = cosq_ref[...], sinq_ref[...]
    qm_ref[0] = jnp.concatenate([_rope128(qm[:, i * LANES:(i + 1) * LANES], cosq, sinq) for i in range(8)],
                                axis=1).astype(BF16)
    gqn = gqn_ref[...]
    qd_ref[0] = jnp.concatenate(
        [_rope128(_head_norm(r[:, 384 + i * LANES:384 + (i + 1) * LANES], ones_bd, gqn), cosa, sina)
         for i in range(4)], axis=1).astype(BF16)
    ckv = _rms(r[:, 896:1152], gkv_ref[...])
    kr = _rope128(r[:, 1152:1280], cosk_ref[...], sink_ref[...])
    kv = jnp.dot(jnp.concatenate([ckv, kr], axis=1).astype(BF16), wkv_ref[...], preferred_element_type=F32)
    km_ref[0] = kv[:, 0:1024].astype(BF16)
    vm_ref[0] = kv[:, 1024:1536].astype(BF16)
    gkn = gkn_ref[...]
    kd = [_rope128(_head_norm(r[:, 1280 + i * LANES:1280 + (i + 1) * LANES], ones_bd, gkn), cosa, sina)
          for i in range(2)]
    kvd_ref[0] = jnp.concatenate(kd + [r[:, 1536:1792]], axis=1).astype(BF16)


def _proj1(xa, mods, g_mix, w_in, g_q_lora, w_uq, g_kv_lora, w_ukv, g_qn, g_kn, tabs):
    b, u, d = xa.shape
    tm = TOK_TILE
    zpad = jnp.zeros((d, LANES - C_ROPE), F32)
    w = jnp.concatenate([w_in[:, 0:896], w_in[:, 896:1152], w_in[:, 1152:1184], zpad,
                         _dup_heads(w_in[:, 1184:1312], 2), _dup_heads(w_in[:, 1312:1440], 2)], axis=1).astype(BF16)
    nw = w.shape[1]
    wuq = w_uq.reshape(C_Q_LORA, 8, C_NOPE + C_ROPE)
    wuq = jnp.concatenate([wuq, jnp.zeros((C_Q_LORA, 8, LANES - C_NOPE - C_ROPE), F32)], axis=-1)
    wuq = wuq.reshape(C_Q_LORA, 8 * LANES).astype(BF16)
    wukv = w_ukv.reshape(C_KV_LORA, 8, C_NOPE + C_V)
    wk = jnp.concatenate([wukv[:, :, :C_NOPE], jnp.zeros((C_KV_LORA, 8, LANES - C_NOPE), F32)], axis=-1)
    wk = wk.reshape(C_KV_LORA, 8 * LANES)
    wv = wukv[:, :, C_NOPE:].reshape(C_KV_LORA, 8 * C_V)
    place = np.zeros((LANES, 8, LANES), np.float32)
    for j in range(C_ROPE):
        place[j, :, C_NOPE + j] = 1.0
    place = jnp.asarray(place.reshape(LANES, 8 * LANES))
    wkv = jnp.concatenate([jnp.concatenate([wk, wv], axis=1),
                           jnp.concatenate([place, jnp.zeros((LANES, 8 * C_V), F32)], axis=1)], axis=0).astype(BF16)
    ones_bd = jnp.asarray(np.kron(np.eye(2, dtype=np.float32), np.ones((HEAD_DIM, HEAD_DIM), np.float32))).astype(BF16)
    gqn = jnp.tile(g_qn.astype(F32) * (1.0 / 8.0), 2).reshape(1, LANES)
    gkn = jnp.tile(g_kn.astype(F32), 2).reshape(1, LANES)
    cosa, sina, cosq, sinq, cosk, sink = tabs
    tok = lambda n: pl.BlockSpec((1, tm, n), lambda i, t: (i, t, 0))
    whole = lambda a: pl.BlockSpec(a.shape, lambda i, t: (0,) * a.ndim)
    tab = pl.BlockSpec((tm, LANES), lambda i, t: (t, 0))
    small = [g_mix.reshape(1, d), w, g_q_lora.reshape(1, -1).astype(F32), wuq, g_kv_lora.reshape(1, -1).astype(F32),
             wkv, gqn, gkn, ones_bd]
    return pl.pallas_call(
        _proj1_kernel,
        grid=(b, u // tm),
        in_specs=[tok(d), pl.BlockSpec((1, 1, 6, d), lambda i, t: (i, jnp.minimum(t, 1), 0, 0))]
                 + [whole(a) for a in small] + [tab] * 6,
        out_specs=[tok(1024), tok(512), tok(1024), tok(512), tok(512)],
        out_shape=[jax.ShapeDtypeStruct((b, u, n), BF16) for n in (1024, 512, 1024, 512, 512)],
        compiler_params=_cparams(("arbitrary", "arbitrary")),
        name="proj1",
    )(xa, mods, *small, cosa, sina, cosq, sinq, cosk, sink)


def _mla_kernel(q_ref, k_ref, v_ref, o_ref):
    outs = []
    for h in range(8):
        s = _qk(q_ref[0, :, h * LANES:(h + 1) * LANES], k_ref[0, :, h * LANES:(h + 1) * LANES])
        outs.append(_softmax_pv(s, v_ref[0, :, (h // 2) * LANES:(h // 2 + 1) * LANES]))
    o_ref[0] = jnp.concatenate([_merge_halves(outs[2 * p], outs[2 * p + 1]) for p in range(4)], axis=1).astype(BF16)


def _mla_attention(qm, km, vm):
    b, u, _ = qm.shape
    tq = 256
    nq = (u - CTX_LEN) // tq
    off = CTX_LEN // tq
    return pl.pallas_call(
        _mla_kernel,
        grid=(b, nq),
        in_specs=[pl.BlockSpec((1, tq, 1024), lambda i, n: (i, n + off, 0)),
                  pl.BlockSpec((1, u, 1024), lambda i, n: (i, 0, 0)),
                  pl.BlockSpec((1, u, 512), lambda i, n: (i, 0, 0))],
        out_specs=pl.BlockSpec((1, tq, 512), lambda i, n: (i, n, 0)),
        out_shape=jax.ShapeDtypeStruct((b, u - CTX_LEN, 512), BF16),
        compiler_params=_cparams(("arbitrary", "arbitrary")),
        name="mla_attn",
    )(qm, km, vm)


def _gqa_kernel(q_ref, kv_ref, o_ref):
    outs = [None] * 8
    for g in range(2):
        k = kv_ref[0, :, g * LANES:(g + 1) * LANES]
        v = kv_ref[0, :, 256 + g * LANES:256 + (g + 1) * LANES]
        for hh in range(4):
            h = 4 * g + hh
            q = _keep_half(q_ref[0, :, (h // 2) * LANES:(h // 2 + 1) * LANES], h % 2)
            outs[h] = _softmax_pv(_qk(q, k), v)
    o_ref[0] = jnp.concatenate([_merge_halves(outs[2 * p], outs[2 * p + 1]) for p in range(4)], axis=1).astype(BF16)


def _gqa_attention(qd, kvd):
    b, u, _ = qd.shape
    tq = 256
    nq = (u - CTX_LEN) // tq
    off = CTX_LEN // tq
    return pl.pallas_call(
        _gqa_kernel,
        grid=(b, nq),
        in_specs=[pl.BlockSpec((1, tq, 512), lambda i, n: (i, n + off, 0)),
                  pl.BlockSpec((1, u, 512), lambda i, n: (i, 0, 0))],
        out_specs=pl.BlockSpec((1, tq, 512), lambda i, n: (i, n, 0)),
        out_shape=jax.ShapeDtypeStruct((b, u - CTX_LEN, 512), BF16),
        compiler_params=_cparams(("arbitrary", "arbitrary")),
        name="gqa_attn",
    )(qd, kvd)


def _post_attn_kernel(ya_ref, yb_ref, w_ref, x_ref, mod_ref, g_ref, wr_ref, br_ref,
                      xmid_ref, h2_ref, slot_ref, wts_ref, slott_ref, gtab_ref, tot_ref, run_ref):
    first = (pl.program_id(0) == 0) & (pl.program_id(1) == 0)

    @pl.when(first)
    def _():
        run_ref[...] = jnp.zeros_like(run_ref)

    y = (jnp.dot(ya_ref[0], w_ref[0:512, :], preferred_element_type=F32)
         + jnp.dot(yb_ref[0], w_ref[512:1024, :], preferred_element_type=F32))
    x1 = x_ref[0] + mod_ref[0, 0, 2:3, :] * y
    xmid_ref[0] = x1
    h2 = _norm_mod(x1, g_ref[...], mod_ref[0, 0, 3:4, :], mod_ref[0, 0, 4:5, :])
    h2_ref[...] = h2.astype(BF16)

    logits = jnp.dot(h2, wr_ref[...], precision=lax.Precision.HIGHEST, preferred_element_type=F32)
    scores = jax.nn.sigmoid(logits)
    tm = scores.shape[0]
    lane = lax.broadcasted_iota(I32, (tm, LANES), 1).astype(F32)
    biased = jnp.where(lane < N_EXPERTS, scores + br_ref[...], -jnp.inf)
    def top_k(scores_g, biased_g, lane_g):
        picked_g = jnp.zeros_like(scores_g)
        idx_g, val_g = [], []
        for _k in range(TOP_K):
            m = jnp.max(biased_g, axis=-1, keepdims=True)
            i_k = jnp.min(jnp.where(biased_g == m, lane_g, float(LANES)), axis=-1, keepdims=True)
            hit = lane_g == i_k
            idx_g.append(i_k)
            val_g.append(jnp.sum(jnp.where(hit, scores_g, 0.0), axis=-1, keepdims=True))
            picked_g = jnp.where(hit, 1.0, picked_g)
            biased_g = jnp.where(hit, -jnp.inf, biased_g)
        return idx_g, val_g, picked_g

    lane_g = lax.broadcasted_iota(I32, (ROUTE_ROWS, LANES), 1).astype(F32)
    groups = [top_k(scores[r:r + ROUTE_ROWS], biased[r:r + ROUTE_ROWS], lane_g) for r in range(0, tm, ROUTE_ROWS)]
    sel_idx = [jnp.concatenate([g[0][k] for g in groups], axis=0) for k in range(TOP_K)]
    sel_val = [jnp.concatenate([g[1][k] for g in groups], axis=0) for k in range(TOP_K)]
    picked = jnp.concatenate([g[2] for g in groups], axis=0)
    total = sel_val[0]
    for v in sel_val[1:]:
        total = total + v
    cnt = jnp.sum(picked, axis=0, keepdims=True)
    ng = jnp.floor((cnt + float(GRAN - 1)) * (1.0 / GRAN))
    e_i = lax.broadcasted_iota(I32, (LANES, LANES), 0)
    e_j = lax.broadcasted_iota(I32, (LANES, LANES), 1)
    upper = jnp.where(e_i < e_j, 1.0, 0.0).astype(BF16)
    loffg = jnp.dot(jnp.broadcast_to(ng, (8, LANES)).astype(BF16), upper, preferred_element_type=F32)[0:1]
    r_i = lax.broadcasted_iota(I32, (tm, tm), 0)
    c_i = lax.broadcasted_iota(I32, (tm, tm), 1)
    before = jnp.dot(jnp.where(c_i < r_i, 1.0, 0.0).astype(BF16), picked.astype(BF16), preferred_element_type=F32)
    base = before + float(GRAN) * loffg
    slot_o = jnp.zeros((tm, LANES), F32)
    wts_o = jnp.zeros((tm, LANES), F32)
    for k in range(TOP_K):
        slot_k = jnp.sum(jnp.where(lane == sel_idx[k], base, 0.0), axis=-1, keepdims=True)
        slot_o = jnp.where(lane == float(k), slot_k, slot_o)
        wts_o = jnp.where(lane == float(k), ROUTED_SCALE * sel_val[k] / total, wts_o)
    slot_ref[...] = slot_o[:, :KPAD].astype(I32)
    wts_ref[...] = wts_o[:, :KPAD]
    slott_ref[0] = slot_o.T[:KPAD, :].astype(I32)

    run = run_ref[...]

    def as_col(v):
        return jnp.sum(jnp.where(e_i == e_j, jnp.broadcast_to(v, (LANES, LANES)), 0.0), axis=1, keepdims=True)

    end_c = as_col(loffg + ng)
    val_c = as_col(run - loffg)
    jj = lax.broadcasted_iota(I32, (LANES, NGRAN), 1).astype(F32)
    ee = lax.broadcasted_iota(I32, (LANES, NGRAN), 0).astype(F32)
    e_of_j = jnp.sum(jnp.where(end_c <= jj, 1.0, 0.0), axis=0, keepdims=True)
    dst = jnp.sum(jnp.where(ee == e_of_j, val_c + jj, 0.0), axis=0, keepdims=True)
    used = jnp.broadcast_to(jnp.sum(ng, axis=-1, keepdims=True), (1, NGRAN))
    gtab_ref[0] = jnp.concatenate([e_of_j, dst, used, jnp.zeros((5, NGRAN), F32)], axis=0).astype(I32)
    new_run = run + ng
    run_ref[...] = new_run
    tot_ref[...] = new_run


def _post_attn(ya, yb, w_out, x, x_tile_off, mods, mod_sel, g_moe, w_router, b_router):
    b, n, _ = ya.shape
    d = D_MODEL
    tm = TOK_TILE
    nt = n // tm
    t_tot = b * n
    wr = jnp.zeros((d, LANES), F32).at[:, :N_EXPERTS].set(w_router.astype(F32))
    br = jnp.zeros((1, LANES), F32).at[0, :N_EXPERTS].set(b_router.astype(F32))
    flat = lambda c: pl.BlockSpec((tm, c), lambda i, t: (i * nt + t, 0))
    outs = pl.pallas_call(
        _post_attn_kernel,
        grid=(b, nt),
        in_specs=[pl.BlockSpec((1, tm, 512), lambda i, t: (i, t, 0)),
                  pl.BlockSpec((1, tm, 512), lambda i, t: (i, t, 0)),
                  pl.BlockSpec((d, d), lambda i, t: (0, 0)),
                  pl.BlockSpec((1, tm, d), lambda i, t: (i, t + x_tile_off, 0)),
                  pl.BlockSpec((1, 1, 6, d), lambda i, t: (i, mod_sel(t), 0, 0)),
                  pl.BlockSpec((1, d), lambda i, t: (0, 0)),
                  pl.BlockSpec((d, LANES), lambda i, t: (0, 0)),
                  pl.BlockSpec((1, LANES), lambda i, t: (0, 0))],
        out_specs=[pl.BlockSpec((1, tm, d), lambda i, t: (i, t, 0)),
                   flat(d), flat(KPAD), flat(KPAD),
                   pl.BlockSpec((1, KPAD, tm), lambda i, t: (i * nt + t, 0, 0)),
                   pl.BlockSpec((1, 8, NGRAN), lambda i, t: (i * nt + t, 0, 0)),
                   pl.BlockSpec((1, LANES), lambda i, t: (0, 0))],
        out_shape=[jax.ShapeDtypeStruct((b, n, d), F32),
                   jax.ShapeDtypeStruct((t_tot, d), BF16),
                   jax.ShapeDtypeStruct((t_tot, KPAD), I32),
                   jax.ShapeDtypeStruct((t_tot, KPAD), F32),
                   jax.ShapeDtypeStruct((b * nt, KPAD, tm), I32),
                   jax.ShapeDtypeStruct((b * nt, 8, NGRAN), I32),
                   jax.ShapeDtypeStruct((1, LANES), F32)],
        scratch_shapes=[pltpu.VMEM((1, LANES), F32)],
        compiler_params=_cparams(("arbitrary", "arbitrary")),
        name="post_attn_route",
    )(ya, yb, w_out.astype(BF16), x, mods, g_moe.reshape(1, d), wr, br)
    return outs


def _granule_copy(src, src_g, dst, dst_g, sem):
    return pltpu.make_async_copy(src.at[pl.ds(pl.multiple_of(src_g * GRAN, GRAN), GRAN), :],
                                 dst.at[pl.ds(pl.multiple_of(dst_g * GRAN, GRAN), GRAN), :], sem)


def _for_each(n, body):
    trips = lax.shift_right_logical(n, jnp.int32(ISSUE_UNROLL.bit_length() - 1))

    def block(b, carry):
        for u in range(ISSUE_UNROLL):
            body(b * ISSUE_UNROLL + u)
        return carry

    def single(j, carry):
        body(j)
        return carry

    lax.fori_loop(0, trips, block, 0)
    lax.fori_loop(trips * ISSUE_UNROLL, n, single, 0)


def _drain(src, dst, sem, n):
    rows = WAIT_BATCH * GRAN
    batches = lax.shift_right_logical(n, jnp.int32(WAIT_BATCH.bit_length() - 1))

    def big(j, carry):
        pltpu.make_async_copy(src.at[pl.ds(0, rows), :], dst.at[pl.ds(0, rows), :], sem).wait()
        return carry

    def small(j, carry):
        _granule_copy(src, 0, dst, 0, sem).wait()
        return carry

    lax.fori_loop(0, batches, big, 0)
    lax.fori_loop(0, n - batches * WAIT_BATCH, small, 0)


def _dispatch_kernel(gtab_ref, h_ref, slott_ref, xs_ref, xloc, gprev, sems, *, nt):
    i = pl.program_id(0)
    cur = i % 2
    used = gtab_ref[0, 2, 0]
    slott = slott_ref[0]
    h = h_ref[...]
    tm = h.shape[0]
    half = h.shape[1] // 2
    def sort_chunk(c):
        rows = c * CHUNK + lax.broadcasted_iota(I32, (CHUNK, tm), 0)
        p = jnp.zeros((CHUNK, tm), F32)
        for k in range(TOP_K):
            p = jnp.where(rows == slott[k:k + 1, :], 1.0, p)
        xc = jnp.dot(p.astype(BF16), h, preferred_element_type=F32)
        lo = lax.bitcast_convert_type(xc[:, :half], U32)
        hi = lax.bitcast_convert_type(xc[:, half:], U32)
        xloc[cur, c * CHUNK:(c + 1) * CHUNK, :] = (hi & jnp.uint32(0xFFFF0000)) | (lo >> 16)

    for c in range(SORT_ROWS // CHUNK):
        if (c + 1) * CHUNK <= ALWAYS_ROWS:
            sort_chunk(c)
        else:
            pl.when(c * CHUNK < used * GRAN)(functools.partial(sort_chunk, c))

    @pl.when(i > 0)
    def _():
        _drain(xloc.at[1 - cur], xs_ref, sems.at[1 - cur], gprev[0])

    _for_each(used, lambda j: _granule_copy(xloc.at[cur], j, xs_ref, gtab_ref[0, 0, j], sems.at[cur]).start())
    gprev[0] = used

    @pl.when(i == nt - 1)
    def _():
        _drain(xloc.at[cur], xs_ref, sems.at[cur], used)


def _dispatch(h2, slott, gtab, n_slots):
    t_tot, d = h2.shape
    tm = TOK_TILE
    nt = t_tot // tm
    return pl.pallas_call(
        functools.partial(_dispatch_kernel, nt=nt),
        grid=(nt,),
        in_specs=[pl.BlockSpec((1, 8, NGRAN), lambda i: (i, 0, 0), memory_space=pltpu.SMEM),
                  pl.BlockSpec((tm, d), lambda i: (i, 0)),
                  pl.BlockSpec((1, KPAD, tm), lambda i: (i, 0, 0))],
        out_specs=pl.BlockSpec(memory_space=pl.ANY),
        out_shape=jax.ShapeDtypeStruct((n_slots, d // 2), U32),
        scratch_shapes=[pltpu.VMEM((2, SORT_ROWS, d // 2), U32), pltpu.SMEM((1,), I32),
                        pltpu.SemaphoreType.DMA((2,))],
        compiler_params=_cparams(("arbitrary",)),
        name="moe_dispatch",
    )(gtab, h2, slott)


def _expert_kernel(te_ref, nused_ref, xs_ref, wg_ref, wu_ref, wd_ref, ys_ref, wgu_s, wd_s):
    i = pl.program_id(0)
    changed = (i == 0) | (te_ref[i] != te_ref[jnp.maximum(i - 1, 0)])

    @pl.when(changed)
    def _():
        wgu_s[:, 0:EXPERT_FF] = wg_ref[0].astype(BF16)
        wgu_s[:, EXPERT_FF:2 * EXPERT_FF] = wu_ref[0].astype(BF16)
        wd_s[...] = wd_ref[0].astype(BF16)

    @pl.when(i < nused_ref[0])
    def _():
        lo, hi = _unpack_bf16_pair(xs_ref[...])
        x = jnp.concatenate([lo.astype(BF16), hi.astype(BF16)], axis=1)
        gu = jnp.dot(x, wgu_s[...], preferred_element_type=F32)
        gate, up = gu[:, :EXPERT_FF], gu[:, EXPERT_FF:]
        act = gate * jax.nn.sigmoid(gate) * up
        y = jnp.dot(act.astype(BF16), wd_s[...], preferred_element_type=F32)
        half = y.shape[1] // 2
        ys_ref[...] = _pack_bf16_pair(y[:, :half], y[:, half:])


def _expert_ffn(xs, tile_expert, n_used, w_gate, w_up, w_down):
    n_slots, w = xs.shape
    ts = SLOT_TILE
    d, f = w_gate.shape[1], w_gate.shape[2]
    grid_spec = pltpu.PrefetchScalarGridSpec(
        num_scalar_prefetch=2,
        grid=(n_slots // ts,),
        in_specs=[pl.BlockSpec((ts, w), lambda i, te, nu: (i, 0)),
                  pl.BlockSpec((1, d, f), lambda i, te, nu: (te[i], 0, 0)),
                  pl.BlockSpec((1, d, f), lambda i, te, nu: (te[i], 0, 0)),
                  pl.BlockSpec((1, f, d), lambda i, te, nu: (te[i], 0, 0))],
        out_specs=pl.BlockSpec((ts, w), lambda i, te, nu: (i, 0)),
        scratch_shapes=[pltpu.VMEM((d, 2 * f), BF16), pltpu.VMEM((f, d), BF16)],
    )
    return pl.pallas_call(
        _expert_kernel,
        grid_spec=grid_spec,
        out_shape=jax.ShapeDtypeStruct((n_slots, w), U32),
        compiler_params=_cparams(("arbitrary",)),
        name="moe_experts",
    )(tile_expert, n_used, xs, w_gate, w_up, w_down)


def _combine_kernel(gtab_ref, gnext_ref, ys_ref, slot_ref, wts_ref, h_ref, x_ref, mod_ref, wsgu_ref,
                    wsd_ref, gf_ref, o_ref, yloc, acc_ref, sems, *, nt, final_norm):
    i = pl.program_id(0)
    cur = i % 2

    def fetch(tab_ref, buf):
        base_g = buf * NGRAN
        sem = sems.at[buf]

        _for_each(tab_ref[0, 2, 0], lambda j: _granule_copy(ys_ref, tab_ref[0, 0, j], yloc, base_g + j, sem).start())

    @pl.when(i == 0)
    def _():
        fetch(gtab_ref, 0)

    @pl.when(i + 1 < nt)
    def _():
        fetch(gnext_ref, 1 - cur)

    h = h_ref[...]
    tm = h.shape[0]
    gu = jnp.dot(h, wsgu_ref[...], preferred_element_type=F32)
    gate, up = gu[:, :SHARED_FF], gu[:, SHARED_FF:]
    shared = jnp.dot((gate * jax.nn.sigmoid(gate) * up).astype(BF16), wsd_ref[...], preferred_element_type=F32)

    used = gtab_ref[0, 2, 0]
    _drain(ys_ref, yloc, sems.at[cur], used)
    row0 = cur * SORT_ROWS

    slot_b = [jnp.broadcast_to(slot_ref[:, k:k + 1], (tm, LANES)) for k in range(TOP_K)]
    wts_b = [jnp.broadcast_to(wts_ref[:, k:k + 1], (tm, LANES)) for k in range(TOP_K)]
    lane = lax.broadcasted_iota(I32, (tm, LANES), 1)
    w = yloc.shape[1]

    def weights(c):
        blocks = []
        for j in range(CHUNK // LANES):
            cols = lane + (c * CHUNK + j * LANES)
            pw = jnp.zeros((tm, LANES), F32)
            for k in range(TOP_K):
                pw = jnp.where(cols == slot_b[k], wts_b[k], pw)
            blocks.append(pw.astype(BF16))
        return jnp.concatenate(blocks, axis=1)

    def values(c, masked):
        packed = yloc[pl.ds(pl.multiple_of(row0 + c * CHUNK, CHUNK), CHUNK), :]
        if masked:
            rows = c * CHUNK + lax.broadcasted_iota(I32, (CHUNK, w), 0)
            packed = jnp.where(rows < used * GRAN, packed, jnp.uint32(0))
        lo, hi = _unpack_bf16_pair(packed)
        return jnp.concatenate([lo.astype(BF16), hi.astype(BF16)], axis=1)

    n_always = ALWAYS_ROWS // CHUNK
    acc = shared
    for c in range(n_always):
        acc = acc + jnp.dot(weights(c), values(c, False), preferred_element_type=F32)
    acc_ref[...] = acc
    for c in range(n_always, SORT_ROWS // CHUNK):
        @pl.when(c * CHUNK < used * GRAN)
        def _():
            acc_ref[...] += jnp.dot(weights(c), values(c, True), preferred_element_type=F32)

    out = x_ref[...] + mod_ref[0, 0, 5:6, :] * acc_ref[...]
    if final_norm:
        out = _rms(out, gf_ref[...])
    o_ref[...] = out


def _combine(ys, slot, wts, gtab, h2, xmid, mods, mod_map, ws_gate, ws_up, ws_down, g_final, final_norm):
    t_tot, d = h2.shape
    tm = TOK_TILE
    nt = t_tot // tm
    wsgu = jnp.concatenate([ws_gate, ws_up], axis=1).astype(BF16)
    return pl.pallas_call(
        functools.partial(_combine_kernel, nt=nt, final_norm=final_norm),
        grid=(nt,),
        in_specs=[pl.BlockSpec((1, 8, NGRAN), lambda i: (i, 0, 0), memory_space=pltpu.SMEM),
                  pl.BlockSpec((1, 8, NGRAN), lambda i: (jnp.minimum(i + 1, nt - 1), 0, 0), memory_space=pltpu.SMEM),
                  pl.BlockSpec(memory_space=pl.ANY),
                  pl.BlockSpec((tm, KPAD), lambda i: (i, 0)),
                  pl.BlockSpec((tm, KPAD), lambda i: (i, 0)),
                  pl.BlockSpec((tm, d), lambda i: (i, 0)),
                  pl.BlockSpec((tm, d), lambda i: (i, 0)),
                  pl.BlockSpec((1, 1, 6, d), mod_map),
                  pl.BlockSpec((d, 2 * SHARED_FF), lambda i: (0, 0)),
                  pl.BlockSpec((SHARED_FF, d), lambda i: (0, 0)),
                  pl.BlockSpec((1, d), lambda i: (0, 0))],
        out_specs=pl.BlockSpec((tm, d), lambda i: (i, 0)),
        out_shape=jax.ShapeDtypeStruct((t_tot, d), F32),
        scratch_shapes=[pltpu.VMEM((2 * SORT_ROWS, d // 2), U32), pltpu.VMEM((tm, d), F32),
                        pltpu.SemaphoreType.DMA((2,))],
        compiler_params=_cparams(("arbitrary",)),
        name="moe_combine",
    )(gtab, gtab, ys, slot, wts, h2, xmid.reshape(t_tot, d), mods, wsgu, ws_down.astype(BF16),
      g_final.reshape(1, d).astype(F32))


def _moe(h2, slot, wts, slott, gtab, tot, xmid, mods, mod_map, moe_w, g_final, final_norm):
    (w_gate, w_up, w_down, ws_gate, ws_up, ws_down) = moe_w
    t_tot = h2.shape[0]
    ts = SLOT_TILE
    gpt = ts // GRAN
    nt = t_tot // TOK_TILE
    max_rows = t_tot * TOP_K + nt * N_EXPERTS * (GRAN - 1)
    n_tiles = -(-max_rows // ts) + N_EXPERTS
    totg = tot[0, :N_EXPERTS].astype(I32)
    tiles_e = (totg + gpt - 1) // gpt
    ends = jnp.cumsum(tiles_e)
    poffg = jnp.zeros((LANES,), I32).at[:N_EXPERTS].set((ends - tiles_e) * gpt)
    n_used = ends[-1:]
    tile_ids = jnp.minimum(jnp.arange(n_tiles, dtype=I32), n_used[0] - 1)
    tile_expert = jnp.sum((ends[None, :] <= tile_ids[:, None]).astype(I32), axis=1)
    tile_expert = jnp.minimum(tile_expert, N_EXPERTS - 1)
    region = jnp.sum(jnp.where(gtab[:, 0, :, None] == jnp.arange(N_EXPERTS, dtype=I32), poffg[:N_EXPERTS], 0), axis=-1)
    gtab = gtab.at[:, 0, :].set(region + gtab[:, 1, :])
    xs = _dispatch(h2, slott, gtab, n_tiles * ts)
    ys = _expert_ffn(xs, tile_expert, n_used.astype(I32), w_gate, w_up, w_down)
    return _combine(ys, slot, wts, gtab, h2, xmid, mods, mod_map, ws_gate, ws_up, ws_down, g_final, final_norm)


def kernel(x, c, ctx, c_ctx, l0_w_ada, l0_b_ada, l0_g_mix, l0_w_in, l0_sink, l0_rpb, l0_w_out, l0_g_moe, l0_w_router, l0_b_router, l0_w_gate, l0_w_up, l0_w_down, l0_ws_gate, l0_ws_up, l0_ws_down, l1_w_ada, l1_b_ada, l1_g_mix, l1_w_in, l1_g_q_lora, l1_w_uq, l1_g_kv_lora, l1_w_ukv, l1_g_qn, l1_g_kn, l1_w_out, l1_g_moe, l1_w_router, l1_b_router, l1_w_gate, l1_w_up, l1_w_down, l1_ws_gate, l1_ws_up, l1_ws_down, g_final):
    b, seq, d = x.shape
    assert d == D_MODEL and ctx.shape[1] == CTX_LEN and seq % TOK_TILE == 0
    u = CTX_LEN + seq
    tiles_u = u // TOK_TILE
    tiles_s = seq // TOK_TILE
    xa = jnp.concatenate([ctx, x], axis=1).astype(F32)

    cos_a, sin_a = _rope_tables(seq, HEAD_DIM, 0, HEAD_DIM, 1.0)
    cos_q, sin_q = _rope_tables(seq, C_ROPE, C_NOPE, LANES, float((C_NOPE + C_ROPE) ** -0.5))
    cos_k, sin_k = _rope_tables(seq, C_ROPE, 0, LANES, 1.0)

    mods0 = _mods(c, c_ctx, l0_w_ada, l0_b_ada)
    qa, kva, qb, kb, vb = _proj0(xa, mods0, l0_g_mix, l0_w_in, cos_a, sin_a)
    ya = _window_attention(qa, kva, l0_sink)
    yb = _na_attention(qb, kb, vb, l0_rpb)
    xmid, h2, slot, wts, slott, gtab, tot = _post_attn(ya, yb, l0_w_out, xa, 0, mods0, lambda t: jnp.minimum(t, 1),
                                                   l0_g_moe, l0_w_router, l0_b_router)
    xa = _moe(h2, slot, wts, slott, gtab, tot, xmid, mods0,
              lambda i: (i // tiles_u, jnp.minimum(i % tiles_u, 1), 0, 0),
              (l0_w_gate, l0_w_up, l0_w_down, l0_ws_gate, l0_ws_up, l0_ws_down), g_final, False).reshape(b, u, d)

    mods1 = _mods(c, c_ctx, l1_w_ada, l1_b_ada)
    qm, qd, km, vm, kvd = _proj1(xa, mods1, l1_g_mix, l1_w_in, l1_g_q_lora, l1_w_uq, l1_g_kv_lora, l1_w_ukv,
                                 l1_g_qn, l1_g_kn, (cos_a, sin_a, cos_q, sin_q, cos_k, sin_k))
    ym = _mla_attention(qm, km, vm)
    yd = _gqa_attention(qd, kvd)
    xmid, h2, slot, wts, slott, gtab, tot = _post_attn(ym, yd, l1_w_out, xa, CTX_LEN // TOK_TILE, mods1, lambda t: 1,
                                                   l1_g_moe, l1_w_router, l1_b_router)
    out = _moe(h2, slot, wts, slott, gtab, tot, xmid, mods1, lambda i: (i // tiles_s, 1, 0, 0),
               (l1_w_gate, l1_w_up, l1_w_down, l1_ws_gate, l1_ws_up, l1_ws_down), g_final, True)
    return out.reshape(b, seq, d)
```

```python
import functools

import numpy as np
import jax
import jax.numpy as jnp
from jax import lax
from jax.experimental import pallas as pl
from jax.experimental.pallas import tpu as pltpu

F32 = jnp.float32
BF16 = jnp.bfloat16
U32 = jnp.uint32
I32 = jnp.int32

D_MODEL = 1024
CTX_LEN = 256
GRID_W = 64
HEAD_DIM = 64
ROPE_THETA = 10000.0
NORM_EPS = 1e-6
NEG_INF = -1e30
A_WINDOW = 128
NA_ROWS = 8
NA_COLS = 16
NA_QROWS = 4
NA_BAND = NA_ROWS + NA_QROWS - 1
C_Q_LORA = 384
C_KV_LORA = 256
C_NOPE = 64
C_ROPE = 32
C_V = 64
N_EXPERTS = 64
TOP_K = 6
EXPERT_FF = 256
SHARED_FF = 256
ROUTED_SCALE = 2.5

LANES = 128
TOK_TILE = 256
SLOT_TILE = 1024
KPAD = 8
GRAN = 8
CHUNK = 256
SORT_ROWS = 2048
NGRAN = SORT_ROWS // GRAN
ALWAYS_ROWS = TOK_TILE * TOP_K
WAIT_BATCH = 16
ISSUE_UNROLL = 4
ROUTE_ROWS = 256
VMEM_LIMIT = 48 * 1024 * 1024


def _cparams(sem):
    return pltpu.CompilerParams(dimension_semantics=sem, vmem_limit_bytes=VMEM_LIMIT)


def _rms(x, g):
    return x * lax.rsqrt(jnp.mean(x * x, axis=-1, keepdims=True) + NORM_EPS) * g


def _norm_mod(x, g, shift, scale):
    return _rms(x, g) * (1.0 + scale) + shift


def _rope128(x, cos, sin):
    lane = lax.broadcasted_iota(I32, x.shape, 1)
    swapped = jnp.where(lane % 2 == 0, pltpu.roll(x, LANES - 1, 1), pltpu.roll(x, 1, 1))
    return x * cos + swapped * sin


def _group_sumsq(x, ones_bd):
    sq = x * x
    hi = sq.astype(BF16)
    lo = (sq - hi.astype(F32)).astype(BF16)
    return (jnp.dot(hi, ones_bd, preferred_element_type=F32) + jnp.dot(lo, ones_bd, preferred_element_type=F32))


def _head_norm(x, ones_bd, g):
    return x * lax.rsqrt(_group_sumsq(x, ones_bd) * (1.0 / HEAD_DIM) + NORM_EPS) * g


def _pack_bf16_pair(lo, hi):
    lo_bits = lax.bitcast_convert_type(lo.astype(BF16).astype(F32), U32)
    hi_bits = lax.bitcast_convert_type(hi.astype(BF16).astype(F32), U32)
    return (hi_bits & jnp.uint32(0xFFFF0000)) | (lo_bits >> 16)


def _unpack_bf16_pair(u):
    lo = lax.bitcast_convert_type(u << 16, F32)
    hi = lax.bitcast_convert_type(u & jnp.uint32(0xFFFF0000), F32)
    return lo, hi


def _keep_half(q128, half):
    lane = lax.broadcasted_iota(I32, q128.shape, 1)
    keep = (lane < HEAD_DIM) if half == 0 else (lane >= HEAD_DIM)
    return jnp.where(keep, q128, jnp.zeros_like(q128))


def _merge_halves(o_even, o_odd):
    lane = lax.broadcasted_iota(I32, o_even.shape, 1)
    return jnp.where(lane < HEAD_DIM, o_even, o_odd)


def _qk(q, k):
    return lax.dot_general(q, k, (((1,), (1,)), ((), ())), preferred_element_type=F32)


def _softmax_pv(s, v, sinks=None):
    m = jnp.max(s, axis=-1, keepdims=True)
    if sinks is not None:
        r = s.shape[0] // len(sinks)
        m = jnp.concatenate([jnp.maximum(m[i * r:(i + 1) * r], sinks[i]) for i in range(len(sinks))], axis=0)
    e = jnp.exp(s - m)
    den = jnp.sum(e, axis=-1, keepdims=True)
    if sinks is not None:
        den = den + jnp.concatenate([jnp.exp(sinks[i] - m[i * r:(i + 1) * r]) for i in range(len(sinks))], axis=0)
    return jnp.dot(e.astype(BF16), v, preferred_element_type=F32) / den


def _ada_kernel(c_ref, w_ref, b_ref, o_ref):
    c = c_ref[...]
    a = c * jax.nn.sigmoid(c)
    o_ref[...] = jnp.dot(a, w_ref[...], precision=lax.Precision.HIGHEST, preferred_element_type=F32) + b_ref[...]


def _ada(cond, w_ada, b_ada):
    n, d = cond.shape
    nout = w_ada.shape[1]
    bn = 512
    return pl.pallas_call(
        _ada_kernel,
        grid=(nout // bn,),
        in_specs=[pl.BlockSpec((n, d), lambda j: (0, 0)),
                  pl.BlockSpec((d, bn), lambda j: (0, j)),
                  pl.BlockSpec((1, bn), lambda j: (0, j))],
        out_specs=pl.BlockSpec((n, bn), lambda j: (0, j)),
        out_shape=jax.ShapeDtypeStruct((n, nout), F32),
        compiler_params=_cparams(("arbitrary",)),
        name="ada",
    )(cond, w_ada, b_ada.reshape(1, nout))


def _mods(c, c_ctx, w_ada, b_ada):
    b = c.shape[0]
    rows = ((b + 1 + 7) // 8) * 8
    cond = jnp.zeros((rows, D_MODEL), F32).at[:b].set(c).at[b].set(c_ctx)
    out = _ada(cond, w_ada, b_ada)
    lat = out[:b].reshape(b, 1, 6, D_MODEL)
    cx = jnp.broadcast_to(out[b].reshape(1, 1, 6, D_MODEL), (b, 1, 6, D_MODEL))
    return jnp.concatenate([cx, lat], axis=1)


def _axial_angles(n_tokens, rot_dim):
    t = np.arange(n_tokens)
    row = (t // GRID_W).astype(np.float32)
    col = (t % GRID_W).astype(np.float32)
    n_axis = rot_dim // 4
    inv_freq = (np.float32(ROPE_THETA) ** (-np.arange(n_axis, dtype=np.float32) / n_axis)).astype(np.float32)
    return jnp.concatenate([jnp.asarray(row[:, None] * inv_freq), jnp.asarray(col[:, None] * inv_freq)], axis=-1)


def _rope_tables(seq, rot_dim, lane_start, period, scale):
    ang = _axial_angles(seq, rot_dim)
    cos = jnp.repeat(jnp.cos(ang), 2, axis=-1)
    sin = jnp.repeat(jnp.sin(ang), 2, axis=-1) * jnp.tile(jnp.asarray([-1.0, 1.0], F32), rot_dim // 2)
    cos_p = jnp.ones((seq, period), F32).at[:, lane_start:lane_start + rot_dim].set(cos)
    sin_p = jnp.zeros((seq, period), F32).at[:, lane_start:lane_start + rot_dim].set(sin)
    cos_f = jnp.concatenate([jnp.ones((CTX_LEN, period), F32), cos_p], axis=0)
    sin_f = jnp.concatenate([jnp.zeros((CTX_LEN, period), F32), sin_p], axis=0)
    reps = LANES // period
    return jnp.tile(cos_f, (1, reps)) * scale, jnp.tile(sin_f, (1, reps)) * scale


def _na_band_start(block, rows):
    return np.clip(block * NA_QROWS - NA_ROWS // 2, 0, rows - NA_BAND)


def _na_bias_table(rpb, rows):
    h = rpb.shape[0]
    qc = np.arange(GRID_W)
    kc = np.arange(GRID_W)
    cstart = np.clip(qc - NA_COLS // 2, 0, GRID_W - NA_COLS)
    col_ok = (kc[None, :] >= cstart[:, None]) & (kc[None, :] < cstart[:, None] + NA_COLS)
    dc = np.clip(kc[None, :] - qc[:, None] + NA_COLS - 1, 0, 2 * NA_COLS - 2)
    col_sel = np.eye(2 * NA_COLS - 1, dtype=np.float32)[dc]
    variants, ids = [], [None]
    for blk in range(rows // NA_QROWS):
        rs0 = _na_band_start(blk, rows)
        r = blk * NA_QROWS + np.arange(NA_QROWS)
        rs = np.clip(r - NA_ROWS // 2, 0, rows - NA_ROWS)
        krow = rs0 + np.arange(NA_BAND)
        row_ok = (krow[None, :] >= rs[:, None]) & (krow[None, :] < rs[:, None] + NA_ROWS)
        dr = np.clip(krow[None, :] - r[:, None] + NA_ROWS - 1, 0, 2 * NA_ROWS - 2)
        key = (row_ok.tobytes(), dr.tobytes())
        if key not in [v[0] for v in variants]:
            variants.append((key, row_ok, dr))
        ids.append([v[0] for v in variants].index(key))
    ids[0] = len(variants)
    tabs = []
    for _, row_ok, dr in variants:
        row_sel = np.eye(2 * NA_ROWS - 1, dtype=np.float32)[dr]
        b = jnp.einsum('hrc,ijr,qkc->hiqjk', rpb.astype(F32), jnp.asarray(row_sel), jnp.asarray(col_sel),
                       precision=lax.Precision.HIGHEST)
        ok = row_ok[:, None, :, None] & col_ok[None, :, None, :]
        b = jnp.where(jnp.asarray(ok)[None], b, NEG_INF)
        tabs.append(b.reshape(h, NA_QROWS * GRID_W, NA_BAND * GRID_W))
    tabs.append(jnp.full((h, NA_QROWS * GRID_W, NA_BAND * GRID_W), NEG_INF, F32))
    loc = jnp.stack(tabs, axis=0)
    tab = jnp.concatenate([jnp.zeros(loc.shape[:3] + (CTX_LEN,), F32), loc], axis=-1)
    return tab, jnp.asarray(ids, I32)


def _dup_heads(w, n_heads):
    d = w.shape[0]
    w = w.reshape(d, n_heads, 1, HEAD_DIM)
    return jnp.broadcast_to(w, (d, n_heads, 2, HEAD_DIM)).reshape(d, n_heads * 2 * HEAD_DIM)


def _proj0_kernel(x_ref, mod_ref, g_ref, w_ref, cos_ref, sin_ref, qa_ref, kva_ref, qb_ref, kb_ref, vb_ref):
    h = _norm_mod(x_ref[0], g_ref[...], mod_ref[0, 0, 0:1, :], mod_ref[0, 0, 1:2, :])
    r = jnp.dot(h.astype(BF16), w_ref[...], preferred_element_type=F32)
    cos, sin = cos_ref[...], sin_ref[...]
    roped = [_rope128(r[:, i * LANES:(i + 1) * LANES], cos, sin) for i in range(6)]
    qa_ref[0] = jnp.concatenate(roped[0:4], axis=1).astype(BF16)
    kva_ref[0] = jnp.concatenate(roped[4:6] + [r[:, 768:1024]], axis=1).astype(BF16)
    qb_ref[0] = r[:, 1024:1536].astype(BF16)
    kb_ref[0] = r[:, 1536:2048].astype(BF16)
    vb_ref[0] = r[:, 2048:2560].astype(BF16)


def _proj0(xa, mods, g_mix, w_in, cos, sin):
    b, u, d = xa.shape
    tm = TOK_TILE
    s = 1.0 / 8.0
    w = jnp.concatenate([w_in[:, 0:512] * s, _dup_heads(w_in[:, 512:640], 2), _dup_heads(w_in[:, 640:768], 2),
                         w_in[:, 768:1280] * s, w_in[:, 1280:1792], w_in[:, 1792:2304]], axis=1).astype(BF16)
    nw = w.shape[1]
    tok = lambda n: pl.BlockSpec((1, tm, n), lambda i, t: (i, t, 0))
    return pl.pallas_call(
        _proj0_kernel,
        grid=(b, u // tm),
        in_specs=[tok(d),
                  pl.BlockSpec((1, 1, 6, d), lambda i, t: (i, jnp.minimum(t, 1), 0, 0)),
                  pl.BlockSpec((1, d), lambda i, t: (0, 0)),
                  pl.BlockSpec((d, nw), lambda i, t: (0, 0)),
                  pl.BlockSpec((tm, LANES), lambda i, t: (t, 0)),
                  pl.BlockSpec((tm, LANES), lambda i, t: (t, 0))],
        out_specs=[tok(512)] * 5,
        out_shape=[jax.ShapeDtypeStruct((b, u, 512), BF16)] * 5,
        compiler_params=_cparams(("arbitrary", "arbitrary")),
        name="proj0",
    )(xa, mods, g_mix.reshape(1, d), w, cos, sin)


def _window_kernel(sink_ref, q_ref, kv_ref, o_ref, *, tq, span):
    n = pl.program_id(1)
    u = kv_ref.shape[1]
    ls = pl.multiple_of(jnp.clip((n - 1) * tq, CTX_LEN, u - span), tq)
    nk = CTX_LEN + span
    row = lax.broadcasted_iota(I32, (4 * tq, nk), 0)
    col = lax.broadcasted_iota(I32, (4 * tq, nk), 1)
    qpos = n * tq + (row & (tq - 1))
    kpos = ls + col - CTX_LEN
    valid = (col < CTX_LEN) | ((kpos >= CTX_LEN) & (jnp.abs(qpos - kpos) <= A_WINDOW) & (qpos >= CTX_LEN))
    kv_all = jnp.concatenate([kv_ref[0, 0:CTX_LEN, :], kv_ref[0, pl.ds(ls, span), :]], axis=0)
    outs = [None] * 8
    for g in range(2):
        q4 = jnp.concatenate([_keep_half(q_ref[0, :, ((4 * g + hh) // 2) * LANES:((4 * g + hh) // 2 + 1) * LANES],
                                         hh % 2) for hh in range(4)], axis=0)
        s = jnp.where(valid, _qk(q4, kv_all[:, g * LANES:(g + 1) * LANES]), NEG_INF)
        o = _softmax_pv(s, kv_all[:, 256 + g * LANES:256 + (g + 1) * LANES],
                        [sink_ref[4 * g + hh] for hh in range(4)])
        for hh in range(4):
            outs[4 * g + hh] = o[hh * tq:(hh + 1) * tq]
    o_ref[0] = jnp.concatenate([_merge_halves(outs[2 * p], outs[2 * p + 1]) for p in range(4)], axis=1).astype(BF16)


def _window_attention(qa, kva, sink):
    b, u, _ = qa.shape
    tq = A_WINDOW
    span = 3 * A_WINDOW
    return pl.pallas_call(
        functools.partial(_window_kernel, tq=tq, span=span),
        grid=(b, u // tq),
        in_specs=[pl.BlockSpec(memory_space=pltpu.SMEM),
                  pl.BlockSpec((1, tq, 512), lambda i, n: (i, n, 0)),
                  pl.BlockSpec((1, u, 512), lambda i, n: (i, 0, 0))],
        out_specs=pl.BlockSpec((1, tq, 512), lambda i, n: (i, n, 0)),
        out_shape=jax.ShapeDtypeStruct((b, u, 512), BF16),
        compiler_params=_cparams(("arbitrary", "arbitrary")),
        name="window_attn",
    )(sink.astype(F32), qa, kva)


def _na_kernel(var_ref, q_ref, k_ref, v_ref, bias_ref, o_ref, *, rows):
    j = pl.program_id(1)
    rs0 = jnp.clip((j - 1) * NA_QROWS - NA_ROWS // 2, 0, rows - NA_BAND)
    band = NA_BAND * GRID_W
    start = pl.multiple_of(CTX_LEN + rs0 * GRID_W, GRID_W)
    k_all = jnp.concatenate([k_ref[0, 0:CTX_LEN, :], k_ref[0, pl.ds(start, band), :]], axis=0)
    v_all = jnp.concatenate([v_ref[0, 0:CTX_LEN, :], v_ref[0, pl.ds(start, band), :]], axis=0)
    tq = q_ref.shape[1]
    outs = []
    for p in range(4):
        q128 = q_ref[0, :, p * LANES:(p + 1) * LANES]
        q2 = jnp.concatenate([_keep_half(q128, 0), _keep_half(q128, 1)], axis=0)
        s = _qk(q2, k_all[:, p * LANES:(p + 1) * LANES])
        s = s + jnp.concatenate([bias_ref[0, 2 * p], bias_ref[0, 2 * p + 1]], axis=0)
        o = _softmax_pv(s, v_all[:, p * LANES:(p + 1) * LANES])
        outs.append(_merge_halves(o[:tq], o[tq:]))
    o_ref[0] = jnp.concatenate(outs, axis=1).astype(BF16)


def _na_attention(qb, kb, vb, rpb):
    b, u, _ = qb.shape
    rows = (u - CTX_LEN) // GRID_W
    assert rows >= NA_BAND and rows % NA_QROWS == 0
    bias_tab, variant = _na_bias_table(rpb, rows)
    tq = NA_QROWS * GRID_W
    assert tq == CTX_LEN
    nk = bias_tab.shape[-1]
    grid_spec = pltpu.PrefetchScalarGridSpec(
        num_scalar_prefetch=1,
        grid=(b, u // tq),
        in_specs=[pl.BlockSpec((1, tq, 512), lambda i, j, var: (i, j, 0)),
                  pl.BlockSpec((1, u, 512), lambda i, j, var: (i, 0, 0)),
                  pl.BlockSpec((1, u, 512), lambda i, j, var: (i, 0, 0)),
                  pl.BlockSpec((1, 8, tq, nk), lambda i, j, var: (var[j], 0, 0, 0))],
        out_specs=pl.BlockSpec((1, tq, 512), lambda i, j, var: (i, j, 0)),
    )
    return pl.pallas_call(
        functools.partial(_na_kernel, rows=rows),
        grid_spec=grid_spec,
        out_shape=jax.ShapeDtypeStruct((b, u, 512), BF16),
        compiler_params=_cparams(("arbitrary", "arbitrary")),
        name="na_attn",
    )(variant, qb, kb, vb, bias_tab)


def _proj1_kernel(x_ref, mod_ref, g_ref, w_ref, gq_ref, wuq_ref, gkv_ref, wkv_ref, gqn_ref, gkn_ref, ones_ref,
                  cosa_ref, sina_ref, cosq_ref, sinq_ref, cosk_ref, sink_ref,
                  qm_ref, qd_ref, km_ref, vm_ref, kvd_ref):
    h = _norm_mod(x_ref[0], g_ref[...], mod_ref[0, 0, 0:1, :], mod_ref[0, 0, 1:2, :])
    r = jnp.dot(h.astype(BF16), w_ref[...], preferred_element_type=F32)
    ones_bd = ones_ref[...]
    cosa, sina = cosa_ref[...], sina_ref[...]
    cq = _rms(r[:, 0:C_Q_LORA], gq_ref[...])
    qm = jnp.dot(cq.astype(BF16), wuq_ref[...], preferred_element_type=F32)
    cosq, sinq = cosq_ref[...], sinq_ref[...]
    qm_ref[0] = jnp.concatenate([_rope128(qm[:, i * LANES:(i + 1) * LANES], cosq, sinq) for i in range(8)],
                                axis=1).astype(BF16)
    gqn = gqn_ref[...]
    qd_ref[0] = jnp.concatenate(
        [_rope128(_head_norm(r[:, 384 + i * LANES:384 + (i + 1) * LANES], ones_bd, gqn), cosa, sina)
         for i in range(4)], axis=1).astype(BF16)
    ckv = _rms(r[:, 896:1152], gkv_ref[...])
    kr = _rope128(r[:, 1152:1280], cosk_ref[...], sink_ref[...])
    kv = jnp.dot(jnp.concatenate([ckv, kr], axis=1).astype(BF16), wkv_ref[...], preferred_element_type=F32)
    km_ref[0] = kv[:, 0:1024].astype(BF16)
    vm_ref[0] = kv[:, 1024:1536].astype(BF16)
    gkn = gkn_ref[...]
    kd = [_rope128(_head_norm(r[:, 1280 + i * LANES:1280 + (i + 1) * LANES], ones_bd, gkn), cosa, sina)
          for i in range(2)]
    kvd_ref[0] = jnp.concatenate(kd + [r[:, 1536:1792]], axis=1).astype(BF16)


def _proj1(xa, mods, g_mix, w_in, g_q_lora, w_uq, g_kv_lora, w_ukv, g_qn, g_kn, tabs):
    b, u, d = xa.shape
    tm = TOK_TILE
    zpad = jnp.zeros((d, LANES - C_ROPE), F32)
    w = jnp.concatenate([w_in[:, 0:896], w_in[:, 896:1152], w_in[:, 1152:1184], zpad,
                         _dup_heads(w_in[:, 1184:1312], 2), _dup_heads(w_in[:, 1312:1440], 2)], axis=1).astype(BF16)
    nw = w.shape[1]
    wuq = w_uq.reshape(C_Q_LORA, 8, C_NOPE + C_ROPE)
    wuq = jnp.concatenate([wuq, jnp.zeros((C_Q_LORA, 8, LANES - C_NOPE - C_ROPE), F32)], axis=-1)
    wuq = wuq.reshape(C_Q_LORA, 8 * LANES).astype(BF16)
    wukv = w_ukv.reshape(C_KV_LORA, 8, C_NOPE + C_V)
    wk = jnp.concatenate([wukv[:, :, :C_NOPE], jnp.zeros((C_KV_LORA, 8, LANES - C_NOPE), F32)], axis=-1)
    wk = wk.reshape(C_KV_LORA, 8 * LANES)
    wv = wukv[:, :, C_NOPE:].reshape(C_KV_LORA, 8 * C_V)
    place = np.zeros((LANES, 8, LANES), np.float32)
    for j in range(C_ROPE):
        place[j, :, C_NOPE + j] = 1.0
    place = jnp.asarray(place.reshape(LANES, 8 * LANES))
    wkv = jnp.concatenate([jnp.concatenate([wk, wv], axis=1),
                           jnp.concatenate([place, jnp.zeros((LANES, 8 * C_V), F32)], axis=1)], axis=0).astype(BF16)
    ones_bd = jnp.asarray(np.kron(np.eye(2, dtype=np.float32), np.ones((HEAD_DIM, HEAD_DIM), np.float32))).astype(BF16)
    gqn = jnp.tile(g_qn.astype(F32) * (1.0 / 8.0), 2).reshape(1, LANES)
    gkn = jnp.tile(g_kn.astype(F32), 2).reshape(1, LANES)
    cosa, sina, cosq, sinq, cosk, sink = tabs
    tok = lambda n: pl.BlockSpec((1, tm, n), lambda i, t: (i, t, 0))
    whole = lambda a: pl.BlockSpec(a.shape, lambda i, t: (0,) * a.ndim)
    tab = pl.BlockSpec((tm, LANES), lambda i, t: (t, 0))
    small = [g_mix.reshape(1, d), w, g_q_lora.reshape(1, -1).astype(F32), wuq, g_kv_lora.reshape(1, -1).astype(F32),
             wkv, gqn, gkn, ones_bd]
    return pl.pallas_call(
        _proj1_kernel,
        grid=(b, u // tm),
        in_specs=[tok(d), pl.BlockSpec((1, 1, 6, d), lambda i, t: (i, jnp.minimum(t, 1), 0, 0))]
                 + [whole(a) for a in small] + [tab] * 6,
        out_specs=[tok(1024), tok(512), tok(1024), tok(512), tok(512)],
        out_shape=[jax.ShapeDtypeStruct((b, u, n), BF16) for n in (1024, 512, 1024, 512, 512)],
        compiler_params=_cparams(("arbitrary", "arbitrary")),
        name="proj1",
    )(xa, mods, *small, cosa, sina, cosq, sinq, cosk, sink)


def _mla_kernel(q_ref, k_ref, v_ref, o_ref):
    outs = []
    for h in range(8):
        s = _qk(q_ref[0, :, h * LANES:(h + 1) * LANES], k_ref[0, :, h * LANES:(h + 1) * LANES])
        outs.append(_softmax_pv(s, v_ref[0, :, (h // 2) * LANES:(h // 2 + 1) * LANES]))
    o_ref[0] = jnp.concatenate([_merge_halves(outs[2 * p], outs[2 * p + 1]) for p in range(4)], axis=1).astype(BF16)


def _mla_attention(qm, km, vm):
    b, u, _ = qm.shape
    tq = 256
    nq = (u - CTX_LEN) // tq
    off = CTX_LEN // tq
    return pl.pallas_call(
        _mla_kernel,
        grid=(b, nq),
        in_specs=[pl.BlockSpec((1, tq, 1024), lambda i, n: (i, n + off, 0)),
                  pl.BlockSpec((1, u, 1024), lambda i, n: (i, 0, 0)),
                  pl.BlockSpec((1, u, 512), lambda i, n: (i, 0, 0))],
        out_specs=pl.BlockSpec((1, tq, 512), lambda i, n: (i, n, 0)),
        out_shape=jax.ShapeDtypeStruct((b, u - CTX_LEN, 512), BF16),
        compiler_params=_cparams(("arbitrary", "arbitrary")),
        name="mla_attn",
    )(qm, km, vm)


def _gqa_kernel(q_ref, kv_ref, o_ref):
    outs = [None] * 8
    for g in range(2):
        k = kv_ref[0, :, g * LANES:(g + 1) * LANES]
        v = kv_ref[0, :, 256 + g * LANES:256 + (g + 1) * LANES]
        for hh in range(4):
            h = 4 * g + hh
            q = _keep_half(q_ref[0, :, (h // 2) * LANES:(h // 2 + 1) * LANES], h % 2)
            outs[h] = _softmax_pv(_qk(q, k), v)
    o_ref[0] = jnp.concatenate([_merge_halves(outs[2 * p], outs[2 * p + 1]) for p in range(4)], axis=1).astype(BF16)


def _gqa_attention(qd, kvd):
    b, u, _ = qd.shape
    tq = 256
    nq = (u - CTX_LEN) // tq
    off = CTX_LEN // tq
    return pl.pallas_call(
        _gqa_kernel,
        grid=(b, nq),
        in_specs=[pl.BlockSpec((1, tq, 512), lambda i, n: (i, n + off, 0)),
                  pl.BlockSpec((1, u, 512), lambda i, n: (i, 0, 0))],
        out_specs=pl.BlockSpec((1, tq, 512), lambda i, n: (i, n, 0)),
        out_shape=jax.ShapeDtypeStruct((b, u - CTX_LEN, 512), BF16),
        compiler_params=_cparams(("arbitrary", "arbitrary")),
        name="gqa_attn",
    )(qd, kvd)


def _post_attn_kernel(ya_ref, yb_ref, w_ref, x_ref, mod_ref, g_ref, wr_ref, br_ref,
                      xmid_ref, h2_ref, slot_ref, wts_ref, slott_ref, gtab_ref, tot_ref, run_ref):
    first = (pl.program_id(0) == 0) & (pl.program_id(1) == 0)

    @pl.when(first)
    def _():
        run_ref[...] = jnp.zeros_like(run_ref)

    y = (jnp.dot(ya_ref[0], w_ref[0:512, :], preferred_element_type=F32)
         + jnp.dot(yb_ref[0], w_ref[512:1024, :], preferred_element_type=F32))
    x1 = x_ref[0] + mod_ref[0, 0, 2:3, :] * y
    xmid_ref[0] = x1
    h2 = _norm_mod(x1, g_ref[...], mod_ref[0, 0, 3:4, :], mod_ref[0, 0, 4:5, :])
    h_hi = h2.astype(BF16)
    h2_ref[...] = h_hi

    h_lo = (h2 - h_hi.astype(F32)).astype(BF16)
    logits = jnp.dot(jnp.concatenate([h_hi, h_hi, h_lo], axis=1), wr_ref[...], preferred_element_type=F32)
    scores = jax.nn.sigmoid(logits)
    tm = scores.shape[0]
    lane = lax.broadcasted_iota(I32, (tm, LANES), 1).astype(F32)
    biased = jnp.where(lane < N_EXPERTS, scores + br_ref[...], -jnp.inf)
    def top_k(scores_g, biased_g, lane_g):
        picked_g = jnp.zeros_like(scores_g)
        idx_g, val_g = [], []
        for _k in range(TOP_K):
            m = jnp.max(biased_g, axis=-1, keepdims=True)
            i_k = jnp.min(jnp.where(biased_g == m, lane_g, float(LANES)), axis=-1, keepdims=True)
            hit = lane_g == i_k
            idx_g.append(i_k)
            val_g.append(jnp.sum(jnp.where(hit, scores_g, 0.0), axis=-1, keepdims=True))
            picked_g = jnp.where(hit, 1.0, picked_g)
            biased_g = jnp.where(hit, -jnp.inf, biased_g)
        return idx_g, val_g, picked_g

    lane_g = lax.broadcasted_iota(I32, (ROUTE_ROWS, LANES), 1).astype(F32)
    groups = [top_k(scores[r:r + ROUTE_ROWS], biased[r:r + ROUTE_ROWS], lane_g) for r in range(0, tm, ROUTE_ROWS)]
    sel_idx = [jnp.concatenate([g[0][k] for g in groups], axis=0) for k in range(TOP_K)]
    sel_val = [jnp.concatenate([g[1][k] for g in groups], axis=0) for k in range(TOP_K)]
    picked = jnp.concatenate([g[2] for g in groups], axis=0)
    total = sel_val[0]
    for v in sel_val[1:]:
        total = total + v
    cnt = jnp.sum(picked, axis=0, keepdims=True)
    ng = jnp.floor((cnt + float(GRAN - 1)) * (1.0 / GRAN))
    e_i = lax.broadcasted_iota(I32, (LANES, LANES), 0)
    e_j = lax.broadcasted_iota(I32, (LANES, LANES), 1)
    upper = jnp.where(e_i < e_j, 1.0, 0.0).astype(BF16)
    loffg = jnp.dot(jnp.broadcast_to(ng, (8, LANES)).astype(BF16), upper, preferred_element_type=F32)[0:1]
    r_i = lax.broadcasted_iota(I32, (tm, tm), 0)
    c_i = lax.broadcasted_iota(I32, (tm, tm), 1)
    before = jnp.dot(jnp.where(c_i < r_i, 1.0, 0.0).astype(BF16), picked.astype(BF16), preferred_element_type=F32)
    base = before + float(GRAN) * loffg
    slot_o = jnp.zeros((tm, LANES), F32)
    wts_o = jnp.zeros((tm, LANES), F32)
    for k in range(TOP_K):
        slot_k = jnp.sum(jnp.where(lane == sel_idx[k], base, 0.0), axis=-1, keepdims=True)
        slot_o = jnp.where(lane == float(k), slot_k, slot_o)
        wts_o = jnp.where(lane == float(k), ROUTED_SCALE * sel_val[k] / total, wts_o)
    slot_ref[...] = slot_o[:, :KPAD].astype(I32)
    wts_ref[...] = wts_o[:, :KPAD]
    slott_ref[0] = slot_o.T[:KPAD, :].astype(I32)

    run = run_ref[...]

    def as_col(v):
        return jnp.sum(jnp.where(e_i == e_j, jnp.broadcast_to(v, (LANES, LANES)), 0.0), axis=1, keepdims=True)

    end_c = as_col(loffg + ng)
    val_c = as_col(run - loffg)
    jj = lax.broadcasted_iota(I32, (LANES, NGRAN), 1).astype(F32)
    ee = lax.broadcasted_iota(I32, (LANES, NGRAN), 0).astype(F32)
    e_of_j = jnp.sum(jnp.where(end_c <= jj, 1.0, 0.0), axis=0, keepdims=True)
    dst = jnp.sum(jnp.where(ee == e_of_j, val_c + jj, 0.0), axis=0, keepdims=True)
    used = jnp.broadcast_to(jnp.sum(ng, axis=-1, keepdims=True), (1, NGRAN))
    gtab_ref[0] = jnp.concatenate([e_of_j, dst, used, jnp.zeros((5, NGRAN), F32)], axis=0).astype(I32)
    new_run = run + ng
    run_ref[...] = new_run
    tot_ref[...] = new_run


def _post_attn(ya, yb, w_out, x, x_tile_off, mods, mod_sel, g_moe, w_router, b_router):
    b, n, _ = ya.shape
    d = D_MODEL
    tm = TOK_TILE
    nt = n // tm
    t_tot = b * n
    wr = jnp.zeros((d, LANES), F32).at[:, :N_EXPERTS].set(w_router.astype(F32))
    wr_hi = wr.astype(BF16)
    wr_lo = (wr - wr_hi.astype(F32)).astype(BF16)
    wr = jnp.concatenate([wr_hi, wr_lo, wr_hi], axis=0)
    br = jnp.zeros((1, LANES), F32).at[0, :N_EXPERTS].set(b_router.astype(F32))
    flat = lambda c: pl.BlockSpec((tm, c), lambda i, t: (i * nt + t, 0))
    outs = pl.pallas_call(
        _post_attn_kernel,
        grid=(b, nt),
        in_specs=[pl.BlockSpec((1, tm, 512), lambda i, t: (i, t, 0)),
                  pl.BlockSpec((1, tm, 512), lambda i, t: (i, t, 0)),
                  pl.BlockSpec((d, d), lambda i, t: (0, 0)),
                  pl.BlockSpec((1, tm, d), lambda i, t: (i, t + x_tile_off, 0)),
                  pl.BlockSpec((1, 1, 6, d), lambda i, t: (i, mod_sel(t), 0, 0)),
                  pl.BlockSpec((1, d), lambda i, t: (0, 0)),
                  pl.BlockSpec((3 * d, LANES), lambda i, t: (0, 0)),
                  pl.BlockSpec((1, LANES), lambda i, t: (0, 0))],
        out_specs=[pl.BlockSpec((1, tm, d), lambda i, t: (i, t, 0)),
                   flat(d), flat(KPAD), flat(KPAD),
                   pl.BlockSpec((1, KPAD, tm), lambda i, t: (i * nt + t, 0, 0)),
                   pl.BlockSpec((1, 8, NGRAN), lambda i, t: (i * nt + t, 0, 0)),
                   pl.BlockSpec((1, LANES), lambda i, t: (0, 0))],
        out_shape=[jax.ShapeDtypeStruct((b, n, d), F32),
                   jax.ShapeDtypeStruct((t_tot, d), BF16),
                   jax.ShapeDtypeStruct((t_tot, KPAD), I32),
                   jax.ShapeDtypeStruct((t_tot, KPAD), F32),
                   jax.ShapeDtypeStruct((b * nt, KPAD, tm), I32),
                   jax.ShapeDtypeStruct((b * nt, 8, NGRAN), I32),
                   jax.ShapeDtypeStruct((1, LANES), F32)],
        scratch_shapes=[pltpu.VMEM((1, LANES), F32)],
        compiler_params=_cparams(("arbitrary", "arbitrary")),
        name="post_attn_route",
    )(ya, yb, w_out.astype(BF16), x, mods, g_moe.reshape(1, d), wr, br)
    return outs


def _granule_copy(src, src_g, dst, dst_g, sem):
    return pltpu.make_async_copy(src.at[pl.ds(pl.multiple_of(src_g * GRAN, GRAN), GRAN), :],
                                 dst.at[pl.ds(pl.multiple_of(dst_g * GRAN, GRAN), GRAN), :], sem)


def _for_each(n, body):
    trips = lax.shift_right_logical(n, jnp.int32(ISSUE_UNROLL.bit_length() - 1))

    def block(b, carry):
        for u in range(ISSUE_UNROLL):
            body(b * ISSUE_UNROLL + u)
        return carry

    def single(j, carry):
        body(j)
        return carry

    lax.fori_loop(0, trips, block, 0)
    lax.fori_loop(trips * ISSUE_UNROLL, n, single, 0)


def _drain(src, dst, sem, n):
    rows = WAIT_BATCH * GRAN
    batches = lax.shift_right_logical(n, jnp.int32(WAIT_BATCH.bit_length() - 1))

    def big(j, carry):
        pltpu.make_async_copy(src.at[pl.ds(0, rows), :], dst.at[pl.ds(0, rows), :], sem).wait()
        return carry

    def small(j, carry):
        _granule_copy(src, 0, dst, 0, sem).wait()
        return carry

    lax.fori_loop(0, batches, big, 0)
    lax.fori_loop(0, n - batches * WAIT_BATCH, small, 0)


def _dispatch_kernel(fill_ref, gtab_ref, h_ref, slott_ref, xs_ref, xloc, gprev, sems, *, nt):
    i = pl.program_id(0)
    cur = i % 2
    used = gtab_ref[0, 2, 0]
    slott = slott_ref[0]
    h = h_ref[...]
    tm = h.shape[0]
    half = h.shape[1] // 2
    def sort_chunk(c):
        rows = c * CHUNK + lax.broadcasted_iota(I32, (CHUNK, tm), 0)
        p = jnp.zeros((CHUNK, tm), F32)
        for k in range(TOP_K):
            p = jnp.where(rows == slott[k:k + 1, :], 1.0, p)
        xc = jnp.dot(p.astype(BF16), h, preferred_element_type=F32)
        lo = lax.bitcast_convert_type(xc[:, :half], U32)
        hi = lax.bitcast_convert_type(xc[:, half:], U32)
        xloc[cur, c * CHUNK:(c + 1) * CHUNK, :] = (hi & jnp.uint32(0xFFFF0000)) | (lo >> 16)

    for c in range(SORT_ROWS // CHUNK):
        if (c + 1) * CHUNK <= ALWAYS_ROWS:
            sort_chunk(c)
        else:
            pl.when(c * CHUNK < used * GRAN)(functools.partial(sort_chunk, c))

    @pl.when(i > 0)
    def _():
        _drain(xloc.at[1 - cur], xs_ref, sems.at[1 - cur], gprev[0])

    _for_each(used, lambda j: _granule_copy(xloc.at[cur], j, xs_ref, gtab_ref[0, 0, j], sems.at[cur]).start())
    gprev[0] = used

    @pl.when(i == nt - 1)
    def _():
        _drain(xloc.at[cur], xs_ref, sems.at[cur], used)
        zeros = xloc.at[1 - cur]
        zeros[0:SLOT_TILE, :] = jnp.zeros((SLOT_TILE, xloc.shape[2]), U32)

        def fill_expert(e, count):
            lo, hi = fill_ref[0, e], fill_ref[1, e]
            lax.fori_loop(lo, hi, lambda g, c: (_granule_copy(zeros, 0, xs_ref, g, sems.at[0]).start(), c)[1], 0)
            return count + (hi - lo)

        _drain(zeros, xs_ref, sems.at[0], lax.fori_loop(0, N_EXPERTS, fill_expert, jnp.int32(0)))

        def tile_copy(t):
            return pltpu.make_async_copy(zeros.at[pl.ds(0, SLOT_TILE), :],
                                         xs_ref.at[pl.ds(pl.multiple_of(t * SLOT_TILE, SLOT_TILE), SLOT_TILE), :],
                                         sems.at[1])

        first_free, n_tiles = fill_ref[2, 0], xs_ref.shape[0] // SLOT_TILE
        lax.fori_loop(first_free, n_tiles, lambda t, c: (tile_copy(t).start(), c)[1], 0)
        lax.fori_loop(first_free, n_tiles, lambda t, c: (tile_copy(0).wait(), c)[1], 0)


def _dispatch(h2, slott, gtab, fill, n_slots):
    t_tot, d = h2.shape
    tm = TOK_TILE
    nt = t_tot // tm
    return pl.pallas_call(
        functools.partial(_dispatch_kernel, nt=nt),
        grid=(nt,),
        in_specs=[pl.BlockSpec(memory_space=pltpu.SMEM),
                  pl.BlockSpec((1, 8, NGRAN), lambda i: (i, 0, 0), memory_space=pltpu.SMEM),
                  pl.BlockSpec((tm, d), lambda i: (i, 0)),
                  pl.BlockSpec((1, KPAD, tm), lambda i: (i, 0, 0))],
        out_specs=pl.BlockSpec(memory_space=pl.ANY),
        out_shape=jax.ShapeDtypeStruct((n_slots, d // 2), U32),
        scratch_shapes=[pltpu.VMEM((2, SORT_ROWS, d // 2), U32), pltpu.SMEM((1,), I32),
                        pltpu.SemaphoreType.DMA((2,))],
        compiler_params=_cparams(("arbitrary",)),
        name="moe_dispatch",
    )(fill, gtab, h2, slott)


def _expert_kernel(te_ref, nused_ref, xs_ref, wg_ref, wu_ref, wd_ref, ys_ref, wgu_s, wd_s):
    i = pl.program_id(0)
    changed = (i == 0) | (te_ref[i] != te_ref[jnp.maximum(i - 1, 0)])

    @pl.when(changed)
    def _():
        wgu_s[:, 0:EXPERT_FF] = wg_ref[0].astype(BF16)
        wgu_s[:, EXPERT_FF:2 * EXPERT_FF] = wu_ref[0].astype(BF16)
        wd_s[...] = wd_ref[0].astype(BF16)

    @pl.when(i < nused_ref[0])
    def _():
        lo, hi = _unpack_bf16_pair(xs_ref[...])
        x = jnp.concatenate([lo.astype(BF16), hi.astype(BF16)], axis=1)
        gu = jnp.dot(x, wgu_s[...], preferred_element_type=F32)
        gate, up = gu[:, :EXPERT_FF], gu[:, EXPERT_FF:]
        act = gate * jax.nn.sigmoid(gate) * up
        y = jnp.dot(act.astype(BF16), wd_s[...], preferred_element_type=F32)
        half = y.shape[1] // 2
        ys_ref[...] = _pack_bf16_pair(y[:, :half], y[:, half:])

    @pl.when(i >= nused_ref[0])
    def _():
        ys_ref[...] = jnp.zeros_like(ys_ref)


def _expert_ffn(xs, tile_expert, n_used, w_gate, w_up, w_down):
    n_slots, w = xs.shape
    ts = SLOT_TILE
    d, f = w_gate.shape[1], w_gate.shape[2]
    grid_spec = pltpu.PrefetchScalarGridSpec(
        num_scalar_prefetch=2,
        grid=(n_slots // ts,),
        in_specs=[pl.BlockSpec((ts, w), lambda i, te, nu: (i, 0)),
                  pl.BlockSpec((1, d, f), lambda i, te, nu: (te[i], 0, 0)),
                  pl.BlockSpec((1, d, f), lambda i, te, nu: (te[i], 0, 0)),
                  pl.BlockSpec((1, f, d), lambda i, te, nu: (te[i], 0, 0))],
        out_specs=pl.BlockSpec((ts, w), lambda i, te, nu: (i, 0)),
        scratch_shapes=[pltpu.VMEM((d, 2 * f), BF16), pltpu.VMEM((f, d), BF16)],
    )
    return pl.pallas_call(
        _expert_kernel,
        grid_spec=grid_spec,
        out_shape=jax.ShapeDtypeStruct((n_slots, w), U32),
        compiler_params=_cparams(("arbitrary",)),
        name="moe_experts",
    )(tile_expert, n_used, xs, w_gate, w_up, w_down)


def _combine_kernel(gtab_ref, gnext_ref, ys_ref, slot_ref, wts_ref, h_ref, x_ref, mod_ref, wsgu_ref,
                    wsd_ref, gf_ref, o_ref, yloc, acc_ref, sems, *, nt, final_norm):
    i = pl.program_id(0)
    cur = i % 2

    def fetch(tab_ref, buf):
        base_g = buf * NGRAN
        sem = sems.at[buf]

        _for_each(tab_ref[0, 2, 0], lambda j: _granule_copy(ys_ref, tab_ref[0, 0, j], yloc, base_g + j, sem).start())

    @pl.when(i == 0)
    def _():
        fetch(gtab_ref, 0)

    @pl.when(i + 1 < nt)
    def _():
        fetch(gnext_ref, 1 - cur)

    h = h_ref[...]
    tm = h.shape[0]
    gu = jnp.dot(h, wsgu_ref[...], preferred_element_type=F32)
    gate, up = gu[:, :SHARED_FF], gu[:, SHARED_FF:]
    shared = jnp.dot((gate * jax.nn.sigmoid(gate) * up).astype(BF16), wsd_ref[...], preferred_element_type=F32)

    used = gtab_ref[0, 2, 0]
    _drain(ys_ref, yloc, sems.at[cur], used)
    row0 = cur * SORT_ROWS

    slot_b = [jnp.broadcast_to(slot_ref[:, k:k + 1], (tm, LANES)) for k in range(TOP_K)]
    wts_b = [jnp.broadcast_to(wts_ref[:, k:k + 1], (tm, LANES)) for k in range(TOP_K)]
    lane = lax.broadcasted_iota(I32, (tm, LANES), 1)
    w = yloc.shape[1]

    def weights(c):
        blocks = []
        for j in range(CHUNK // LANES):
            cols = lane + (c * CHUNK + j * LANES)
            pw = jnp.zeros((tm, LANES), F32)
            for k in range(TOP_K):
                pw = jnp.where(cols == slot_b[k], wts_b[k], pw)
            blocks.append(pw.astype(BF16))
        return jnp.concatenate(blocks, axis=1)

    def values(c, masked):
        packed = yloc[pl.ds(pl.multiple_of(row0 + c * CHUNK, CHUNK), CHUNK), :]
        if masked:
            rows = c * CHUNK + lax.broadcasted_iota(I32, (CHUNK, w), 0)
            packed = jnp.where(rows < used * GRAN, packed, jnp.uint32(0))
        lo, hi = _unpack_bf16_pair(packed)
        return jnp.concatenate([lo.astype(BF16), hi.astype(BF16)], axis=1)

    n_always = ALWAYS_ROWS // CHUNK
    acc = shared
    for c in range(n_always):
        acc = acc + jnp.dot(weights(c), values(c, False), preferred_element_type=F32)
    acc_ref[...] = acc
    for c in range(n_always, SORT_ROWS // CHUNK):
        @pl.when(c * CHUNK < used * GRAN)
        def _():
            acc_ref[...] += jnp.dot(weights(c), values(c, True), preferred_element_type=F32)

    out = x_ref[...] + mod_ref[0, 0, 5:6, :] * acc_ref[...]
    if final_norm:
        out = _rms(out, gf_ref[...])
    o_ref[...] = out


def _combine(ys, slot, wts, gtab, h2, xmid, mods, mod_map, ws_gate, ws_up, ws_down, g_final, final_norm):
    t_tot, d = h2.shape
    tm = TOK_TILE
    nt = t_tot // tm
    wsgu = jnp.concatenate([ws_gate, ws_up], axis=1).astype(BF16)
    return pl.pallas_call(
        functools.partial(_combine_kernel, nt=nt, final_norm=final_norm),
        grid=(nt,),
        in_specs=[pl.BlockSpec((1, 8, NGRAN), lambda i: (i, 0, 0), memory_space=pltpu.SMEM),
                  pl.BlockSpec((1, 8, NGRAN), lambda i: (jnp.minimum(i + 1, nt - 1), 0, 0), memory_space=pltpu.SMEM),
                  pl.BlockSpec(memory_space=pl.ANY),
                  pl.BlockSpec((tm, KPAD), lambda i: (i, 0)),
                  pl.BlockSpec((tm, KPAD), lambda i: (i, 0)),
                  pl.BlockSpec((tm, d), lambda i: (i, 0)),
                  pl.BlockSpec((tm, d), lambda i: (i, 0)),
                  pl.BlockSpec((1, 1, 6, d), mod_map),
                  pl.BlockSpec((d, 2 * SHARED_FF), lambda i: (0, 0)),
                  pl.BlockSpec((SHARED_FF, d), lambda i: (0, 0)),
                  pl.BlockSpec((1, d), lambda i: (0, 0))],
        out_specs=pl.BlockSpec((tm, d), lambda i: (i, 0)),
        out_shape=jax.ShapeDtypeStruct((t_tot, d), F32),
        scratch_shapes=[pltpu.VMEM((2 * SORT_ROWS, d // 2), U32), pltpu.VMEM((tm, d), F32),
                        pltpu.SemaphoreType.DMA((2,))],
        compiler_params=_cparams(("arbitrary",)),
        name="moe_combine",
    )(gtab, gtab, ys, slot, wts, h2, xmid.reshape(t_tot, d), mods, wsgu, ws_down.astype(BF16),
      g_final.reshape(1, d).astype(F32))


def _moe(h2, slot, wts, slott, gtab, tot, xmid, mods, mod_map, moe_w, g_final, final_norm):
    (w_gate, w_up, w_down, ws_gate, ws_up, ws_down) = moe_w
    t_tot = h2.shape[0]
    ts = SLOT_TILE
    gpt = ts // GRAN
    nt = t_tot // TOK_TILE
    max_rows = t_tot * TOP_K + nt * N_EXPERTS * (GRAN - 1)
    n_tiles = -(-max_rows // ts) + N_EXPERTS
    totg = tot[0, :N_EXPERTS].astype(I32)
    tiles_e = (totg + gpt - 1) // gpt
    ends = jnp.cumsum(tiles_e)
    poffg = jnp.zeros((LANES,), I32).at[:N_EXPERTS].set((ends - tiles_e) * gpt)
    n_used = ends[-1:]
    tile_ids = jnp.minimum(jnp.arange(n_tiles, dtype=I32), n_used[0] - 1)
    tile_expert = jnp.sum((ends[None, :] <= tile_ids[:, None]).astype(I32), axis=1)
    tile_expert = jnp.minimum(tile_expert, N_EXPERTS - 1)
    region = jnp.sum(jnp.where(gtab[:, 0, :, None] == jnp.arange(N_EXPERTS, dtype=I32), poffg[:N_EXPERTS], 0), axis=-1)
    gtab = gtab.at[:, 0, :].set(region + gtab[:, 1, :])
    fill = jnp.zeros((8, LANES), I32)
    fill = fill.at[0, :N_EXPERTS].set(poffg[:N_EXPERTS] + totg).at[1, :N_EXPERTS].set(ends * gpt).at[2, 0].set(ends[-1])
    xs = _dispatch(h2, slott, gtab, fill, n_tiles * ts)
    ys = _expert_ffn(xs, tile_expert, n_used.astype(I32), w_gate, w_up, w_down)
    return _combine(ys, slot, wts, gtab, h2, xmid, mods, mod_map, ws_gate, ws_up, ws_down, g_final, final_norm)


def kernel(x, c, ctx, c_ctx, l0_w_ada, l0_b_ada, l0_g_mix, l0_w_in, l0_sink, l0_rpb, l0_w_out, l0_g_moe, l0_w_router, l0_b_router, l0_w_gate, l0_w_up, l0_w_down, l0_ws_gate, l0_ws_up, l0_ws_down, l1_w_ada, l1_b_ada, l1_g_mix, l1_w_in, l1_g_q_lora, l1_w_uq, l1_g_kv_lora, l1_w_ukv, l1_g_qn, l1_g_kn, l1_w_out, l1_g_moe, l1_w_router, l1_b_router, l1_w_gate, l1_w_up, l1_w_down, l1_ws_gate, l1_ws_up, l1_ws_down, g_final):
    b, seq, d = x.shape
    assert d == D_MODEL and ctx.shape[1] == CTX_LEN and seq % TOK_TILE == 0
    u = CTX_LEN + seq
    tiles_u = u // TOK_TILE
    tiles_s = seq // TOK_TILE
    xa = jnp.concatenate([ctx, x], axis=1).astype(F32)

    cos_a, sin_a = _rope_tables(seq, HEAD_DIM, 0, HEAD_DIM, 1.0)
    cos_q, sin_q = _rope_tables(seq, C_ROPE, C_NOPE, LANES, float((C_NOPE + C_ROPE) ** -0.5))
    cos_k, sin_k = _rope_tables(seq, C_ROPE, 0, LANES, 1.0)

    mods0 = _mods(c, c_ctx, l0_w_ada, l0_b_ada)
    qa, kva, qb, kb, vb = _proj0(xa, mods0, l0_g_mix, l0_w_in, cos_a, sin_a)
    ya = _window_attention(qa, kva, l0_sink)
    yb = _na_attention(qb, kb, vb, l0_rpb)
    xmid, h2, slot, wts, slott, gtab, tot = _post_attn(ya, yb, l0_w_out, xa, 0, mods0, lambda t: jnp.minimum(t, 1),
                                                   l0_g_moe, l0_w_router, l0_b_router)
    xa = _moe(h2, slot, wts, slott, gtab, tot, xmid, mods0,
              lambda i: (i // tiles_u, jnp.minimum(i % tiles_u, 1), 0, 0),
              (l0_w_gate, l0_w_up, l0_w_down, l0_ws_gate, l0_ws_up, l0_ws_down), g_final, False).reshape(b, u, d)

    mods1 = _mods(c, c_ctx, l1_w_ada, l1_b_ada)
    qm, qd, km, vm, kvd = _proj1(xa, mods1, l1_g_mix, l1_w_in, l1_g_q_lora, l1_w_uq, l1_g_kv_lora, l1_w_ukv,
                                 l1_g_qn, l1_g_kn, (cos_a, sin_a, cos_q, sin_q, cos_k, sin_k))
    ym = _mla_attention(qm, km, vm)
    yd = _gqa_attention(qd, kvd)
    xmid, h2, slot, wts, slott, gtab, tot = _post_attn(ym, yd, l1_w_out, xa, CTX_LEN // TOK_TILE, mods1, lambda t: 1,
                                                   l1_g_moe, l1_w_router, l1_b_router)
    out = _moe(h2, slot, wts, slott, gtab, tot, xmid, mods1, lambda i: (i // tiles_s, 1, 0, 0),
               (l1_w_gate, l1_w_up, l1_w_down, l1_ws_gate, l1_ws_up, l1_ws_down), g_final, True)
    return out.reshape(b, seq, d)
```

```python
import functools

import numpy as np
import jax
import jax.numpy as jnp
from jax import lax
from jax.experimental import pallas as pl
from jax.experimental.pallas import tpu as pltpu

F32 = jnp.float32
BF16 = jnp.bfloat16
U32 = jnp.uint32
I32 = jnp.int32

D_MODEL = 1024
CTX_LEN = 256
GRID_W = 64
HEAD_DIM = 64
ROPE_THETA = 10000.0
NORM_EPS = 1e-6
NEG_INF = -1e30
A_WINDOW = 128
NA_ROWS = 8
NA_COLS = 16
NA_QROWS = 4
NA_BAND = NA_ROWS + NA_QROWS - 1
C_Q_LORA = 384
C_KV_LORA = 256
C_NOPE = 64
C_ROPE = 32
C_V = 64
N_EXPERTS = 64
TOP_K = 6
EXPERT_FF = 256
SHARED_FF = 256
ROUTED_SCALE = 2.5

LANES = 128
TOK_TILE = 256
SLOT_TILE = 1024
KPAD = 8
GRAN = 8
CHUNK = 256
SORT_ROWS = 2048
NGRAN = SORT_ROWS // GRAN
ALWAYS_ROWS = TOK_TILE * TOP_K
WAIT_BATCH = 16
ISSUE_UNROLL = 4
ROUTE_ROWS = 256
VMEM_LIMIT = 48 * 1024 * 1024


def _cparams(sem):
    return pltpu.CompilerParams(dimension_semantics=sem, vmem_limit_bytes=VMEM_LIMIT)


def _rms(x, g):
    return x * lax.rsqrt(jnp.mean(x * x, axis=-1, keepdims=True) + NORM_EPS) * g


def _norm_mod(x, g, shift, scale):
    return _rms(x, g) * (1.0 + scale) + shift


def _rope128(x, cos, sin):
    lane = lax.broadcasted_iota(I32, x.shape, 1)
    swapped = jnp.where(lane % 2 == 0, pltpu.roll(x, LANES - 1, 1), pltpu.roll(x, 1, 1))
    return x * cos + swapped * sin


def _group_sumsq(x, ones_bd):
    sq = x * x
    hi = sq.astype(BF16)
    lo = (sq - hi.astype(F32)).astype(BF16)
    return (jnp.dot(hi, ones_bd, preferred_element_type=F32) + jnp.dot(lo, ones_bd, preferred_element_type=F32))


def _head_norm(x, ones_bd, g):
    return x * lax.rsqrt(_group_sumsq(x, ones_bd) * (1.0 / HEAD_DIM) + NORM_EPS) * g


def _pack_bf16_pair(lo, hi):
    lo_bits = lax.bitcast_convert_type(lo.astype(BF16).astype(F32), U32)
    hi_bits = lax.bitcast_convert_type(hi.astype(BF16).astype(F32), U32)
    return (hi_bits & jnp.uint32(0xFFFF0000)) | (lo_bits >> 16)


def _unpack_bf16_pair(u):
    lo = lax.bitcast_convert_type(u << 16, F32)
    hi = lax.bitcast_convert_type(u & jnp.uint32(0xFFFF0000), F32)
    return lo, hi


def _keep_half(q128, half):
    lane = lax.broadcasted_iota(I32, q128.shape, 1)
    keep = (lane < HEAD_DIM) if half == 0 else (lane >= HEAD_DIM)
    return jnp.where(keep, q128, jnp.zeros_like(q128))


def _merge_halves(o_even, o_odd):
    lane = lax.broadcasted_iota(I32, o_even.shape, 1)
    return jnp.where(lane < HEAD_DIM, o_even, o_odd)


def _qk(q, k):
    return lax.dot_general(q, k, (((1,), (1,)), ((), ())), preferred_element_type=F32)


def _softmax_pv(s, v, sinks=None):
    m = jnp.max(s, axis=-1, keepdims=True)
    if sinks is not None:
        r = s.shape[0] // len(sinks)
        m = jnp.concatenate([jnp.maximum(m[i * r:(i + 1) * r], sinks[i]) for i in range(len(sinks))], axis=0)
    e = jnp.exp(s - m)
    den = jnp.sum(e, axis=-1, keepdims=True)
    if sinks is not None:
        den = den + jnp.concatenate([jnp.exp(sinks[i] - m[i * r:(i + 1) * r]) for i in range(len(sinks))], axis=0)
    return jnp.dot(e.astype(BF16), v, preferred_element_type=F32) / den


def _ada_kernel(c_ref, w_ref, b_ref, o_ref):
    c = c_ref[...]
    a = c * jax.nn.sigmoid(c)
    o_ref[...] = jnp.dot(a, w_ref[...], precision=lax.Precision.HIGHEST, preferred_element_type=F32) + b_ref[...]


def _ada(cond, w_ada, b_ada):
    n, d = cond.shape
    nout = w_ada.shape[1]
    bn = 512
    return pl.pallas_call(
        _ada_kernel,
        grid=(nout // bn,),
        in_specs=[pl.BlockSpec((n, d), lambda j: (0, 0)),
                  pl.BlockSpec((d, bn), lambda j: (0, j)),
                  pl.BlockSpec((1, bn), lambda j: (0, j))],
        out_specs=pl.BlockSpec((n, bn), lambda j: (0, j)),
        out_shape=jax.ShapeDtypeStruct((n, nout), F32),
        compiler_params=_cparams(("arbitrary",)),
        name="ada",
    )(cond, w_ada, b_ada.reshape(1, nout))


def _mods(c, c_ctx, w_ada, b_ada):
    b = c.shape[0]
    rows = ((b + 1 + 7) // 8) * 8
    cond = jnp.zeros((rows, D_MODEL), F32).at[:b].set(c).at[b].set(c_ctx)
    out = _ada(cond, w_ada, b_ada)
    lat = out[:b].reshape(b, 1, 6, D_MODEL)
    cx = jnp.broadcast_to(out[b].reshape(1, 1, 6, D_MODEL), (b, 1, 6, D_MODEL))
    return jnp.concatenate([cx, lat], axis=1)


def _axial_angles(n_tokens, rot_dim):
    t = np.arange(n_tokens)
    row = (t // GRID_W).astype(np.float32)
    col = (t % GRID_W).astype(np.float32)
    n_axis = rot_dim // 4
    inv_freq = (np.float32(ROPE_THETA) ** (-np.arange(n_axis, dtype=np.float32) / n_axis)).astype(np.float32)
    return jnp.concatenate([jnp.asarray(row[:, None] * inv_freq), jnp.asarray(col[:, None] * inv_freq)], axis=-1)


def _rope_tables(seq, rot_dim, lane_start, period, scale):
    ang = _axial_angles(seq, rot_dim)
    cos = jnp.repeat(jnp.cos(ang), 2, axis=-1)
    sin = jnp.repeat(jnp.sin(ang), 2, axis=-1) * jnp.tile(jnp.asarray([-1.0, 1.0], F32), rot_dim // 2)
    cos_p = jnp.ones((seq, period), F32).at[:, lane_start:lane_start + rot_dim].set(cos)
    sin_p = jnp.zeros((seq, period), F32).at[:, lane_start:lane_start + rot_dim].set(sin)
    cos_f = jnp.concatenate([jnp.ones((CTX_LEN, period), F32), cos_p], axis=0)
    sin_f = jnp.concatenate([jnp.zeros((CTX_LEN, period), F32), sin_p], axis=0)
    reps = LANES // period
    return jnp.tile(cos_f, (1, reps)) * scale, jnp.tile(sin_f, (1, reps)) * scale


def _na_band_start(block, rows):
    return np.clip(block * NA_QROWS - NA_ROWS // 2, 0, rows - NA_BAND)


def _na_bias_table(rpb, rows):
    h = rpb.shape[0]
    qc = np.arange(GRID_W)
    kc = np.arange(GRID_W)
    cstart = np.clip(qc - NA_COLS // 2, 0, GRID_W - NA_COLS)
    col_ok = (kc[None, :] >= cstart[:, None]) & (kc[None, :] < cstart[:, None] + NA_COLS)
    dc = np.clip(kc[None, :] - qc[:, None] + NA_COLS - 1, 0, 2 * NA_COLS - 2)
    col_sel = np.eye(2 * NA_COLS - 1, dtype=np.float32)[dc]
    variants, ids = [], [None]
    for blk in range(rows // NA_QROWS):
        rs0 = _na_band_start(blk, rows)
        r = blk * NA_QROWS + np.arange(NA_QROWS)
        rs = np.clip(r - NA_ROWS // 2, 0, rows - NA_ROWS)
        krow = rs0 + np.arange(NA_BAND)
        row_ok = (krow[None, :] >= rs[:, None]) & (krow[None, :] < rs[:, None] + NA_ROWS)
        dr = np.clip(krow[None, :] - r[:, None] + NA_ROWS - 1, 0, 2 * NA_ROWS - 2)
        key = (row_ok.tobytes(), dr.tobytes())
        if key not in [v[0] for v in variants]:
            variants.append((key, row_ok, dr))
        ids.append([v[0] for v in variants].index(key))
    ids[0] = len(variants)
    tabs = []
    for _, row_ok, dr in variants:
        row_sel = np.eye(2 * NA_ROWS - 1, dtype=np.float32)[dr]
        b = jnp.einsum('hrc,ijr,qkc->hiqjk', rpb.astype(F32), jnp.asarray(row_sel), jnp.asarray(col_sel),
                       precision=lax.Precision.HIGHEST)
        ok = row_ok[:, None, :, None] & col_ok[None, :, None, :]
        b = jnp.where(jnp.asarray(ok)[None], b, NEG_INF)
        tabs.append(b.reshape(h, NA_QROWS * GRID_W, NA_BAND * GRID_W))
    tabs.append(jnp.full((h, NA_QROWS * GRID_W, NA_BAND * GRID_W), NEG_INF, F32))
    loc = jnp.stack(tabs, axis=0)
    tab = jnp.concatenate([jnp.zeros(loc.shape[:3] + (CTX_LEN,), F32), loc], axis=-1)
    return tab, jnp.asarray(ids, I32)


def _dup_heads(w, n_heads):
    d = w.shape[0]
    w = w.reshape(d, n_heads, 1, HEAD_DIM)
    return jnp.broadcast_to(w, (d, n_heads, 2, HEAD_DIM)).reshape(d, n_heads * 2 * HEAD_DIM)


def _proj0_kernel(c_ref, x_ref, mod_ref, g_ref, w_ref, cos_ref, sin_ref, qa_ref, kva_ref, qb_ref, kb_ref, vb_ref):
    xt = jnp.where(pl.program_id(1) == 0, c_ref[0], x_ref[0])
    h = _norm_mod(xt, g_ref[...], mod_ref[0, 0, 0:1, :], mod_ref[0, 0, 1:2, :])
    r = jnp.dot(h.astype(BF16), w_ref[...], preferred_element_type=F32)
    cos, sin = cos_ref[...], sin_ref[...]
    roped = [_rope128(r[:, i * LANES:(i + 1) * LANES], cos, sin) for i in range(6)]
    qa_ref[0] = jnp.concatenate(roped[0:4], axis=1).astype(BF16)
    kva_ref[0] = jnp.concatenate(roped[4:6] + [r[:, 768:1024]], axis=1).astype(BF16)
    qb_ref[0] = r[:, 1024:1536].astype(BF16)
    kb_ref[0] = r[:, 1536:2048].astype(BF16)
    vb_ref[0] = r[:, 2048:2560].astype(BF16)


def _proj0(ctx, x, mods, g_mix, w_in, cos, sin):
    b, seq, d = x.shape
    u = CTX_LEN + seq
    tm = TOK_TILE
    s = 1.0 / 8.0
    w = jnp.concatenate([w_in[:, 0:512] * s, _dup_heads(w_in[:, 512:640], 2), _dup_heads(w_in[:, 640:768], 2),
                         w_in[:, 768:1280] * s, w_in[:, 1280:1792], w_in[:, 1792:2304]], axis=1).astype(BF16)
    nw = w.shape[1]
    tok = lambda n: pl.BlockSpec((1, tm, n), lambda i, t: (i, t, 0))
    return pl.pallas_call(
        _proj0_kernel,
        grid=(b, u // tm),
        in_specs=[pl.BlockSpec((1, tm, d), lambda i, t: (i, 0, 0)),
                  pl.BlockSpec((1, tm, d), lambda i, t: (i, jnp.maximum(t - 1, 0), 0)),
                  pl.BlockSpec((1, 1, 6, d), lambda i, t: (i, jnp.minimum(t, 1), 0, 0)),
                  pl.BlockSpec((1, d), lambda i, t: (0, 0)),
                  pl.BlockSpec((d, nw), lambda i, t: (0, 0)),
                  pl.BlockSpec((tm, LANES), lambda i, t: (t, 0)),
                  pl.BlockSpec((tm, LANES), lambda i, t: (t, 0))],
        out_specs=[tok(512)] * 5,
        out_shape=[jax.ShapeDtypeStruct((b, u, 512), BF16)] * 5,
        compiler_params=_cparams(("arbitrary", "arbitrary")),
        name="proj0",
    )(ctx, x, mods, g_mix.reshape(1, d), w, cos, sin)


def _window_starts(n_tiles, tq, span, u):
    return np.clip((np.arange(n_tiles) - 1) * tq, CTX_LEN, u - span)


def _window_mask_table(u, tq, span):
    n_tiles = u // tq
    starts = _window_starts(n_tiles, tq, span, u)
    variants, ids = [], []
    for n in range(n_tiles):
        qpos = n * tq + np.arange(tq)[:, None]
        kpos = starts[n] + np.arange(span)[None, :]
        ok = (kpos >= CTX_LEN) & (np.abs(qpos - kpos) <= A_WINDOW) & (qpos >= CTX_LEN)
        key = ok.tobytes()
        if key not in [v[0] for v in variants]:
            variants.append((key, ok))
        ids.append([v[0] for v in variants].index(key))
    local = np.stack([np.where(v[1], 0.0, NEG_INF) for v in variants]).astype(np.float32)
    tab = np.concatenate([np.zeros(local.shape[:2] + (CTX_LEN,), np.float32), local], axis=-1)
    return jnp.asarray(tab), jnp.asarray(ids, I32)


def _window_kernel(var_ref, sink_ref, q_ref, kv_ref, mask_ref, o_ref, *, tq, span):
    n = pl.program_id(1)
    u = kv_ref.shape[1]
    ls = pl.multiple_of(jnp.clip((n - 1) * tq, CTX_LEN, u - span), tq)
    mask = mask_ref[var_ref[n]]
    kv_all = jnp.concatenate([kv_ref[0, 0:CTX_LEN, :], kv_ref[0, pl.ds(ls, span), :]], axis=0)
    outs = [None] * 8
    for g in range(2):
        q4 = jnp.concatenate([_keep_half(q_ref[0, :, ((4 * g + hh) // 2) * LANES:((4 * g + hh) // 2 + 1) * LANES],
                                         hh % 2) for hh in range(4)], axis=0)
        s = _qk(q4, kv_all[:, g * LANES:(g + 1) * LANES])
        s = jnp.concatenate([s[hh * tq:(hh + 1) * tq] + mask for hh in range(4)], axis=0)
        o = _softmax_pv(s, kv_all[:, 256 + g * LANES:256 + (g + 1) * LANES],
                        [sink_ref[4 * g + hh] for hh in range(4)])
        for hh in range(4):
            outs[4 * g + hh] = o[hh * tq:(hh + 1) * tq]
    o_ref[0] = jnp.concatenate([_merge_halves(outs[2 * p], outs[2 * p + 1]) for p in range(4)], axis=1).astype(BF16)


def _window_attention(qa, kva, sink):
    b, u, _ = qa.shape
    tq = A_WINDOW
    span = 3 * A_WINDOW
    mask_tab, variant = _window_mask_table(u, tq, span)
    return pl.pallas_call(
        functools.partial(_window_kernel, tq=tq, span=span),
        grid=(b, u // tq),
        in_specs=[pl.BlockSpec(memory_space=pltpu.SMEM),
                  pl.BlockSpec(memory_space=pltpu.SMEM),
                  pl.BlockSpec((1, tq, 512), lambda i, n: (i, n, 0)),
                  pl.BlockSpec((1, u, 512), lambda i, n: (i, 0, 0)),
                  pl.BlockSpec(mask_tab.shape, lambda i, n: (0, 0, 0))],
        out_specs=pl.BlockSpec((1, tq, 512), lambda i, n: (i, n, 0)),
        out_shape=jax.ShapeDtypeStruct((b, u, 512), BF16),
        compiler_params=_cparams(("arbitrary", "arbitrary")),
        name="window_attn",
    )(variant, sink.astype(F32), qa, kva, mask_tab)


def _na_kernel(var_ref, q_ref, k_ref, v_ref, bias_ref, o_ref, *, rows):
    j = pl.program_id(1)
    rs0 = jnp.clip((j - 1) * NA_QROWS - NA_ROWS // 2, 0, rows - NA_BAND)
    band = NA_BAND * GRID_W
    start = pl.multiple_of(CTX_LEN + rs0 * GRID_W, GRID_W)
    k_all = jnp.concatenate([k_ref[0, 0:CTX_LEN, :], k_ref[0, pl.ds(start, band), :]], axis=0)
    v_all = jnp.concatenate([v_ref[0, 0:CTX_LEN, :], v_ref[0, pl.ds(start, band), :]], axis=0)
    tq = q_ref.shape[1]
    outs = []
    for p in range(4):
        q128 = q_ref[0, :, p * LANES:(p + 1) * LANES]
        q2 = jnp.concatenate([_keep_half(q128, 0), _keep_half(q128, 1)], axis=0)
        s = _qk(q2, k_all[:, p * LANES:(p + 1) * LANES])
        s = s + jnp.concatenate([bias_ref[0, 2 * p], bias_ref[0, 2 * p + 1]], axis=0)
        o = _softmax_pv(s, v_all[:, p * LANES:(p + 1) * LANES])
        outs.append(_merge_halves(o[:tq], o[tq:]))
    o_ref[0] = jnp.concatenate(outs, axis=1).astype(BF16)


def _na_attention(qb, kb, vb, rpb):
    b, u, _ = qb.shape
    rows = (u - CTX_LEN) // GRID_W
    assert rows >= NA_BAND and rows % NA_QROWS == 0
    bias_tab, variant = _na_bias_table(rpb, rows)
    tq = NA_QROWS * GRID_W
    assert tq == CTX_LEN
    nk = bias_tab.shape[-1]
    grid_spec = pltpu.PrefetchScalarGridSpec(
        num_scalar_prefetch=1,
        grid=(b, u // tq),
        in_specs=[pl.BlockSpec((1, tq, 512), lambda i, j, var: (i, j, 0)),
                  pl.BlockSpec((1, u, 512), lambda i, j, var: (i, 0, 0)),
                  pl.BlockSpec((1, u, 512), lambda i, j, var: (i, 0, 0)),
                  pl.BlockSpec((1, 8, tq, nk), lambda i, j, var: (var[j], 0, 0, 0))],
        out_specs=pl.BlockSpec((1, tq, 512), lambda i, j, var: (i, j, 0)),
    )
    return pl.pallas_call(
        functools.partial(_na_kernel, rows=rows),
        grid_spec=grid_spec,
        out_shape=jax.ShapeDtypeStruct((b, u, 512), BF16),
        compiler_params=_cparams(("arbitrary", "arbitrary")),
        name="na_attn",
    )(variant, qb, kb, vb, bias_tab)


def _proj1_kernel(x_ref, mod_ref, g_ref, w_ref, gq_ref, wuq_ref, gkv_ref, wkv_ref, gqn_ref, gkn_ref, ones_ref,
                  cosa_ref, sina_ref, cosq_ref, sinq_ref, cosk_ref, sink_ref,
                  qm_ref, qd_ref, km_ref, vm_ref, kvd_ref):
    h = _norm_mod(x_ref[0], g_ref[...], mod_ref[0, 0, 0:1, :], mod_ref[0, 0, 1:2, :])
    r = jnp.dot(h.astype(BF16), w_ref[...], preferred_element_type=F32)
    ones_bd = ones_ref[...]
    cosa, sina = cosa_ref[...], sina_ref[...]
    cq = _rms(r[:, 0:C_Q_LORA], gq_ref[...])
    qm = jnp.dot(cq.astype(BF16), wuq_ref[...], preferred_element_type=F32)
    cosq, sinq = cosq_ref[...], sinq_ref[...]
    qm_ref[0] = jnp.concatenate([_rope128(qm[:, i * LANES:(i + 1) * LANES], cosq, sinq) for i in range(8)],
                                axis=1).astype(BF16)
    gqn = gqn_ref[...]
    qd_ref[0] = jnp.concatenate(
        [_rope128(_head_norm(r[:, 384 + i * LANES:384 + (i + 1) * LANES], ones_bd, gqn), cosa, sina)
         for i in range(4)], axis=1).astype(BF16)
    ckv = _rms(r[:, 896:1152], gkv_ref[...])
    kr = _rope128(r[:, 1152:1280], cosk_ref[...], sink_ref[...])
    kv = jnp.dot(jnp.concatenate([ckv, kr], axis=1).astype(BF16), wkv_ref[...], preferred_element_type=F32)
    km_ref[0] = kv[:, 0:1024].astype(BF16)
    vm_ref[0] = kv[:, 1024:1536].astype(BF16)
    gkn = gkn_ref[...]
    kd = [_rope128(_head_norm(r[:, 1280 + i * LANES:1280 + (i + 1) * LANES], ones_bd, gkn), cosa, sina)
          for i in range(2)]
    kvd_ref[0] = jnp.concatenate(kd + [r[:, 1536:1792]], axis=1).astype(BF16)


def _proj1(xa, mods, g_mix, w_in, g_q_lora, w_uq, g_kv_lora, w_ukv, g_qn, g_kn, tabs):
    b, u, d = xa.shape
    tm = TOK_TILE
    zpad = jnp.zeros((d, LANES - C_ROPE), F32)
    w = jnp.concatenate([w_in[:, 0:896], w_in[:, 896:1152], w_in[:, 1152:1184], zpad,
                         _dup_heads(w_in[:, 1184:1312], 2), _dup_heads(w_in[:, 1312:1440], 2)], axis=1).astype(BF16)
    nw = w.shape[1]
    wuq = w_uq.reshape(C_Q_LORA, 8, C_NOPE + C_ROPE)
    wuq = jnp.concatenate([wuq, jnp.zeros((C_Q_LORA, 8, LANES - C_NOPE - C_ROPE), F32)], axis=-1)
    wuq = wuq.reshape(C_Q_LORA, 8 * LANES).astype(BF16)
    wukv = w_ukv.reshape(C_KV_LORA, 8, C_NOPE + C_V)
    wk = jnp.concatenate([wukv[:, :, :C_NOPE], jnp.zeros((C_KV_LORA, 8, LANES - C_NOPE), F32)], axis=-1)
    wk = wk.reshape(C_KV_LORA, 8 * LANES)
    wv = wukv[:, :, C_NOPE:].reshape(C_KV_LORA, 8 * C_V)
    place = np.zeros((LANES, 8, LANES), np.float32)
    for j in range(C_ROPE):
        place[j, :, C_NOPE + j] = 1.0
    place = jnp.asarray(place.reshape(LANES, 8 * LANES))
    wkv = jnp.concatenate([jnp.concatenate([wk, wv], axis=1),
                           jnp.concatenate([place, jnp.zeros((LANES, 8 * C_V), F32)], axis=1)], axis=0).astype(BF16)
    ones_bd = jnp.asarray(np.kron(np.eye(2, dtype=np.float32), np.ones((HEAD_DIM, HEAD_DIM), np.float32))).astype(BF16)
    gqn = jnp.tile(g_qn.astype(F32) * (1.0 / 8.0), 2).reshape(1, LANES)
    gkn = jnp.tile(g_kn.astype(F32), 2).reshape(1, LANES)
    cosa, sina, cosq, sinq, cosk, sink = tabs
    tok = lambda n: pl.BlockSpec((1, tm, n), lambda i, t: (i, t, 0))
    whole = lambda a: pl.BlockSpec(a.shape, lambda i, t: (0,) * a.ndim)
    tab = pl.BlockSpec((tm, LANES), lambda i, t: (t, 0))
    small = [g_mix.reshape(1, d), w, g_q_lora.reshape(1, -1).astype(F32), wuq, g_kv_lora.reshape(1, -1).astype(F32),
             wkv, gqn, gkn, ones_bd]
    return pl.pallas_call(
        _proj1_kernel,
        grid=(b, u // tm),
        in_specs=[tok(d), pl.BlockSpec((1, 1, 6, d), lambda i, t: (i, jnp.minimum(t, 1), 0, 0))]
                 + [whole(a) for a in small] + [tab] * 6,
        out_specs=[tok(1024), tok(512), tok(1024), tok(512), tok(512)],
        out_shape=[jax.ShapeDtypeStruct((b, u, n), BF16) for n in (1024, 512, 1024, 512, 512)],
        compiler_params=_cparams(("arbitrary", "arbitrary")),
        name="proj1",
    )(xa, mods, *small, cosa, sina, cosq, sinq, cosk, sink)


def _mla_kernel(q_ref, k_ref, v_ref, o_ref):
    outs = []
    for h in range(8):
        s = _qk(q_ref[0, :, h * LANES:(h + 1) * LANES], k_ref[0, :, h * LANES:(h + 1) * LANES])
        outs.append(_softmax_pv(s, v_ref[0, :, (h // 2) * LANES:(h // 2 + 1) * LANES]))
    o_ref[0] = jnp.concatenate([_merge_halves(outs[2 * p], outs[2 * p + 1]) for p in range(4)], axis=1).astype(BF16)


def _mla_attention(qm, km, vm):
    b, u, _ = qm.shape
    tq = 256
    nq = (u - CTX_LEN) // tq
    off = CTX_LEN // tq
    return pl.pallas_call(
        _mla_kernel,
        grid=(b, nq),
        in_specs=[pl.BlockSpec((1, tq, 1024), lambda i, n: (i, n + off, 0)),
                  pl.BlockSpec((1, u, 1024), lambda i, n: (i, 0, 0)),
                  pl.BlockSpec((1, u, 512), lambda i, n: (i, 0, 0))],
        out_specs=pl.BlockSpec((1, tq, 512), lambda i, n: (i, n, 0)),
        out_shape=jax.ShapeDtypeStruct((b, u - CTX_LEN, 512), BF16),
        compiler_params=_cparams(("arbitrary", "arbitrary")),
        name="mla_attn",
    )(qm, km, vm)


def _gqa_kernel(q_ref, kv_ref, o_ref):
    outs = [None] * 8
    for g in range(2):
        k = kv_ref[0, :, g * LANES:(g + 1) * LANES]
        v = kv_ref[0, :, 256 + g * LANES:256 + (g + 1) * LANES]
        for hh in range(4):
            h = 4 * g + hh
            q = _keep_half(q_ref[0, :, (h // 2) * LANES:(h // 2 + 1) * LANES], h % 2)
            outs[h] = _softmax_pv(_qk(q, k), v)
    o_ref[0] = jnp.concatenate([_merge_halves(outs[2 * p], outs[2 * p + 1]) for p in range(4)], axis=1).astype(BF16)


def _gqa_attention(qd, kvd):
    b, u, _ = qd.shape
    tq = 256
    nq = (u - CTX_LEN) // tq
    off = CTX_LEN // tq
    return pl.pallas_call(
        _gqa_kernel,
        grid=(b, nq),
        in_specs=[pl.BlockSpec((1, tq, 512), lambda i, n: (i, n + off, 0)),
                  pl.BlockSpec((1, u, 512), lambda i, n: (i, 0, 0))],
        out_specs=pl.BlockSpec((1, tq, 512), lambda i, n: (i, n, 0)),
        out_shape=jax.ShapeDtypeStruct((b, u - CTX_LEN, 512), BF16),
        compiler_params=_cparams(("arbitrary", "arbitrary")),
        name="gqa_attn",
    )(qd, kvd)


def _post_attn_kernel(ya_ref, yb_ref, w_ref, c_ref, x_ref, mod_ref, g_ref, wr_ref, br_ref,
                      xmid_ref, h2_ref, slot_ref, wts_ref, slott_ref, gtab_ref, tot_ref, run_ref, *, ctx_first):
    first = (pl.program_id(0) == 0) & (pl.program_id(1) == 0)
    x_in = jnp.where(pl.program_id(1) == 0, c_ref[0], x_ref[0]) if ctx_first else x_ref[0]

    @pl.when(first)
    def _():
        run_ref[...] = jnp.zeros_like(run_ref)

    y = (jnp.dot(ya_ref[0], w_ref[0:512, :], preferred_element_type=F32)
         + jnp.dot(yb_ref[0], w_ref[512:1024, :], preferred_element_type=F32))
    x1 = x_in + mod_ref[0, 0, 2:3, :] * y
    xmid_ref[0] = x1
    h2 = _norm_mod(x1, g_ref[...], mod_ref[0, 0, 3:4, :], mod_ref[0, 0, 4:5, :])
    h_hi = h2.astype(BF16)
    h2_ref[...] = h_hi

    h_lo = (h2 - h_hi.astype(F32)).astype(BF16)
    logits = jnp.dot(jnp.concatenate([h_hi, h_hi, h_lo], axis=1), wr_ref[...], preferred_element_type=F32)
    scores = jax.nn.sigmoid(logits)
    tm = scores.shape[0]
    lane = lax.broadcasted_iota(I32, (tm, LANES), 1).astype(F32)
    biased = jnp.where(lane < N_EXPERTS, scores + br_ref[...], -jnp.inf)
    def top_k(scores_g, biased_g, lane_g):
        picked_g = jnp.zeros_like(scores_g)
        idx_g, val_g = [], []
        for _k in range(TOP_K):
            m = jnp.max(biased_g, axis=-1, keepdims=True)
            i_k = jnp.min(jnp.where(biased_g == m, lane_g, float(LANES)), axis=-1, keepdims=True)
            hit = lane_g == i_k
            idx_g.append(i_k)
            val_g.append(jnp.sum(jnp.where(hit, scores_g, 0.0), axis=-1, keepdims=True))
            picked_g = jnp.where(hit, 1.0, picked_g)
            biased_g = jnp.where(hit, -jnp.inf, biased_g)
        return idx_g, val_g, picked_g

    lane_g = lax.broadcasted_iota(I32, (ROUTE_ROWS, LANES), 1).astype(F32)
    groups = [top_k(scores[r:r + ROUTE_ROWS], biased[r:r + ROUTE_ROWS], lane_g) for r in range(0, tm, ROUTE_ROWS)]
    sel_idx = [jnp.concatenate([g[0][k] for g in groups], axis=0) for k in range(TOP_K)]
    sel_val = [jnp.concatenate([g[1][k] for g in groups], axis=0) for k in range(TOP_K)]
    picked = jnp.concatenate([g[2] for g in groups], axis=0)
    total = sel_val[0]
    for v in sel_val[1:]:
        total = total + v
    cnt = jnp.sum(picked, axis=0, keepdims=True)
    ng = jnp.floor((cnt + float(GRAN - 1)) * (1.0 / GRAN))
    e_i = lax.broadcasted_iota(I32, (LANES, LANES), 0)
    e_j = lax.broadcasted_iota(I32, (LANES, LANES), 1)
    upper = jnp.where(e_i < e_j, 1.0, 0.0).astype(BF16)
    loffg = jnp.dot(jnp.broadcast_to(ng, (8, LANES)).astype(BF16), upper, preferred_element_type=F32)[0:1]
    r_i = lax.broadcasted_iota(I32, (tm, tm), 0)
    c_i = lax.broadcasted_iota(I32, (tm, tm), 1)
    before = jnp.dot(jnp.where(c_i < r_i, 1.0, 0.0).astype(BF16), picked.astype(BF16), preferred_element_type=F32)
    base = before + float(GRAN) * loffg
    slot_o = jnp.zeros((tm, LANES), F32)
    wts_o = jnp.zeros((tm, LANES), F32)
    for k in range(TOP_K):
        slot_k = jnp.sum(jnp.where(lane == sel_idx[k], base, 0.0), axis=-1, keepdims=True)
        slot_o = jnp.where(lane == float(k), slot_k, slot_o)
        wts_o = jnp.where(lane == float(k), ROUTED_SCALE * sel_val[k] / total, wts_o)
    slot_ref[...] = slot_o[:, :KPAD].astype(I32)
    wts_ref[...] = wts_o[:, :KPAD]
    slott_ref[0] = slot_o.T[:KPAD, :].astype(I32)

    run = run_ref[...]

    def as_col(v):
        return jnp.sum(jnp.where(e_i == e_j, jnp.broadcast_to(v, (LANES, LANES)), 0.0), axis=1, keepdims=True)

    end_c = as_col(loffg + ng)
    val_c = as_col(run - loffg)
    jj = lax.broadcasted_iota(I32, (LANES, NGRAN), 1).astype(F32)
    ee = lax.broadcasted_iota(I32, (LANES, NGRAN), 0).astype(F32)
    e_of_j = jnp.sum(jnp.where(end_c <= jj, 1.0, 0.0), axis=0, keepdims=True)
    dst = jnp.sum(jnp.where(ee == e_of_j, val_c + jj, 0.0), axis=0, keepdims=True)
    used = jnp.broadcast_to(jnp.sum(ng, axis=-1, keepdims=True), (1, NGRAN))
    gtab_ref[0] = jnp.concatenate([e_of_j, dst, used, jnp.zeros((5, NGRAN), F32)], axis=0).astype(I32)
    new_run = run + ng
    run_ref[...] = new_run
    tot_ref[...] = new_run


def _post_attn(ya, yb, w_out, ctx, x, x_tile_off, ctx_first, mods, mod_sel, g_moe, w_router, b_router):
    b, n, _ = ya.shape
    d = D_MODEL
    tm = TOK_TILE
    nt = n // tm
    t_tot = b * n
    wr = jnp.zeros((d, LANES), F32).at[:, :N_EXPERTS].set(w_router.astype(F32))
    wr_hi = wr.astype(BF16)
    wr_lo = (wr - wr_hi.astype(F32)).astype(BF16)
    wr = jnp.concatenate([wr_hi, wr_lo, wr_hi], axis=0)
    br = jnp.zeros((1, LANES), F32).at[0, :N_EXPERTS].set(b_router.astype(F32))
    flat = lambda c: pl.BlockSpec((tm, c), lambda i, t: (i * nt + t, 0))
    outs = pl.pallas_call(
        functools.partial(_post_attn_kernel, ctx_first=ctx_first),
        grid=(b, nt),
        in_specs=[pl.BlockSpec((1, tm, 512), lambda i, t: (i, t, 0)),
                  pl.BlockSpec((1, tm, 512), lambda i, t: (i, t, 0)),
                  pl.BlockSpec((d, d), lambda i, t: (0, 0)),
                  pl.BlockSpec((1, tm, d), lambda i, t: (i, 0, 0)),
                  pl.BlockSpec((1, tm, d), lambda i, t: (i, jnp.maximum(t + x_tile_off, 0), 0)),
                  pl.BlockSpec((1, 1, 6, d), lambda i, t: (i, mod_sel(t), 0, 0)),
                  pl.BlockSpec((1, d), lambda i, t: (0, 0)),
                  pl.BlockSpec((3 * d, LANES), lambda i, t: (0, 0)),
                  pl.BlockSpec((1, LANES), lambda i, t: (0, 0))],
        out_specs=[pl.BlockSpec((1, tm, d), lambda i, t: (i, t, 0)),
                   flat(d), flat(KPAD), flat(KPAD),
                   pl.BlockSpec((1, KPAD, tm), lambda i, t: (i * nt + t, 0, 0)),
                   pl.BlockSpec((1, 8, NGRAN), lambda i, t: (i * nt + t, 0, 0)),
                   pl.BlockSpec((1, LANES), lambda i, t: (0, 0))],
        out_shape=[jax.ShapeDtypeStruct((b, n, d), F32),
                   jax.ShapeDtypeStruct((t_tot, d), BF16),
                   jax.ShapeDtypeStruct((t_tot, KPAD), I32),
                   jax.ShapeDtypeStruct((t_tot, KPAD), F32),
                   jax.ShapeDtypeStruct((b * nt, KPAD, tm), I32),
                   jax.ShapeDtypeStruct((b * nt, 8, NGRAN), I32),
                   jax.ShapeDtypeStruct((1, LANES), F32)],
        scratch_shapes=[pltpu.VMEM((1, LANES), F32)],
        compiler_params=_cparams(("arbitrary", "arbitrary")),
        name="post_attn_route",
    )(ya, yb, w_out.astype(BF16), ctx, x, mods, g_moe.reshape(1, d), wr, br)
    return outs


def _granule_copy(src, src_g, dst, dst_g, sem):
    return pltpu.make_async_copy(src.at[pl.ds(pl.multiple_of(src_g * GRAN, GRAN), GRAN), :],
                                 dst.at[pl.ds(pl.multiple_of(dst_g * GRAN, GRAN), GRAN), :], sem)


def _for_each(lo, hi, body):
    trips = lax.shift_right_logical(hi - lo, jnp.int32(ISSUE_UNROLL.bit_length() - 1))

    def block(b, carry):
        for u in range(ISSUE_UNROLL):
            body(lo + b * ISSUE_UNROLL + u)
        return carry

    def single(j, carry):
        body(j)
        return carry

    lax.fori_loop(0, trips, block, 0)
    lax.fori_loop(lo + trips * ISSUE_UNROLL, hi, single, 0)


def _drain(src, dst, sem, n):
    rows = WAIT_BATCH * GRAN
    batches = lax.shift_right_logical(n, jnp.int32(WAIT_BATCH.bit_length() - 1))

    def big(j, carry):
        pltpu.make_async_copy(src.at[pl.ds(0, rows), :], dst.at[pl.ds(0, rows), :], sem).wait()
        return carry

    def small(j, carry):
        _granule_copy(src, 0, dst, 0, sem).wait()
        return carry

    lax.fori_loop(0, batches, big, 0)
    lax.fori_loop(0, n - batches * WAIT_BATCH, small, 0)


def _dispatch_kernel(fill_ref, gtab_ref, h_ref, slott_ref, xs_ref, xloc, gprev, sems, *, nt):
    i = pl.program_id(0)
    cur = i % 2
    used = gtab_ref[0, 2, 0]
    slott = slott_ref[0]
    h = h_ref[...]
    tm = h.shape[0]
    half = h.shape[1] // 2
    def sort_chunk(c):
        rows = c * CHUNK + lax.broadcasted_iota(I32, (CHUNK, tm), 0)
        p = jnp.zeros((CHUNK, tm), F32)
        for k in range(TOP_K):
            p = jnp.where(rows == slott[k:k + 1, :], 1.0, p)
        xc = jnp.dot(p.astype(BF16), h, preferred_element_type=F32)
        lo = lax.bitcast_convert_type(xc[:, :half], U32)
        hi = lax.bitcast_convert_type(xc[:, half:], U32)
        xloc[cur, c * CHUNK:(c + 1) * CHUNK, :] = (hi & jnp.uint32(0xFFFF0000)) | (lo >> 16)

    def send(j):
        _granule_copy(xloc.at[cur], j, xs_ref, gtab_ref[0, 0, j], sems.at[cur]).start()

    for c in range(SORT_ROWS // CHUNK):
        if (c + 1) * CHUNK <= ALWAYS_ROWS:
            sort_chunk(c)
            for j in range(c * CHUNK // GRAN, (c + 1) * CHUNK // GRAN):
                send(j)
        else:
            pl.when(c * CHUNK < used * GRAN)(functools.partial(sort_chunk, c))
    _for_each(ALWAYS_ROWS // GRAN, used, send)

    @pl.when(i > 0)
    def _():
        _drain(xloc.at[1 - cur], xs_ref, sems.at[1 - cur], gprev[0])

    gprev[0] = used

    @pl.when(i == nt - 1)
    def _():
        _drain(xloc.at[cur], xs_ref, sems.at[cur], used)
        zeros = xloc.at[1 - cur]
        zeros[0:SLOT_TILE, :] = jnp.zeros((SLOT_TILE, xloc.shape[2]), U32)

        def fill_expert(e, count):
            lo, hi = fill_ref[0, e], fill_ref[1, e]
            lax.fori_loop(lo, hi, lambda g, c: (_granule_copy(zeros, 0, xs_ref, g, sems.at[0]).start(), c)[1], 0)
            return count + (hi - lo)

        _drain(zeros, xs_ref, sems.at[0], lax.fori_loop(0, N_EXPERTS, fill_expert, jnp.int32(0)))

        def tile_copy(t):
            return pltpu.make_async_copy(zeros.at[pl.ds(0, SLOT_TILE), :],
                                         xs_ref.at[pl.ds(pl.multiple_of(t * SLOT_TILE, SLOT_TILE), SLOT_TILE), :],
                                         sems.at[1])

        first_free, n_tiles = fill_ref[2, 0], xs_ref.shape[0] // SLOT_TILE
        lax.fori_loop(first_free, n_tiles, lambda t, c: (tile_copy(t).start(), c)[1], 0)
        lax.fori_loop(first_free, n_tiles, lambda t, c: (tile_copy(0).wait(), c)[1], 0)


def _dispatch(h2, slott, gtab, fill, n_slots):
    t_tot, d = h2.shape
    tm = TOK_TILE
    nt = t_tot // tm
    return pl.pallas_call(
        functools.partial(_dispatch_kernel, nt=nt),
        grid=(nt,),
        in_specs=[pl.BlockSpec(memory_space=pltpu.SMEM),
                  pl.BlockSpec((1, 8, NGRAN), lambda i: (i, 0, 0), memory_space=pltpu.SMEM),
                  pl.BlockSpec((tm, d), lambda i: (i, 0)),
                  pl.BlockSpec((1, KPAD, tm), lambda i: (i, 0, 0))],
        out_specs=pl.BlockSpec(memory_space=pl.ANY),
        out_shape=jax.ShapeDtypeStruct((n_slots, d // 2), U32),
        scratch_shapes=[pltpu.VMEM((2, SORT_ROWS, d // 2), U32), pltpu.SMEM((1,), I32),
                        pltpu.SemaphoreType.DMA((2,))],
        compiler_params=_cparams(("arbitrary",)),
        name="moe_dispatch",
    )(fill, gtab, h2, slott)


def _expert_kernel(te_ref, nused_ref, xs_ref, wg_ref, wu_ref, wd_ref, ys_ref, wgu_s, wd_s):
    i = pl.program_id(0)
    changed = (i == 0) | (te_ref[i] != te_ref[jnp.maximum(i - 1, 0)])

    @pl.when(changed)
    def _():
        wgu_s[:, 0:EXPERT_FF] = wg_ref[0].astype(BF16)
        wgu_s[:, EXPERT_FF:2 * EXPERT_FF] = wu_ref[0].astype(BF16)
        wd_s[...] = wd_ref[0].astype(BF16)

    @pl.when(i < nused_ref[0])
    def _():
        lo, hi = _unpack_bf16_pair(xs_ref[...])
        x = jnp.concatenate([lo.astype(BF16), hi.astype(BF16)], axis=1)
        gu = jnp.dot(x, wgu_s[...], preferred_element_type=F32)
        gate, up = gu[:, :EXPERT_FF], gu[:, EXPERT_FF:]
        act = gate * jax.nn.sigmoid(gate) * up
        y = jnp.dot(act.astype(BF16), wd_s[...], preferred_element_type=F32)
        half = y.shape[1] // 2
        ys_ref[...] = _pack_bf16_pair(y[:, :half], y[:, half:])

    @pl.when(i >= nused_ref[0])
    def _():
        ys_ref[...] = jnp.zeros_like(ys_ref)


def _expert_ffn(xs, tile_expert, n_used, w_gate, w_up, w_down):
    n_slots, w = xs.shape
    ts = SLOT_TILE
    d, f = w_gate.shape[1], w_gate.shape[2]
    grid_spec = pltpu.PrefetchScalarGridSpec(
        num_scalar_prefetch=2,
        grid=(n_slots // ts,),
        in_specs=[pl.BlockSpec((ts, w), lambda i, te, nu: (i, 0)),
                  pl.BlockSpec((1, d, f), lambda i, te, nu: (te[i], 0, 0)),
                  pl.BlockSpec((1, d, f), lambda i, te, nu: (te[i], 0, 0)),
                  pl.BlockSpec((1, f, d), lambda i, te, nu: (te[i], 0, 0))],
        out_specs=pl.BlockSpec((ts, w), lambda i, te, nu: (i, 0)),
        scratch_shapes=[pltpu.VMEM((d, 2 * f), BF16), pltpu.VMEM((f, d), BF16)],
    )
    return pl.pallas_call(
        _expert_kernel,
        grid_spec=grid_spec,
        out_shape=jax.ShapeDtypeStruct((n_slots, w), U32),
        compiler_params=_cparams(("arbitrary",)),
        name="moe_experts",
    )(tile_expert, n_used, xs, w_gate, w_up, w_down)


def _combine_kernel(gtab_ref, gnext_ref, ys_ref, slot_ref, wts_ref, h_ref, x_ref, mod_ref, wsgu_ref,
                    wsd_ref, gf_ref, o_ref, yloc, acc_ref, sems, *, nt, final_norm):
    i = pl.program_id(0)
    cur = i % 2

    @pl.when(i == 0)
    def _():
        _for_each(0, gtab_ref[0, 2, 0], lambda j: _granule_copy(ys_ref, gtab_ref[0, 0, j], yloc, j, sems.at[0]).start())

    n_always = ALWAYS_ROWS // CHUNK
    inline = ALWAYS_ROWS // GRAN
    pieces = [inline * p // (n_always + 1) for p in range(n_always + 2)]

    def fetch_next(j):
        _granule_copy(ys_ref, gnext_ref[0, 0, j], yloc, (1 - cur) * NGRAN + j, sems.at[1 - cur]).start()

    def fetch_piece(p):
        for j in range(pieces[p], pieces[p + 1]):
            fetch_next(j)

    fetch_piece(0)
    h = h_ref[...]
    tm = h.shape[0]
    gu = jnp.dot(h, wsgu_ref[...], preferred_element_type=F32)
    gate, up = gu[:, :SHARED_FF], gu[:, SHARED_FF:]
    shared = jnp.dot((gate * jax.nn.sigmoid(gate) * up).astype(BF16), wsd_ref[...], preferred_element_type=F32)

    used = gtab_ref[0, 2, 0]
    _drain(ys_ref, yloc, sems.at[cur], used)
    row0 = cur * SORT_ROWS

    slot_b = [jnp.broadcast_to(slot_ref[:, k:k + 1], (tm, LANES)) for k in range(TOP_K)]
    wts_b = [jnp.broadcast_to(wts_ref[:, k:k + 1], (tm, LANES)) for k in range(TOP_K)]
    lane = lax.broadcasted_iota(I32, (tm, LANES), 1)
    w = yloc.shape[1]

    def weights(c):
        blocks = []
        for j in range(CHUNK // LANES):
            cols = lane + (c * CHUNK + j * LANES)
            pw = jnp.zeros((tm, LANES), F32)
            for k in range(TOP_K):
                pw = jnp.where(cols == slot_b[k], wts_b[k], pw)
            blocks.append(pw.astype(BF16))
        return jnp.concatenate(blocks, axis=1)

    def values(c, masked):
        packed = yloc[pl.ds(pl.multiple_of(row0 + c * CHUNK, CHUNK), CHUNK), :]
        if masked:
            rows = c * CHUNK + lax.broadcasted_iota(I32, (CHUNK, w), 0)
            packed = jnp.where(rows < used * GRAN, packed, jnp.uint32(0))
        lo, hi = _unpack_bf16_pair(packed)
        return jnp.concatenate([lo.astype(BF16), hi.astype(BF16)], axis=1)

    acc = shared
    for c in range(n_always):
        acc = acc + jnp.dot(weights(c), values(c, False), preferred_element_type=F32)
        fetch_piece(c + 1)
    acc_ref[...] = acc
    for c in range(n_always, SORT_ROWS // CHUNK):
        @pl.when(c * CHUNK < used * GRAN)
        def _():
            acc_ref[...] += jnp.dot(weights(c), values(c, True), preferred_element_type=F32)

    out = x_ref[...] + mod_ref[0, 0, 5:6, :] * acc_ref[...]
    if final_norm:
        out = _rms(out, gf_ref[...])
    o_ref[...] = out

    used_next = gnext_ref[0, 2, 0]
    _for_each(inline, used_next, fetch_next)

    @pl.when(i == nt - 1)
    def _():
        _drain(ys_ref, yloc, sems.at[1 - cur], used_next)


def _combine(ys, slot, wts, gtab, h2, xmid, mods, mod_map, ws_gate, ws_up, ws_down, g_final, final_norm):
    t_tot, d = h2.shape
    tm = TOK_TILE
    nt = t_tot // tm
    wsgu = jnp.concatenate([ws_gate, ws_up], axis=1).astype(BF16)
    return pl.pallas_call(
        functools.partial(_combine_kernel, nt=nt, final_norm=final_norm),
        grid=(nt,),
        in_specs=[pl.BlockSpec((1, 8, NGRAN), lambda i: (i, 0, 0), memory_space=pltpu.SMEM),
                  pl.BlockSpec((1, 8, NGRAN), lambda i: (jnp.minimum(i + 1, nt - 1), 0, 0), memory_space=pltpu.SMEM),
                  pl.BlockSpec(memory_space=pl.ANY),
                  pl.BlockSpec((tm, KPAD), lambda i: (i, 0)),
                  pl.BlockSpec((tm, KPAD), lambda i: (i, 0)),
                  pl.BlockSpec((tm, d), lambda i: (i, 0)),
                  pl.BlockSpec((tm, d), lambda i: (i, 0)),
                  pl.BlockSpec((1, 1, 6, d), mod_map),
                  pl.BlockSpec((d, 2 * SHARED_FF), lambda i: (0, 0)),
                  pl.BlockSpec((SHARED_FF, d), lambda i: (0, 0)),
                  pl.BlockSpec((1, d), lambda i: (0, 0))],
        out_specs=pl.BlockSpec((tm, d), lambda i: (i, 0)),
        out_shape=jax.ShapeDtypeStruct((t_tot, d), F32),
        scratch_shapes=[pltpu.VMEM((2 * SORT_ROWS, d // 2), U32), pltpu.VMEM((tm, d), F32),
                        pltpu.SemaphoreType.DMA((2,))],
        compiler_params=_cparams(("arbitrary",)),
        name="moe_combine",
    )(gtab, gtab, ys, slot, wts, h2, xmid.reshape(t_tot, d), mods, wsgu, ws_down.astype(BF16),
      g_final.reshape(1, d).astype(F32))


def _moe(h2, slot, wts, slott, gtab, tot, xmid, mods, mod_map, moe_w, g_final, final_norm):
    (w_gate, w_up, w_down, ws_gate, ws_up, ws_down) = moe_w
    t_tot = h2.shape[0]
    ts = SLOT_TILE
    gpt = ts // GRAN
    nt = t_tot // TOK_TILE
    max_rows = t_tot * TOP_K + nt * N_EXPERTS * (GRAN - 1)
    n_tiles = -(-max_rows // ts) + N_EXPERTS
    totg = tot[0, :N_EXPERTS].astype(I32)
    tiles_e = (totg + gpt - 1) // gpt
    ends = jnp.cumsum(tiles_e)
    poffg = jnp.zeros((LANES,), I32).at[:N_EXPERTS].set((ends - tiles_e) * gpt)
    n_used = ends[-1:]
    tile_ids = jnp.minimum(jnp.arange(n_tiles, dtype=I32), n_used[0] - 1)
    tile_expert = jnp.sum((ends[None, :] <= tile_ids[:, None]).astype(I32), axis=1)
    tile_expert = jnp.minimum(tile_expert, N_EXPERTS - 1)
    region = jnp.sum(jnp.where(gtab[:, 0, :, None] == jnp.arange(N_EXPERTS, dtype=I32), poffg[:N_EXPERTS], 0), axis=-1)
    gtab = gtab.at[:, 0, :].set(region + gtab[:, 1, :])
    fill = jnp.zeros((8, LANES), I32)
    fill = fill.at[0, :N_EXPERTS].set(poffg[:N_EXPERTS] + totg).at[1, :N_EXPERTS].set(ends * gpt).at[2, 0].set(ends[-1])
    xs = _dispatch(h2, slott, gtab, fill, n_tiles * ts)
    ys = _expert_ffn(xs, tile_expert, n_used.astype(I32), w_gate, w_up, w_down)
    return _combine(ys, slot, wts, gtab, h2, xmid, mods, mod_map, ws_gate, ws_up, ws_down, g_final, final_norm)


def kernel(x, c, ctx, c_ctx, l0_w_ada, l0_b_ada, l0_g_mix, l0_w_in, l0_sink, l0_rpb, l0_w_out, l0_g_moe, l0_w_router, l0_b_router, l0_w_gate, l0_w_up, l0_w_down, l0_ws_gate, l0_ws_up, l0_ws_down, l1_w_ada, l1_b_ada, l1_g_mix, l1_w_in, l1_g_q_lora, l1_w_uq, l1_g_kv_lora, l1_w_ukv, l1_g_qn, l1_g_kn, l1_w_out, l1_g_moe, l1_w_router, l1_b_router, l1_w_gate, l1_w_up, l1_w_down, l1_ws_gate, l1_ws_up, l1_ws_down, g_final):
    b, seq, d = x.shape
    assert d == D_MODEL and ctx.shape[1] == CTX_LEN and seq % TOK_TILE == 0
    u = CTX_LEN + seq
    tiles_u = u // TOK_TILE
    tiles_s = seq // TOK_TILE
    ctx, x = ctx.astype(F32), x.astype(F32)

    cos_a, sin_a = _rope_tables(seq, HEAD_DIM, 0, HEAD_DIM, 1.0)
    cos_q, sin_q = _rope_tables(seq, C_ROPE, C_NOPE, LANES, float((C_NOPE + C_ROPE) ** -0.5))
    cos_k, sin_k = _rope_tables(seq, C_ROPE, 0, LANES, 1.0)

    mods0 = _mods(c, c_ctx, l0_w_ada, l0_b_ada)
    qa, kva, qb, kb, vb = _proj0(ctx, x, mods0, l0_g_mix, l0_w_in, cos_a, sin_a)
    ya = _window_attention(qa, kva, l0_sink)
    yb = _na_attention(qb, kb, vb, l0_rpb)
    xmid, h2, slot, wts, slott, gtab, tot = _post_attn(ya, yb, l0_w_out, ctx, x, -1, True, mods0,
                                                   lambda t: jnp.minimum(t, 1), l0_g_moe, l0_w_router, l0_b_router)
    xa = _moe(h2, slot, wts, slott, gtab, tot, xmid, mods0,
              lambda i: (i // tiles_u, jnp.minimum(i % tiles_u, 1), 0, 0),
              (l0_w_gate, l0_w_up, l0_w_down, l0_ws_gate, l0_ws_up, l0_ws_down), g_final, False).reshape(b, u, d)

    mods1 = _mods(c, c_ctx, l1_w_ada, l1_b_ada)
    qm, qd, km, vm, kvd = _proj1(xa, mods1, l1_g_mix, l1_w_in, l1_g_q_lora, l1_w_uq, l1_g_kv_lora, l1_w_ukv,
                                 l1_g_qn, l1_g_kn, (cos_a, sin_a, cos_q, sin_q, cos_k, sin_k))
    ym = _mla_attention(qm, km, vm)
    yd = _gqa_attention(qd, kvd)
    xmid, h2, slot, wts, slott, gtab, tot = _post_attn(ym, yd, l1_w_out, xa, xa, CTX_LEN // TOK_TILE, False, mods1,
                                                   lambda t: 1, l1_g_moe, l1_w_router, l1_b_router)
    out = _moe(h2, slot, wts, slott, gtab, tot, xmid, mods1, lambda i: (i // tiles_s, 1, 0, 0),
               (l1_w_gate, l1_w_up, l1_w_down, l1_ws_gate, l1_ws_up, l1_ws_down), g_final, True)
    return out.reshape(b, seq, d)
```

```python
import functools

import numpy as np
import jax
import jax.numpy as jnp
from jax import lax
from jax.experimental import pallas as pl
from jax.experimental.pallas import tpu as pltpu

F32 = jnp.float32
BF16 = jnp.bfloat16
U32 = jnp.uint32
I32 = jnp.int32

D_MODEL = 1024
CTX_LEN = 256
GRID_W = 64
HEAD_DIM = 64
ROPE_THETA = 10000.0
NORM_EPS = 1e-6
NEG_INF = -1e30
A_WINDOW = 128
NA_ROWS = 8
NA_COLS = 16
NA_QROWS = 4
NA_BAND = NA_ROWS + NA_QROWS - 1
C_Q_LORA = 384
C_KV_LORA = 256
C_NOPE = 64
C_ROPE = 32
C_V = 64
N_EXPERTS = 64
TOP_K = 6
EXPERT_FF = 256
SHARED_FF = 256
ROUTED_SCALE = 2.5

LANES = 128
TOK_TILE = 256
WINDOW_TILE = 128
ATTN_TILE = 512
SLOT_TILE = 1024
KPAD = 8
GRAN = 8
CHUNK = 256
SORT_ROWS = 2048
NGRAN = SORT_ROWS // GRAN
ALWAYS_ROWS = TOK_TILE * TOP_K
WAIT_BATCH = 16
ISSUE_UNROLL = 4
ROUTE_ROWS = 256
VMEM_LIMIT = 48 * 1024 * 1024


def _cparams(sem):
    return pltpu.CompilerParams(dimension_semantics=sem, vmem_limit_bytes=VMEM_LIMIT)


def _rms(x, g):
    return x * lax.rsqrt(jnp.mean(x * x, axis=-1, keepdims=True) + NORM_EPS) * g


def _norm_mod(x, g, shift, scale):
    return _rms(x, g) * (1.0 + scale) + shift


def _rope128(x, cos, sin):
    lane = lax.broadcasted_iota(I32, x.shape, 1)
    swapped = jnp.where(lane % 2 == 0, pltpu.roll(x, LANES - 1, 1), pltpu.roll(x, 1, 1))
    return x * cos + swapped * sin


def _group_sumsq(x, ones_bd):
    sq = x * x
    hi = sq.astype(BF16)
    lo = (sq - hi.astype(F32)).astype(BF16)
    return (jnp.dot(hi, ones_bd, preferred_element_type=F32) + jnp.dot(lo, ones_bd, preferred_element_type=F32))


def _head_norm(x, ones_bd, g):
    return x * lax.rsqrt(_group_sumsq(x, ones_bd) * (1.0 / HEAD_DIM) + NORM_EPS) * g


def _pack_bf16_pair(lo, hi):
    lo_bits = lax.bitcast_convert_type(lo.astype(BF16).astype(F32), U32)
    hi_bits = lax.bitcast_convert_type(hi.astype(BF16).astype(F32), U32)
    return (hi_bits & jnp.uint32(0xFFFF0000)) | (lo_bits >> 16)


def _unpack_bf16_pair(u):
    lo = lax.bitcast_convert_type(u << 16, F32)
    hi = lax.bitcast_convert_type(u & jnp.uint32(0xFFFF0000), F32)
    return lo, hi


def _keep_half(q128, half):
    lane = lax.broadcasted_iota(I32, q128.shape, 1)
    keep = (lane < HEAD_DIM) if half == 0 else (lane >= HEAD_DIM)
    return jnp.where(keep, q128, jnp.zeros_like(q128))


def _merge_halves(o_even, o_odd):
    lane = lax.broadcasted_iota(I32, o_even.shape, 1)
    return jnp.where(lane < HEAD_DIM, o_even, o_odd)


def _qk(q, k):
    return lax.dot_general(q, k, (((1,), (1,)), ((), ())), preferred_element_type=F32)


def _softmax_pv(s, v, sinks=None):
    m = jnp.max(s, axis=-1, keepdims=True)
    if sinks is not None:
        r = s.shape[0] // len(sinks)
        m = jnp.concatenate([jnp.maximum(m[i * r:(i + 1) * r], sinks[i]) for i in range(len(sinks))], axis=0)
    e = jnp.exp(s - m)
    den = jnp.sum(e, axis=-1, keepdims=True)
    if sinks is not None:
        den = den + jnp.concatenate([jnp.exp(sinks[i] - m[i * r:(i + 1) * r]) for i in range(len(sinks))], axis=0)
    return jnp.dot(e.astype(BF16), v, preferred_element_type=F32) / den


def _ada_kernel(c_ref, w_ref, b_ref, o_ref):
    c = c_ref[...]
    a = c * jax.nn.sigmoid(c)
    o_ref[...] = jnp.dot(a, w_ref[...], precision=lax.Precision.HIGHEST, preferred_element_type=F32) + b_ref[...]


def _ada(cond, w_ada, b_ada):
    n, d = cond.shape
    nout = w_ada.shape[1]
    bn = 512
    return pl.pallas_call(
        _ada_kernel,
        grid=(nout // bn,),
        in_specs=[pl.BlockSpec((n, d), lambda j: (0, 0)),
                  pl.BlockSpec((d, bn), lambda j: (0, j)),
                  pl.BlockSpec((1, bn), lambda j: (0, j))],
        out_specs=pl.BlockSpec((n, bn), lambda j: (0, j)),
        out_shape=jax.ShapeDtypeStruct((n, nout), F32),
        compiler_params=_cparams(("arbitrary",)),
        name="ada",
    )(cond, w_ada, b_ada.reshape(1, nout))


def _mods(c, c_ctx, w_ada, b_ada):
    b = c.shape[0]
    rows = ((b + 1 + 7) // 8) * 8
    cond = jnp.zeros((rows, D_MODEL), F32).at[:b].set(c).at[b].set(c_ctx)
    out = _ada(cond, w_ada, b_ada)
    lat = out[:b].reshape(b, 1, 6, D_MODEL)
    cx = jnp.broadcast_to(out[b].reshape(1, 1, 6, D_MODEL), (b, 1, 6, D_MODEL))
    return jnp.concatenate([cx, lat], axis=1)


def _axial_angles(n_tokens, rot_dim):
    t = np.arange(n_tokens)
    row = (t // GRID_W).astype(np.float32)
    col = (t % GRID_W).astype(np.float32)
    n_axis = rot_dim // 4
    inv_freq = (np.float32(ROPE_THETA) ** (-np.arange(n_axis, dtype=np.float32) / n_axis)).astype(np.float32)
    return jnp.concatenate([jnp.asarray(row[:, None] * inv_freq), jnp.asarray(col[:, None] * inv_freq)], axis=-1)


def _rope_tables(seq, rot_dim, lane_start, period, scale):
    ang = _axial_angles(seq, rot_dim)
    cos = jnp.repeat(jnp.cos(ang), 2, axis=-1)
    sin = jnp.repeat(jnp.sin(ang), 2, axis=-1) * jnp.tile(jnp.asarray([-1.0, 1.0], F32), rot_dim // 2)
    cos_p = jnp.ones((seq, period), F32).at[:, lane_start:lane_start + rot_dim].set(cos)
    sin_p = jnp.zeros((seq, period), F32).at[:, lane_start:lane_start + rot_dim].set(sin)
    cos_f = jnp.concatenate([jnp.ones((CTX_LEN, period), F32), cos_p], axis=0)
    sin_f = jnp.concatenate([jnp.zeros((CTX_LEN, period), F32), sin_p], axis=0)
    reps = LANES // period
    return jnp.tile(cos_f, (1, reps)) * scale, jnp.tile(sin_f, (1, reps)) * scale


def _na_band_start(block, rows):
    return np.clip(block * NA_QROWS - NA_ROWS // 2, 0, rows - NA_BAND)


def _na_bias_table(rpb, rows):
    h = rpb.shape[0]
    qc = np.arange(GRID_W)
    kc = np.arange(GRID_W)
    cstart = np.clip(qc - NA_COLS // 2, 0, GRID_W - NA_COLS)
    col_ok = (kc[None, :] >= cstart[:, None]) & (kc[None, :] < cstart[:, None] + NA_COLS)
    dc = np.clip(kc[None, :] - qc[:, None] + NA_COLS - 1, 0, 2 * NA_COLS - 2)
    col_sel = np.eye(2 * NA_COLS - 1, dtype=np.float32)[dc]
    variants, ids = [], [None]
    for blk in range(rows // NA_QROWS):
        rs0 = _na_band_start(blk, rows)
        r = blk * NA_QROWS + np.arange(NA_QROWS)
        rs = np.clip(r - NA_ROWS // 2, 0, rows - NA_ROWS)
        krow = rs0 + np.arange(NA_BAND)
        row_ok = (krow[None, :] >= rs[:, None]) & (krow[None, :] < rs[:, None] + NA_ROWS)
        dr = np.clip(krow[None, :] - r[:, None] + NA_ROWS - 1, 0, 2 * NA_ROWS - 2)
        key = (row_ok.tobytes(), dr.tobytes())
        if key not in [v[0] for v in variants]:
            variants.append((key, row_ok, dr))
        ids.append([v[0] for v in variants].index(key))
    ids[0] = len(variants)
    tabs = []
    for _, row_ok, dr in variants:
        row_sel = np.eye(2 * NA_ROWS - 1, dtype=np.float32)[dr]
        b = jnp.einsum('hrc,ijr,qkc->hiqjk', rpb.astype(F32), jnp.asarray(row_sel), jnp.asarray(col_sel),
                       precision=lax.Precision.HIGHEST)
        ok = row_ok[:, None, :, None] & col_ok[None, :, None, :]
        b = jnp.where(jnp.asarray(ok)[None], b, NEG_INF)
        tabs.append(b.reshape(h, NA_QROWS * GRID_W, NA_BAND * GRID_W))
    tabs.append(jnp.full((h, NA_QROWS * GRID_W, NA_BAND * GRID_W), NEG_INF, F32))
    loc = jnp.stack(tabs, axis=0)
    tab = jnp.concatenate([jnp.zeros(loc.shape[:3] + (CTX_LEN,), F32), loc], axis=-1)
    return tab, jnp.asarray(ids, I32)


def _dup_heads(w, n_heads):
    d = w.shape[0]
    w = w.reshape(d, n_heads, 1, HEAD_DIM)
    return jnp.broadcast_to(w, (d, n_heads, 2, HEAD_DIM)).reshape(d, n_heads * 2 * HEAD_DIM)


def _proj0_kernel(c_ref, x_ref, mod_ref, g_ref, w_ref, cos_ref, sin_ref, qa_ref, kva_ref, qb_ref, kb_ref, vb_ref):
    xt = jnp.where(pl.program_id(1) == 0, c_ref[0], x_ref[0])
    h = _norm_mod(xt, g_ref[...], mod_ref[0, 0, 0:1, :], mod_ref[0, 0, 1:2, :])
    r = jnp.dot(h.astype(BF16), w_ref[...], preferred_element_type=F32)
    cos, sin = cos_ref[...], sin_ref[...]
    roped = [_rope128(r[:, i * LANES:(i + 1) * LANES], cos, sin) for i in range(6)]
    qa_ref[0] = jnp.concatenate(roped[0:4], axis=1).astype(BF16)
    kva_ref[0] = jnp.concatenate(roped[4:6] + [r[:, 768:1024]], axis=1).astype(BF16)
    qb_ref[0] = r[:, 1024:1536].astype(BF16)
    kb_ref[0] = r[:, 1536:2048].astype(BF16)
    vb_ref[0] = r[:, 2048:2560].astype(BF16)


def _proj0(ctx, x, mods, g_mix, w_in, cos, sin):
    b, seq, d = x.shape
    u = CTX_LEN + seq
    tm = TOK_TILE
    s = 1.0 / 8.0
    w = jnp.concatenate([w_in[:, 0:512] * s, _dup_heads(w_in[:, 512:640], 2), _dup_heads(w_in[:, 640:768], 2),
                         w_in[:, 768:1280] * s, w_in[:, 1280:1792], w_in[:, 1792:2304]], axis=1).astype(BF16)
    nw = w.shape[1]
    tok = lambda n: pl.BlockSpec((1, tm, n), lambda i, t: (i, t, 0))
    return pl.pallas_call(
        _proj0_kernel,
        grid=(b, u // tm),
        in_specs=[pl.BlockSpec((1, tm, d), lambda i, t: (i, 0, 0)),
                  pl.BlockSpec((1, tm, d), lambda i, t: (i, jnp.maximum(t - 1, 0), 0)),
                  pl.BlockSpec((1, 1, 6, d), lambda i, t: (i, jnp.minimum(t, 1), 0, 0)),
                  pl.BlockSpec((1, d), lambda i, t: (0, 0)),
                  pl.BlockSpec((d, nw), lambda i, t: (0, 0)),
                  pl.BlockSpec((tm, LANES), lambda i, t: (t, 0)),
                  pl.BlockSpec((tm, LANES), lambda i, t: (t, 0))],
        out_specs=[tok(512)] * 5,
        out_shape=[jax.ShapeDtypeStruct((b, u, 512), BF16)] * 5,
        compiler_params=_cparams(("arbitrary", "arbitrary")),
        name="proj0",
    )(ctx, x, mods, g_mix.reshape(1, d), w, cos, sin)


def _window_starts(n_tiles, tq, span, u):
    return np.clip(np.arange(n_tiles) * tq - A_WINDOW, CTX_LEN, u - span)


def _window_mask_table(u, tq, span):
    n_tiles = u // tq
    starts = _window_starts(n_tiles, tq, span, u)
    variants, ids = [], []
    for n in range(n_tiles):
        qpos = n * tq + np.arange(tq)[:, None]
        kpos = starts[n] + np.arange(span)[None, :]
        ok = (kpos >= CTX_LEN) & (np.abs(qpos - kpos) <= A_WINDOW) & (qpos >= CTX_LEN)
        key = ok.tobytes()
        if key not in [v[0] for v in variants]:
            variants.append((key, ok))
        ids.append([v[0] for v in variants].index(key))
    local = np.stack([np.where(v[1], 0.0, NEG_INF) for v in variants]).astype(np.float32)
    tab = np.concatenate([np.zeros(local.shape[:2] + (CTX_LEN,), np.float32), local], axis=-1)
    return jnp.asarray(tab), jnp.asarray(ids, I32)


def _window_kernel(var_ref, sink_ref, q_ref, kv_ref, mask_ref, o_ref, *, tq, span):
    n = pl.program_id(1)
    u = kv_ref.shape[1]
    ls = pl.multiple_of(jnp.clip(n * tq - A_WINDOW, CTX_LEN, u - span), A_WINDOW)
    mask = mask_ref[var_ref[n]]
    kv_all = jnp.concatenate([kv_ref[0, 0:CTX_LEN, :], kv_ref[0, pl.ds(ls, span), :]], axis=0)
    outs = [None] * 8
    for g in range(2):
        q4 = jnp.concatenate([_keep_half(q_ref[0, :, ((4 * g + hh) // 2) * LANES:((4 * g + hh) // 2 + 1) * LANES],
                                         hh % 2) for hh in range(4)], axis=0)
        s = _qk(q4, kv_all[:, g * LANES:(g + 1) * LANES])
        s = jnp.concatenate([s[hh * tq:(hh + 1) * tq] + mask for hh in range(4)], axis=0)
        o = _softmax_pv(s, kv_all[:, 256 + g * LANES:256 + (g + 1) * LANES],
                        [sink_ref[4 * g + hh] for hh in range(4)])
        for hh in range(4):
            outs[4 * g + hh] = o[hh * tq:(hh + 1) * tq]
    o_ref[0] = jnp.concatenate([_merge_halves(outs[2 * p], outs[2 * p + 1]) for p in range(4)], axis=1).astype(BF16)


def _window_attention(qa, kva, sink):
    b, u, _ = qa.shape
    tq = WINDOW_TILE
    span = tq + 2 * A_WINDOW
    assert CTX_LEN % tq == 0 and (u - span) % A_WINDOW == 0
    mask_tab, variant = _window_mask_table(u, tq, span)
    return pl.pallas_call(
        functools.partial(_window_kernel, tq=tq, span=span),
        grid=(b, u // tq),
        in_specs=[pl.BlockSpec(memory_space=pltpu.SMEM),
                  pl.BlockSpec(memory_space=pltpu.SMEM),
                  pl.BlockSpec((1, tq, 512), lambda i, n: (i, n, 0)),
                  pl.BlockSpec((1, u, 512), lambda i, n: (i, 0, 0)),
                  pl.BlockSpec(mask_tab.shape, lambda i, n: (0, 0, 0))],
        out_specs=pl.BlockSpec((1, tq, 512), lambda i, n: (i, n, 0)),
        out_shape=jax.ShapeDtypeStruct((b, u, 512), BF16),
        compiler_params=_cparams(("arbitrary", "arbitrary")),
        name="window_attn",
    )(variant, sink.astype(F32), qa, kva, mask_tab)


def _na_kernel(var_ref, q_ref, k_ref, v_ref, bias_ref, o_ref, *, rows):
    j = pl.program_id(1)
    rs0 = jnp.clip((j - 1) * NA_QROWS - NA_ROWS // 2, 0, rows - NA_BAND)
    band = NA_BAND * GRID_W
    start = pl.multiple_of(CTX_LEN + rs0 * GRID_W, GRID_W)
    k_all = jnp.concatenate([k_ref[0, 0:CTX_LEN, :], k_ref[0, pl.ds(start, band), :]], axis=0)
    v_all = jnp.concatenate([v_ref[0, 0:CTX_LEN, :], v_ref[0, pl.ds(start, band), :]], axis=0)
    tq = q_ref.shape[1]
    outs = []
    for p in range(4):
        q128 = q_ref[0, :, p * LANES:(p + 1) * LANES]
        q2 = jnp.concatenate([_keep_half(q128, 0), _keep_half(q128, 1)], axis=0)
        s = _qk(q2, k_all[:, p * LANES:(p + 1) * LANES])
        s = s + jnp.concatenate([bias_ref[0, 2 * p], bias_ref[0, 2 * p + 1]], axis=0)
        o = _softmax_pv(s, v_all[:, p * LANES:(p + 1) * LANES])
        outs.append(_merge_halves(o[:tq], o[tq:]))
    o_ref[0] = jnp.concatenate(outs, axis=1).astype(BF16)


def _na_attention(qb, kb, vb, rpb):
    b, u, _ = qb.shape
    rows = (u - CTX_LEN) // GRID_W
    assert rows >= NA_BAND and rows % NA_QROWS == 0
    bias_tab, variant = _na_bias_table(rpb, rows)
    tq = NA_QROWS * GRID_W
    assert tq == CTX_LEN
    nk = bias_tab.shape[-1]
    grid_spec = pltpu.PrefetchScalarGridSpec(
        num_scalar_prefetch=1,
        grid=(b, u // tq),
        in_specs=[pl.BlockSpec((1, tq, 512), lambda i, j, var: (i, j, 0)),
                  pl.BlockSpec((1, u, 512), lambda i, j, var: (i, 0, 0)),
                  pl.BlockSpec((1, u, 512), lambda i, j, var: (i, 0, 0)),
                  pl.BlockSpec((1, 8, tq, nk), lambda i, j, var: (var[j], 0, 0, 0))],
        out_specs=pl.BlockSpec((1, tq, 512), lambda i, j, var: (i, j, 0)),
    )
    return pl.pallas_call(
        functools.partial(_na_kernel, rows=rows),
        grid_spec=grid_spec,
        out_shape=jax.ShapeDtypeStruct((b, u, 512), BF16),
        compiler_params=_cparams(("arbitrary", "arbitrary")),
        name="na_attn",
    )(variant, qb, kb, vb, bias_tab)


def _proj1_kernel(x_ref, mod_ref, g_ref, w_ref, gq_ref, wuq_ref, gkv_ref, wkv_ref, gqn_ref, gkn_ref, ones_ref,
                  cosa_ref, sina_ref, cosq_ref, sinq_ref, cosk_ref, sink_ref,
                  qm_ref, qd_ref, km_ref, vm_ref, kvd_ref):
    h = _norm_mod(x_ref[0], g_ref[...], mod_ref[0, 0, 0:1, :], mod_ref[0, 0, 1:2, :])
    r = jnp.dot(h.astype(BF16), w_ref[...], preferred_element_type=F32)
    ones_bd = ones_ref[...]
    cosa, sina = cosa_ref[...], sina_ref[...]
    cq = _rms(r[:, 0:C_Q_LORA], gq_ref[...])
    qm = jnp.dot(cq.astype(BF16), wuq_ref[...], preferred_element_type=F32)
    cosq, sinq = cosq_ref[...], sinq_ref[...]
    qm_ref[0] = jnp.concatenate([_rope128(qm[:, i * LANES:(i + 1) * LANES], cosq, sinq) for i in range(8)],
                                axis=1).astype(BF16)
    gqn = gqn_ref[...]
    qd_ref[0] = jnp.concatenate(
        [_rope128(_head_norm(r[:, 384 + i * LANES:384 + (i + 1) * LANES], ones_bd, gqn), cosa, sina)
         for i in range(4)], axis=1).astype(BF16)
    ckv = _rms(r[:, 896:1152], gkv_ref[...])
    kr = _rope128(r[:, 1152:1280], cosk_ref[...], sink_ref[...])
    kv = jnp.dot(jnp.concatenate([ckv, kr], axis=1).astype(BF16), wkv_ref[...], preferred_element_type=F32)
    km_ref[0] = kv[:, 0:1024].astype(BF16)
    vm_ref[0] = kv[:, 1024:1536].astype(BF16)
    gkn = gkn_ref[...]
    kd = [_rope128(_head_norm(r[:, 1280 + i * LANES:1280 + (i + 1) * LANES], ones_bd, gkn), cosa, sina)
          for i in range(2)]
    kvd_ref[0] = jnp.concatenate(kd + [r[:, 1536:1792]], axis=1).astype(BF16)


def _proj1(xa, mods, g_mix, w_in, g_q_lora, w_uq, g_kv_lora, w_ukv, g_qn, g_kn, tabs):
    b, u, d = xa.shape
    tm = TOK_TILE
    zpad = jnp.zeros((d, LANES - C_ROPE), F32)
    w = jnp.concatenate([w_in[:, 0:896], w_in[:, 896:1152], w_in[:, 1152:1184], zpad,
                         _dup_heads(w_in[:, 1184:1312], 2), _dup_heads(w_in[:, 1312:1440], 2)], axis=1).astype(BF16)
    nw = w.shape[1]
    wuq = w_uq.reshape(C_Q_LORA, 8, C_NOPE + C_ROPE)
    wuq = jnp.concatenate([wuq, jnp.zeros((C_Q_LORA, 8, LANES - C_NOPE - C_ROPE), F32)], axis=-1)
    wuq = wuq.reshape(C_Q_LORA, 8 * LANES).astype(BF16)
    wukv = w_ukv.reshape(C_KV_LORA, 8, C_NOPE + C_V)
    wk = jnp.concatenate([wukv[:, :, :C_NOPE], jnp.zeros((C_KV_LORA, 8, LANES - C_NOPE), F32)], axis=-1)
    wk = wk.reshape(C_KV_LORA, 8 * LANES)
    wv = wukv[:, :, C_NOPE:].reshape(C_KV_LORA, 8 * C_V)
    place = np.zeros((LANES, 8, LANES), np.float32)
    for j in range(C_ROPE):
        place[j, :, C_NOPE + j] = 1.0
    place = jnp.asarray(place.reshape(LANES, 8 * LANES))
    wkv = jnp.concatenate([jnp.concatenate([wk, wv], axis=1),
                           jnp.concatenate([place, jnp.zeros((LANES, 8 * C_V), F32)], axis=1)], axis=0).astype(BF16)
    ones_bd = jnp.asarray(np.kron(np.eye(2, dtype=np.float32), np.ones((HEAD_DIM, HEAD_DIM), np.float32))).astype(BF16)
    gqn = jnp.tile(g_qn.astype(F32) * (1.0 / 8.0), 2).reshape(1, LANES)
    gkn = jnp.tile(g_kn.astype(F32), 2).reshape(1, LANES)
    cosa, sina, cosq, sinq, cosk, sink = tabs
    tok = lambda n: pl.BlockSpec((1, tm, n), lambda i, t: (i, t, 0))
    lat = lambda n: pl.BlockSpec((1, tm, n), lambda i, t: (i, jnp.maximum(t - 1, 0), 0))
    whole = lambda a: pl.BlockSpec(a.shape, lambda i, t: (0,) * a.ndim)
    tab = pl.BlockSpec((tm, LANES), lambda i, t: (t, 0))
    small = [g_mix.reshape(1, d), w, g_q_lora.reshape(1, -1).astype(F32), wuq, g_kv_lora.reshape(1, -1).astype(F32),
             wkv, gqn, gkn, ones_bd]
    return pl.pallas_call(
        _proj1_kernel,
        grid=(b, u // tm),
        in_specs=[tok(d), pl.BlockSpec((1, 1, 6, d), lambda i, t: (i, jnp.minimum(t, 1), 0, 0))]
                 + [whole(a) for a in small] + [tab] * 6,
        out_specs=[lat(1024), lat(512), tok(1024), tok(512), tok(512)],
        out_shape=[jax.ShapeDtypeStruct((b, u - CTX_LEN, n), BF16) for n in (1024, 512)]
                  + [jax.ShapeDtypeStruct((b, u, n), BF16) for n in (1024, 512, 512)],
        compiler_params=_cparams(("arbitrary", "arbitrary")),
        name="proj1",
    )(xa, mods, *small, cosa, sina, cosq, sinq, cosk, sink)


def _mla_kernel(q_ref, k_ref, v_ref, o_ref):
    outs = []
    for h in range(8):
        s = _qk(q_ref[0, :, h * LANES:(h + 1) * LANES], k_ref[0, :, h * LANES:(h + 1) * LANES])
        outs.append(_softmax_pv(s, v_ref[0, :, (h // 2) * LANES:(h // 2 + 1) * LANES]))
    o_ref[0] = jnp.concatenate([_merge_halves(outs[2 * p], outs[2 * p + 1]) for p in range(4)], axis=1).astype(BF16)


def _mla_attention(qm, km, vm):
    b, seq, _ = qm.shape
    u = km.shape[1]
    tq = ATTN_TILE
    return pl.pallas_call(
        _mla_kernel,
        grid=(b, seq // tq),
        in_specs=[pl.BlockSpec((1, tq, 1024), lambda i, n: (i, n, 0)),
                  pl.BlockSpec((1, u, 1024), lambda i, n: (i, 0, 0)),
                  pl.BlockSpec((1, u, 512), lambda i, n: (i, 0, 0))],
        out_specs=pl.BlockSpec((1, tq, 512), lambda i, n: (i, n, 0)),
        out_shape=jax.ShapeDtypeStruct((b, seq, 512), BF16),
        compiler_params=_cparams(("arbitrary", "arbitrary")),
        name="mla_attn",
    )(qm, km, vm)


def _gqa_kernel(q_ref, kv_ref, o_ref):
    outs = [None] * 8
    for g in range(2):
        k = kv_ref[0, :, g * LANES:(g + 1) * LANES]
        v = kv_ref[0, :, 256 + g * LANES:256 + (g + 1) * LANES]
        for hh in range(4):
            h = 4 * g + hh
            q = _keep_half(q_ref[0, :, (h // 2) * LANES:(h // 2 + 1) * LANES], h % 2)
            outs[h] = _softmax_pv(_qk(q, k), v)
    o_ref[0] = jnp.concatenate([_merge_halves(outs[2 * p], outs[2 * p + 1]) for p in range(4)], axis=1).astype(BF16)


def _gqa_attention(qd, kvd):
    b, seq, _ = qd.shape
    u = kvd.shape[1]
    tq = ATTN_TILE
    return pl.pallas_call(
        _gqa_kernel,
        grid=(b, seq // tq),
        in_specs=[pl.BlockSpec((1, tq, 512), lambda i, n: (i, n, 0)),
                  pl.BlockSpec((1, u, 512), lambda i, n: (i, 0, 0))],
        out_specs=pl.BlockSpec((1, tq, 512), lambda i, n: (i, n, 0)),
        out_shape=jax.ShapeDtypeStruct((b, seq, 512), BF16),
        compiler_params=_cparams(("arbitrary", "arbitrary")),
        name="gqa_attn",
    )(qd, kvd)


def _post_attn_kernel(ya_ref, yb_ref, w_ref, c_ref, x_ref, mod_ref, g_ref, wr_ref, br_ref,
                      xmid_ref, h2_ref, slot_ref, wts_ref, slott_ref, gtab_ref, tot_ref, run_ref, *, ctx_first):
    first = (pl.program_id(0) == 0) & (pl.program_id(1) == 0)
    x_in = jnp.where(pl.program_id(1) == 0, c_ref[0], x_ref[0]) if ctx_first else x_ref[0]

    @pl.when(first)
    def _():
        run_ref[...] = jnp.zeros_like(run_ref)

    y = (jnp.dot(ya_ref[0], w_ref[0:512, :], preferred_element_type=F32)
         + jnp.dot(yb_ref[0], w_ref[512:1024, :], preferred_element_type=F32))
    x1 = x_in + mod_ref[0, 0, 2:3, :] * y
    xmid_ref[0] = x1
    h2 = _norm_mod(x1, g_ref[...], mod_ref[0, 0, 3:4, :], mod_ref[0, 0, 4:5, :])
    h_hi = h2.astype(BF16)
    h2_ref[...] = h_hi

    h_lo = (h2 - h_hi.astype(F32)).astype(BF16)
    logits = jnp.dot(jnp.concatenate([h_hi, h_hi, h_lo], axis=1), wr_ref[...], preferred_element_type=F32)
    scores = jax.nn.sigmoid(logits)
    tm = scores.shape[0]
    lane = lax.broadcasted_iota(I32, (tm, LANES), 1).astype(F32)
    biased = jnp.where(lane < N_EXPERTS, scores + br_ref[...], -jnp.inf)
    def top_k(scores_g, biased_g, lane_g):
        picked_g = jnp.zeros_like(scores_g)
        idx_g, val_g = [], []
        for _k in range(TOP_K):
            m = jnp.max(biased_g, axis=-1, keepdims=True)
            i_k = jnp.min(jnp.where(biased_g == m, lane_g, float(LANES)), axis=-1, keepdims=True)
            hit = lane_g == i_k
            idx_g.append(i_k)
            val_g.append(jnp.sum(jnp.where(hit, scores_g, 0.0), axis=-1, keepdims=True))
            picked_g = jnp.where(hit, 1.0, picked_g)
            biased_g = jnp.where(hit, -jnp.inf, biased_g)
        return idx_g, val_g, picked_g

    lane_g = lax.broadcasted_iota(I32, (ROUTE_ROWS, LANES), 1).astype(F32)
    groups = [top_k(scores[r:r + ROUTE_ROWS], biased[r:r + ROUTE_ROWS], lane_g) for r in range(0, tm, ROUTE_ROWS)]
    sel_idx = [jnp.concatenate([g[0][k] for g in groups], axis=0) for k in range(TOP_K)]
    sel_val = [jnp.concatenate([g[1][k] for g in groups], axis=0) for k in range(TOP_K)]
    picked = jnp.concatenate([g[2] for g in groups], axis=0)
    total = sel_val[0]
    for v in sel_val[1:]:
        total = total + v
    cnt = jnp.sum(picked, axis=0, keepdims=True)
    ng = jnp.floor((cnt + float(GRAN - 1)) * (1.0 / GRAN))
    e_i = lax.broadcasted_iota(I32, (LANES, LANES), 0)
    e_j = lax.broadcasted_iota(I32, (LANES, LANES), 1)
    upper = jnp.where(e_i < e_j, 1.0, 0.0).astype(BF16)
    loffg = jnp.dot(jnp.broadcast_to(ng, (8, LANES)).astype(BF16), upper, preferred_element_type=F32)[0:1]
    r_i = lax.broadcasted_iota(I32, (tm, tm), 0)
    c_i = lax.broadcasted_iota(I32, (tm, tm), 1)
    before = jnp.dot(jnp.where(c_i < r_i, 1.0, 0.0).astype(BF16), picked.astype(BF16), preferred_element_type=F32)
    base = before + float(GRAN) * loffg
    slot_o = jnp.zeros((tm, LANES), F32)
    wts_o = jnp.zeros((tm, LANES), F32)
    for k in range(TOP_K):
        slot_k = jnp.sum(jnp.where(lane == sel_idx[k], base, 0.0), axis=-1, keepdims=True)
        slot_o = jnp.where(lane == float(k), slot_k, slot_o)
        wts_o = jnp.where(lane == float(k), ROUTED_SCALE * sel_val[k] / total, wts_o)
    slot_ref[...] = slot_o[:, :KPAD].astype(I32)
    wts_ref[...] = wts_o[:, :KPAD]
    slott_ref[0] = slot_o.T[:KPAD, :].astype(I32)

    run = run_ref[...]

    def as_col(v):
        return jnp.sum(jnp.where(e_i == e_j, jnp.broadcast_to(v, (LANES, LANES)), 0.0), axis=1, keepdims=True)

    end_c = as_col(loffg + ng)
    val_c = as_col(run - loffg)
    jj = lax.broadcasted_iota(I32, (LANES, NGRAN), 1).astype(F32)
    ee = lax.broadcasted_iota(I32, (LANES, NGRAN), 0).astype(F32)
    e_of_j = jnp.sum(jnp.where(end_c <= jj, 1.0, 0.0), axis=0, keepdims=True)
    dst = jnp.sum(jnp.where(ee == e_of_j, val_c + jj, 0.0), axis=0, keepdims=True)
    used = jnp.broadcast_to(jnp.sum(ng, axis=-1, keepdims=True), (1, NGRAN))
    gtab_ref[0] = jnp.concatenate([e_of_j, dst, used, jnp.zeros((5, NGRAN), F32)], axis=0).astype(I32)
    new_run = run + ng
    run_ref[...] = new_run
    tot_ref[...] = new_run


def _post_attn(ya, yb, w_out, ctx, x, x_tile_off, ctx_first, mods, mod_sel, g_moe, w_router, b_router):
    b, n, _ = ya.shape
    d = D_MODEL
    tm = TOK_TILE
    nt = n // tm
    t_tot = b * n
    wr = jnp.zeros((d, LANES), F32).at[:, :N_EXPERTS].set(w_router.astype(F32))
    wr_hi = wr.astype(BF16)
    wr_lo = (wr - wr_hi.astype(F32)).astype(BF16)
    wr = jnp.concatenate([wr_hi, wr_lo, wr_hi], axis=0)
    br = jnp.zeros((1, LANES), F32).at[0, :N_EXPERTS].set(b_router.astype(F32))
    flat = lambda c: pl.BlockSpec((tm, c), lambda i, t: (i * nt + t, 0))
    outs = pl.pallas_call(
        functools.partial(_post_attn_kernel, ctx_first=ctx_first),
        grid=(b, nt),
        in_specs=[pl.BlockSpec((1, tm, 512), lambda i, t: (i, t, 0)),
                  pl.BlockSpec((1, tm, 512), lambda i, t: (i, t, 0)),
                  pl.BlockSpec((d, d), lambda i, t: (0, 0)),
                  pl.BlockSpec((1, tm, d), lambda i, t: (i, 0, 0)),
                  pl.BlockSpec((1, tm, d), lambda i, t: (i, jnp.maximum(t + x_tile_off, 0), 0)),
                  pl.BlockSpec((1, 1, 6, d), lambda i, t: (i, mod_sel(t), 0, 0)),
                  pl.BlockSpec((1, d), lambda i, t: (0, 0)),
                  pl.BlockSpec((3 * d, LANES), lambda i, t: (0, 0)),
                  pl.BlockSpec((1, LANES), lambda i, t: (0, 0))],
        out_specs=[pl.BlockSpec((1, tm, d), lambda i, t: (i, t, 0)),
                   flat(d), flat(KPAD), flat(KPAD),
                   pl.BlockSpec((1, KPAD, tm), lambda i, t: (i * nt + t, 0, 0)),
                   pl.BlockSpec((1, 8, NGRAN), lambda i, t: (i * nt + t, 0, 0)),
                   pl.BlockSpec((1, LANES), lambda i, t: (0, 0))],
        out_shape=[jax.ShapeDtypeStruct((b, n, d), F32),
                   jax.ShapeDtypeStruct((t_tot, d), BF16),
                   jax.ShapeDtypeStruct((t_tot, KPAD), I32),
                   jax.ShapeDtypeStruct((t_tot, KPAD), F32),
                   jax.ShapeDtypeStruct((b * nt, KPAD, tm), I32),
                   jax.ShapeDtypeStruct((b * nt, 8, NGRAN), I32),
                   jax.ShapeDtypeStruct((1, LANES), F32)],
        scratch_shapes=[pltpu.VMEM((1, LANES), F32)],
        compiler_params=_cparams(("arbitrary", "arbitrary")),
        name="post_attn_route",
    )(ya, yb, w_out.astype(BF16), ctx, x, mods, g_moe.reshape(1, d), wr, br)
    return outs


def _granule_copy(src, src_g, dst, dst_g, sem):
    return pltpu.make_async_copy(src.at[pl.ds(pl.multiple_of(src_g * GRAN, GRAN), GRAN), :],
                                 dst.at[pl.ds(pl.multiple_of(dst_g * GRAN, GRAN), GRAN), :], sem)


def _for_each(lo, hi, body):
    trips = lax.shift_right_logical(hi - lo, jnp.int32(ISSUE_UNROLL.bit_length() - 1))

    def block(b, carry):
        for u in range(ISSUE_UNROLL):
            body(lo + b * ISSUE_UNROLL + u)
        return carry

    def single(j, carry):
        body(j)
        return carry

    lax.fori_loop(0, trips, block, 0)
    lax.fori_loop(lo + trips * ISSUE_UNROLL, hi, single, 0)


def _drain(src, dst, sem, n):
    rows = WAIT_BATCH * GRAN
    batches = lax.shift_right_logical(n, jnp.int32(WAIT_BATCH.bit_length() - 1))

    def big(j, carry):
        pltpu.make_async_copy(src.at[pl.ds(0, rows), :], dst.at[pl.ds(0, rows), :], sem).wait()
        return carry

    def small(j, carry):
        _granule_copy(src, 0, dst, 0, sem).wait()
        return carry

    lax.fori_loop(0, batches, big, 0)
    lax.fori_loop(0, n - batches * WAIT_BATCH, small, 0)


def _dispatch_kernel(fill_ref, gtab_ref, h_ref, slott_ref, xs_ref, xloc, gprev, sems, *, nt):
    i = pl.program_id(0)
    cur = i % 2
    used = gtab_ref[0, 2, 0]
    slott = slott_ref[0]
    h = h_ref[...]
    tm = h.shape[0]
    half = h.shape[1] // 2
    def sort_chunk(c):
        rows = c * CHUNK + lax.broadcasted_iota(I32, (CHUNK, tm), 0)
        p = jnp.zeros((CHUNK, tm), F32)
        for k in range(TOP_K):
            p = jnp.where(rows == slott[k:k + 1, :], 1.0, p)
        xc = jnp.dot(p.astype(BF16), h, preferred_element_type=F32)
        lo = lax.bitcast_convert_type(xc[:, :half], U32)
        hi = lax.bitcast_convert_type(xc[:, half:], U32)
        xloc[cur, c * CHUNK:(c + 1) * CHUNK, :] = (hi & jnp.uint32(0xFFFF0000)) | (lo >> 16)

    def send(j):
        _granule_copy(xloc.at[cur], j, xs_ref, gtab_ref[0, 0, j], sems.at[cur]).start()

    for c in range(SORT_ROWS // CHUNK):
        if (c + 1) * CHUNK <= ALWAYS_ROWS:
            sort_chunk(c)
            for j in range(c * CHUNK // GRAN, (c + 1) * CHUNK // GRAN):
                send(j)
        else:
            pl.when(c * CHUNK < used * GRAN)(functools.partial(sort_chunk, c))
    _for_each(ALWAYS_ROWS // GRAN, used, send)

    @pl.when(i > 0)
    def _():
        _drain(xloc.at[1 - cur], xs_ref, sems.at[1 - cur], gprev[0])

    gprev[0] = used

    @pl.when(i == nt - 1)
    def _():
        _drain(xloc.at[cur], xs_ref, sems.at[cur], used)
        zeros = xloc.at[1 - cur]
        zeros[0:SLOT_TILE, :] = jnp.zeros((SLOT_TILE, xloc.shape[2]), U32)

        def fill_expert(e, count):
            lo, hi = fill_ref[0, e], fill_ref[1, e]
            lax.fori_loop(lo, hi, lambda g, c: (_granule_copy(zeros, 0, xs_ref, g, sems.at[0]).start(), c)[1], 0)
            return count + (hi - lo)

        _drain(zeros, xs_ref, sems.at[0], lax.fori_loop(0, N_EXPERTS, fill_expert, jnp.int32(0)))

        def tile_copy(t):
            return pltpu.make_async_copy(zeros.at[pl.ds(0, SLOT_TILE), :],
                                         xs_ref.at[pl.ds(pl.multiple_of(t * SLOT_TILE, SLOT_TILE), SLOT_TILE), :],
                                         sems.at[1])

        first_free, n_tiles = fill_ref[2, 0], xs_ref.shape[0] // SLOT_TILE
        lax.fori_loop(first_free, n_tiles, lambda t, c: (tile_copy(t).start(), c)[1], 0)
        lax.fori_loop(first_free, n_tiles, lambda t, c: (tile_copy(0).wait(), c)[1], 0)


def _dispatch(h2, slott, gtab, fill, n_slots):
    t_tot, d = h2.shape
    tm = TOK_TILE
    nt = t_tot // tm
    return pl.pallas_call(
        functools.partial(_dispatch_kernel, nt=nt),
        grid=(nt,),
        in_specs=[pl.BlockSpec(memory_space=pltpu.SMEM),
                  pl.BlockSpec((1, 8, NGRAN), lambda i: (i, 0, 0), memory_space=pltpu.SMEM),
                  pl.BlockSpec((tm, d), lambda i: (i, 0)),
                  pl.BlockSpec((1, KPAD, tm), lambda i: (i, 0, 0))],
        out_specs=pl.BlockSpec(memory_space=pl.ANY),
        out_shape=jax.ShapeDtypeStruct((n_slots, d // 2), U32),
        scratch_shapes=[pltpu.VMEM((2, SORT_ROWS, d // 2), U32), pltpu.SMEM((1,), I32),
                        pltpu.SemaphoreType.DMA((2,))],
        compiler_params=_cparams(("arbitrary",)),
        name="moe_dispatch",
    )(fill, gtab, h2, slott)


def _expert_kernel(te_ref, nused_ref, xs_ref, wg_ref, wu_ref, wd_ref, ys_ref, wgu_s, wd_s):
    i = pl.program_id(0)
    changed = (i == 0) | (te_ref[i] != te_ref[jnp.maximum(i - 1, 0)])

    @pl.when(changed)
    def _():
        wgu_s[:, 0:EXPERT_FF] = wg_ref[0].astype(BF16)
        wgu_s[:, EXPERT_FF:2 * EXPERT_FF] = wu_ref[0].astype(BF16)
        wd_s[...] = wd_ref[0].astype(BF16)

    @pl.when(i < nused_ref[0])
    def _():
        lo, hi = _unpack_bf16_pair(xs_ref[...])
        x = jnp.concatenate([lo.astype(BF16), hi.astype(BF16)], axis=1)
        gu = jnp.dot(x, wgu_s[...], preferred_element_type=F32)
        gate, up = gu[:, :EXPERT_FF], gu[:, EXPERT_FF:]
        act = gate * jax.nn.sigmoid(gate) * up
        y = jnp.dot(act.astype(BF16), wd_s[...], preferred_element_type=F32)
        half = y.shape[1] // 2
        ys_ref[...] = _pack_bf16_pair(y[:, :half], y[:, half:])

    @pl.when(i >= nused_ref[0])
    def _():
        ys_ref[...] = jnp.zeros_like(ys_ref)


def _expert_ffn(xs, tile_expert, n_used, w_gate, w_up, w_down):
    n_slots, w = xs.shape
    ts = SLOT_TILE
    d, f = w_gate.shape[1], w_gate.shape[2]
    grid_spec = pltpu.PrefetchScalarGridSpec(
        num_scalar_prefetch=2,
        grid=(n_slots // ts,),
        in_specs=[pl.BlockSpec((ts, w), lambda i, te, nu: (i, 0)),
                  pl.BlockSpec((1, d, f), lambda i, te, nu: (te[i], 0, 0)),
                  pl.BlockSpec((1, d, f), lambda i, te, nu: (te[i], 0, 0)),
                  pl.BlockSpec((1, f, d), lambda i, te, nu: (te[i], 0, 0))],
        out_specs=pl.BlockSpec((ts, w), lambda i, te, nu: (i, 0)),
        scratch_shapes=[pltpu.VMEM((d, 2 * f), BF16), pltpu.VMEM((f, d), BF16)],
    )
    return pl.pallas_call(
        _expert_kernel,
        grid_spec=grid_spec,
        out_shape=jax.ShapeDtypeStruct((n_slots, w), U32),
        compiler_params=_cparams(("arbitrary",)),
        name="moe_experts",
    )(tile_expert, n_used, xs, w_gate, w_up, w_down)


def _combine_kernel(gtab_ref, gnext_ref, ys_ref, slot_ref, wts_ref, h_ref, x_ref, mod_ref, wsgu_ref,
                    wsd_ref, gf_ref, o_ref, yloc, acc_ref, sems, *, nt, final_norm):
    i = pl.program_id(0)
    cur = i % 2

    @pl.when(i == 0)
    def _():
        _for_each(0, gtab_ref[0, 2, 0], lambda j: _granule_copy(ys_ref, gtab_ref[0, 0, j], yloc, j, sems.at[0]).start())

    n_always = ALWAYS_ROWS // CHUNK
    inline = ALWAYS_ROWS // GRAN
    pieces = [inline * p // (n_always + 1) for p in range(n_always + 2)]

    def fetch_next(j):
        _granule_copy(ys_ref, gnext_ref[0, 0, j], yloc, (1 - cur) * NGRAN + j, sems.at[1 - cur]).start()

    def fetch_piece(p):
        for j in range(pieces[p], pieces[p + 1]):
            fetch_next(j)

    fetch_piece(0)
    h = h_ref[...]
    tm = h.shape[0]
    gu = jnp.dot(h, wsgu_ref[...], preferred_element_type=F32)
    gate, up = gu[:, :SHARED_FF], gu[:, SHARED_FF:]
    shared = jnp.dot((gate * jax.nn.sigmoid(gate) * up).astype(BF16), wsd_ref[...], preferred_element_type=F32)

    used = gtab_ref[0, 2, 0]
    _drain(ys_ref, yloc, sems.at[cur], used)
    row0 = cur * SORT_ROWS

    slot_b = [jnp.broadcast_to(slot_ref[:, k:k + 1], (tm, LANES)) for k in range(TOP_K)]
    wts_b = [jnp.broadcast_to(wts_ref[:, k:k + 1], (tm, LANES)) for k in range(TOP_K)]
    lane = lax.broadcasted_iota(I32, (tm, LANES), 1)
    w = yloc.shape[1]

    def weights(c):
        blocks = []
        for j in range(CHUNK // LANES):
            cols = lane + (c * CHUNK + j * LANES)
            pw = jnp.zeros((tm, LANES), F32)
            for k in range(TOP_K):
                pw = jnp.where(cols == slot_b[k], wts_b[k], pw)
            blocks.append(pw.astype(BF16))
        return jnp.concatenate(blocks, axis=1)

    def values(c, masked):
        packed = yloc[pl.ds(pl.multiple_of(row0 + c * CHUNK, CHUNK), CHUNK), :]
        if masked:
            rows = c * CHUNK + lax.broadcasted_iota(I32, (CHUNK, w), 0)
            packed = jnp.where(rows < used * GRAN, packed, jnp.uint32(0))
        lo, hi = _unpack_bf16_pair(packed)
        return jnp.concatenate([lo.astype(BF16), hi.astype(BF16)], axis=1)

    acc = shared
    for c in range(n_always):
        acc = acc + jnp.dot(weights(c), values(c, False), preferred_element_type=F32)
        fetch_piece(c + 1)
    acc_ref[...] = acc
    for c in range(n_always, SORT_ROWS // CHUNK):
        @pl.when(c * CHUNK < used * GRAN)
        def _():
            acc_ref[...] += jnp.dot(weights(c), values(c, True), preferred_element_type=F32)

    out = x_ref[...] + mod_ref[0, 0, 5:6, :] * acc_ref[...]
    if final_norm:
        out = _rms(out, gf_ref[...])
    o_ref[...] = out

    used_next = gnext_ref[0, 2, 0]
    _for_each(inline, used_next, fetch_next)

    @pl.when(i == nt - 1)
    def _():
        _drain(ys_ref, yloc, sems.at[1 - cur], used_next)


def _combine(ys, slot, wts, gtab, h2, xmid, mods, mod_map, ws_gate, ws_up, ws_down, g_final, final_norm):
    t_tot, d = h2.shape
    tm = TOK_TILE
    nt = t_tot // tm
    wsgu = jnp.concatenate([ws_gate, ws_up], axis=1).astype(BF16)
    return pl.pallas_call(
        functools.partial(_combine_kernel, nt=nt, final_norm=final_norm),
        grid=(nt,),
        in_specs=[pl.BlockSpec((1, 8, NGRAN), lambda i: (i, 0, 0), memory_space=pltpu.SMEM),
                  pl.BlockSpec((1, 8, NGRAN), lambda i: (jnp.minimum(i + 1, nt - 1), 0, 0), memory_space=pltpu.SMEM),
                  pl.BlockSpec(memory_space=pl.ANY),
                  pl.BlockSpec((tm, KPAD), lambda i: (i, 0)),
                  pl.BlockSpec((tm, KPAD), lambda i: (i, 0)),
                  pl.BlockSpec((tm, d), lambda i: (i, 0)),
                  pl.BlockSpec((tm, d), lambda i: (i, 0)),
                  pl.BlockSpec((1, 1, 6, d), mod_map),
                  pl.BlockSpec((d, 2 * SHARED_FF), lambda i: (0, 0)),
                  pl.BlockSpec((SHARED_FF, d), lambda i: (0, 0)),
                  pl.BlockSpec((1, d), lambda i: (0, 0))],
        out_specs=pl.BlockSpec((tm, d), lambda i: (i, 0)),
        out_shape=jax.ShapeDtypeStruct((t_tot, d), F32),
        scratch_shapes=[pltpu.VMEM((2 * SORT_ROWS, d // 2), U32), pltpu.VMEM((tm, d), F32),
                        pltpu.SemaphoreType.DMA((2,))],
        compiler_params=_cparams(("arbitrary",)),
        name="moe_combine",
    )(gtab, gtab, ys, slot, wts, h2, xmid.reshape(t_tot, d), mods, wsgu, ws_down.astype(BF16),
      g_final.reshape(1, d).astype(F32))


def _moe(h2, slot, wts, slott, gtab, tot, xmid, mods, mod_map, moe_w, g_final, final_norm):
    (w_gate, w_up, w_down, ws_gate, ws_up, ws_down) = moe_w
    t_tot = h2.shape[0]
    ts = SLOT_TILE
    gpt = ts // GRAN
    nt = t_tot // TOK_TILE
    max_rows = t_tot * TOP_K + nt * N_EXPERTS * (GRAN - 1)
    n_tiles = -(-max_rows // ts) + N_EXPERTS
    totg = tot[0, :N_EXPERTS].astype(I32)
    tiles_e = (totg + gpt - 1) // gpt
    ends = jnp.cumsum(tiles_e)
    poffg = jnp.zeros((LANES,), I32).at[:N_EXPERTS].set((ends - tiles_e) * gpt)
    n_used = ends[-1:]
    tile_ids = jnp.minimum(jnp.arange(n_tiles, dtype=I32), n_used[0] - 1)
    tile_expert = jnp.sum((ends[None, :] <= tile_ids[:, None]).astype(I32), axis=1)
    tile_expert = jnp.minimum(tile_expert, N_EXPERTS - 1)
    region = jnp.sum(jnp.where(gtab[:, 0, :, None] == jnp.arange(N_EXPERTS, dtype=I32), poffg[:N_EXPERTS], 0), axis=-1)
    gtab = gtab.at[:, 0, :].set(region + gtab[:, 1, :])
    fill = jnp.zeros((8, LANES), I32)
    fill = fill.at[0, :N_EXPERTS].set(poffg[:N_EXPERTS] + totg).at[1, :N_EXPERTS].set(ends * gpt).at[2, 0].set(ends[-1])
    xs = _dispatch(h2, slott, gtab, fill, n_tiles * ts)
    ys = _expert_ffn(xs, tile_expert, n_used.astype(I32), w_gate, w_up, w_down)
    return _combine(ys, slot, wts, gtab, h2, xmid, mods, mod_map, ws_gate, ws_up, ws_down, g_final, final_norm)


def kernel(x, c, ctx, c_ctx, l0_w_ada, l0_b_ada, l0_g_mix, l0_w_in, l0_sink, l0_rpb, l0_w_out, l0_g_moe, l0_w_router, l0_b_router, l0_w_gate, l0_w_up, l0_w_down, l0_ws_gate, l0_ws_up, l0_ws_down, l1_w_ada, l1_b_ada, l1_g_mix, l1_w_in, l1_g_q_lora, l1_w_uq, l1_g_kv_lora, l1_w_ukv, l1_g_qn, l1_g_kn, l1_w_out, l1_g_moe, l1_w_router, l1_b_router, l1_w_gate, l1_w_up, l1_w_down, l1_ws_gate, l1_ws_up, l1_ws_down, g_final):
    b, seq, d = x.shape
    assert d == D_MODEL and ctx.shape[1] == CTX_LEN and seq % TOK_TILE == 0
    u = CTX_LEN + seq
    tiles_u = u // TOK_TILE
    tiles_s = seq // TOK_TILE
    ctx, x = ctx.astype(F32), x.astype(F32)

    cos_a, sin_a = _rope_tables(seq, HEAD_DIM, 0, HEAD_DIM, 1.0)
    cos_q, sin_q = _rope_tables(seq, C_ROPE, C_NOPE, LANES, float((C_NOPE + C_ROPE) ** -0.5))
    cos_k, sin_k = _rope_tables(seq, C_ROPE, 0, LANES, 1.0)

    mods0 = _mods(c, c_ctx, l0_w_ada, l0_b_ada)
    qa, kva, qb, kb, vb = _proj0(ctx, x, mods0, l0_g_mix, l0_w_in, cos_a, sin_a)
    ya = _window_attention(qa, kva, l0_sink)
    yb = _na_attention(qb, kb, vb, l0_rpb)
    xmid, h2, slot, wts, slott, gtab, tot = _post_attn(ya, yb, l0_w_out, ctx, x, -1, True, mods0,
                                                   lambda t: jnp.minimum(t, 1), l0_g_moe, l0_w_router, l0_b_router)
    xa = _moe(h2, slot, wts, slott, gtab, tot, xmid, mods0,
              lambda i: (i // tiles_u, jnp.minimum(i % tiles_u, 1), 0, 0),
              (l0_w_gate, l0_w_up, l0_w_down, l0_ws_gate, l0_ws_up, l0_ws_down), g_final, False).reshape(b, u, d)

    mods1 = _mods(c, c_ctx, l1_w_ada, l1_b_ada)
    qm, qd, km, vm, kvd = _proj1(xa, mods1, l1_g_mix, l1_w_in, l1_g_q_lora, l1_w_uq, l1_g_kv_lora, l1_w_ukv,
                                 l1_g_qn, l1_g_kn, (cos_a, sin_a, cos_q, sin_q, cos_k, sin_k))
    ym = _mla_attention(qm, km, vm)
    yd = _gqa_attention(qd, kvd)
    xmid, h2, slot, wts, slott, gtab, tot = _post_attn(ym, yd, l1_w_out, xa, xa, CTX_LEN // TOK_TILE, False, mods1,
                                                   lambda t: 1, l1_g_moe, l1_w_router, l1_b_router)
    out = _moe(h2, slot, wts, slott, gtab, tot, xmid, mods1, lambda i: (i // tiles_s, 1, 0, 0),
               (l1_w_gate, l1_w_up, l1_w_down, l1_ws_gate, l1_ws_up, l1_ws_down), g_final, True)
    return out.reshape(b, seq, d)
```

```python
import functools

import numpy as np
import jax
import jax.numpy as jnp
from jax import lax
from jax.experimental import pallas as pl
from jax.experimental.pallas import tpu as pltpu

F32 = jnp.float32
BF16 = jnp.bfloat16
U32 = jnp.uint32
I32 = jnp.int32

D_MODEL = 1024
CTX_LEN = 256
GRID_W = 64
HEAD_DIM = 64
ROPE_THETA = 10000.0
NORM_EPS = 1e-6
NEG_INF = -1e30
A_WINDOW = 128
NA_ROWS = 8
NA_COLS = 16
NA_QROWS = 4
NA_BAND = NA_ROWS + NA_QROWS - 1
C_Q_LORA = 384
C_KV_LORA = 256
C_NOPE = 64
C_ROPE = 32
C_V = 64
N_EXPERTS = 64
TOP_K = 6
EXPERT_FF = 256
SHARED_FF = 256
ROUTED_SCALE = 2.5

LANES = 128
TOK_TILE = 256
WINDOW_TILE = 128
WINDOW_STEP_TILES = 2
ATTN_TILE = 512
SLOT_TILE = 1024
KPAD = 8
GRAN = 8
CHUNK = 256
SORT_ROWS = 2048
NGRAN = SORT_ROWS // GRAN
ALWAYS_ROWS = TOK_TILE * TOP_K
WAIT_BATCH = 16
ISSUE_UNROLL = 4
VMEM_LIMIT = 48 * 1024 * 1024


def _cparams(sem):
    return pltpu.CompilerParams(dimension_semantics=sem, vmem_limit_bytes=VMEM_LIMIT)


def _rms(x, g):
    return x * lax.rsqrt(jnp.mean(x * x, axis=-1, keepdims=True) + NORM_EPS) * g


def _norm_mod(x, g, shift, scale):
    return _rms(x, g) * (1.0 + scale) + shift


def _rope128(x, cos, sin):
    lane = lax.broadcasted_iota(I32, x.shape, 1)
    swapped = jnp.where(lane % 2 == 0, pltpu.roll(x, LANES - 1, 1), pltpu.roll(x, 1, 1))
    return x * cos + swapped * sin


def _group_sumsq(x, ones_bd):
    sq = x * x
    hi = sq.astype(BF16)
    lo = (sq - hi.astype(F32)).astype(BF16)
    return (jnp.dot(hi, ones_bd, preferred_element_type=F32) + jnp.dot(lo, ones_bd, preferred_element_type=F32))


def _head_norm(x, ones_bd, g):
    return x * lax.rsqrt(_group_sumsq(x, ones_bd) * (1.0 / HEAD_DIM) + NORM_EPS) * g


def _pack_bf16_pair(lo, hi):
    lo_bits = lax.bitcast_convert_type(lo.astype(BF16).astype(F32), U32)
    hi_bits = lax.bitcast_convert_type(hi.astype(BF16).astype(F32), U32)
    return (hi_bits & jnp.uint32(0xFFFF0000)) | (lo_bits >> 16)


def _unpack_bf16_pair(u):
    lo = lax.bitcast_convert_type(u << 16, F32)
    hi = lax.bitcast_convert_type(u & jnp.uint32(0xFFFF0000), F32)
    return lo, hi


def _keep_half(q128, half):
    lane = lax.broadcasted_iota(I32, q128.shape, 1)
    keep = (lane < HEAD_DIM) if half == 0 else (lane >= HEAD_DIM)
    return jnp.where(keep, q128, jnp.zeros_like(q128))


def _merge_halves(o_even, o_odd):
    lane = lax.broadcasted_iota(I32, o_even.shape, 1)
    return jnp.where(lane < HEAD_DIM, o_even, o_odd)


def _qk(q, k):
    return lax.dot_general(q, k, (((1,), (1,)), ((), ())), preferred_element_type=F32)


def _softmax_pv(s, v, sinks=None):
    m = jnp.max(s, axis=-1, keepdims=True)
    if sinks is not None:
        r = s.shape[0] // len(sinks)
        m = jnp.concatenate([jnp.maximum(m[i * r:(i + 1) * r], sinks[i]) for i in range(len(sinks))], axis=0)
    e = jnp.exp(s - m)
    den = jnp.sum(e, axis=-1, keepdims=True)
    if sinks is not None:
        den = den + jnp.concatenate([jnp.exp(sinks[i] - m[i * r:(i + 1) * r]) for i in range(len(sinks))], axis=0)
    return jnp.dot(e.astype(BF16), v, preferred_element_type=F32) / den


def _ada_kernel(c_ref, w_ref, b_ref, o_ref):
    c = c_ref[...]
    a = c * jax.nn.sigmoid(c)
    o_ref[...] = jnp.dot(a, w_ref[...], precision=lax.Precision.HIGHEST, preferred_element_type=F32) + b_ref[...]


def _ada(cond, w_ada, b_ada):
    n, d = cond.shape
    nout = w_ada.shape[1]
    bn = 512
    return pl.pallas_call(
        _ada_kernel,
        grid=(nout // bn,),
        in_specs=[pl.BlockSpec((n, d), lambda j: (0, 0)),
                  pl.BlockSpec((d, bn), lambda j: (0, j)),
                  pl.BlockSpec((1, bn), lambda j: (0, j))],
        out_specs=pl.BlockSpec((n, bn), lambda j: (0, j)),
        out_shape=jax.ShapeDtypeStruct((n, nout), F32),
        compiler_params=_cparams(("arbitrary",)),
        name="ada",
    )(cond, w_ada, b_ada.reshape(1, nout))


def _mods(c, c_ctx, w_ada, b_ada):
    b = c.shape[0]
    rows = ((b + 1 + 7) // 8) * 8
    cond = jnp.zeros((rows, D_MODEL), F32).at[:b].set(c).at[b].set(c_ctx)
    out = _ada(cond, w_ada, b_ada)
    lat = out[:b].reshape(b, 1, 6, D_MODEL)
    cx = jnp.broadcast_to(out[b].reshape(1, 1, 6, D_MODEL), (b, 1, 6, D_MODEL))
    return jnp.concatenate([cx, lat], axis=1)


def _axial_angles(n_tokens, rot_dim):
    t = np.arange(n_tokens)
    row = (t // GRID_W).astype(np.float32)
    col = (t % GRID_W).astype(np.float32)
    n_axis = rot_dim // 4
    inv_freq = (np.float32(ROPE_THETA) ** (-np.arange(n_axis, dtype=np.float32) / n_axis)).astype(np.float32)
    return jnp.concatenate([jnp.asarray(row[:, None] * inv_freq), jnp.asarray(col[:, None] * inv_freq)], axis=-1)


def _rope_tables(seq, rot_dim, lane_start, period, scale):
    ang = _axial_angles(seq, rot_dim)
    cos = jnp.repeat(jnp.cos(ang), 2, axis=-1)
    sin = jnp.repeat(jnp.sin(ang), 2, axis=-1) * jnp.tile(jnp.asarray([-1.0, 1.0], F32), rot_dim // 2)
    cos_p = jnp.ones((seq, period), F32).at[:, lane_start:lane_start + rot_dim].set(cos)
    sin_p = jnp.zeros((seq, period), F32).at[:, lane_start:lane_start + rot_dim].set(sin)
    cos_f = jnp.concatenate([jnp.ones((CTX_LEN, period), F32), cos_p], axis=0)
    sin_f = jnp.concatenate([jnp.zeros((CTX_LEN, period), F32), sin_p], axis=0)
    reps = LANES // period
    return jnp.tile(cos_f, (1, reps)) * scale, jnp.tile(sin_f, (1, reps)) * scale


def _na_band_start(block, rows):
    return np.clip(block * NA_QROWS - NA_ROWS // 2, 0, rows - NA_BAND)


def _na_bias_table(rpb, rows):
    h = rpb.shape[0]
    qc = np.arange(GRID_W)
    kc = np.arange(GRID_W)
    cstart = np.clip(qc - NA_COLS // 2, 0, GRID_W - NA_COLS)
    col_ok = (kc[None, :] >= cstart[:, None]) & (kc[None, :] < cstart[:, None] + NA_COLS)
    dc = np.clip(kc[None, :] - qc[:, None] + NA_COLS - 1, 0, 2 * NA_COLS - 2)
    col_sel = np.eye(2 * NA_COLS - 1, dtype=np.float32)[dc]
    variants, ids = [], [None]
    for blk in range(rows // NA_QROWS):
        rs0 = _na_band_start(blk, rows)
        r = blk * NA_QROWS + np.arange(NA_QROWS)
        rs = np.clip(r - NA_ROWS // 2, 0, rows - NA_ROWS)
        krow = rs0 + np.arange(NA_BAND)
        row_ok = (krow[None, :] >= rs[:, None]) & (krow[None, :] < rs[:, None] + NA_ROWS)
        dr = np.clip(krow[None, :] - r[:, None] + NA_ROWS - 1, 0, 2 * NA_ROWS - 2)
        key = (row_ok.tobytes(), dr.tobytes())
        if key not in [v[0] for v in variants]:
            variants.append((key, row_ok, dr))
        ids.append([v[0] for v in variants].index(key))
    ids[0] = len(variants)
    tabs = []
    for _, row_ok, dr in variants:
        row_sel = np.eye(2 * NA_ROWS - 1, dtype=np.float32)[dr]
        b = jnp.einsum('hrc,ijr,qkc->hiqjk', rpb.astype(F32), jnp.asarray(row_sel), jnp.asarray(col_sel),
                       precision=lax.Precision.HIGHEST)
        ok = row_ok[:, None, :, None] & col_ok[None, :, None, :]
        b = jnp.where(jnp.asarray(ok)[None], b, NEG_INF)
        tabs.append(b.reshape(h, NA_QROWS * GRID_W, NA_BAND * GRID_W))
    tabs.append(jnp.full((h, NA_QROWS * GRID_W, NA_BAND * GRID_W), NEG_INF, F32))
    loc = jnp.stack(tabs, axis=0)
    tab = jnp.concatenate([jnp.zeros(loc.shape[:3] + (CTX_LEN,), F32), loc], axis=-1)
    return tab, jnp.asarray(ids, I32)


def _dup_heads(w, n_heads):
    d = w.shape[0]
    w = w.reshape(d, n_heads, 1, HEAD_DIM)
    return jnp.broadcast_to(w, (d, n_heads, 2, HEAD_DIM)).reshape(d, n_heads * 2 * HEAD_DIM)


def _proj0_kernel(c_ref, x_ref, mod_ref, g_ref, w_ref, cos_ref, sin_ref, qa_ref, kva_ref, qb_ref, kb_ref, vb_ref):
    xt = jnp.where(pl.program_id(1) == 0, c_ref[0], x_ref[0])
    h = _norm_mod(xt, g_ref[...], mod_ref[0, 0, 0:1, :], mod_ref[0, 0, 1:2, :])
    r = jnp.dot(h.astype(BF16), w_ref[...], preferred_element_type=F32)
    cos, sin = cos_ref[...], sin_ref[...]
    roped = [_rope128(r[:, i * LANES:(i + 1) * LANES], cos, sin) for i in range(6)]
    qa_ref[0] = jnp.concatenate(roped[0:4], axis=1).astype(BF16)
    kva_ref[0] = jnp.concatenate(roped[4:6] + [r[:, 768:1024]], axis=1).astype(BF16)
    qb_ref[0] = r[:, 1024:1536].astype(BF16)
    kb_ref[0] = r[:, 1536:2048].astype(BF16)
    vb_ref[0] = r[:, 2048:2560].astype(BF16)


def _proj0(ctx, x, mods, g_mix, w_in, cos, sin):
    b, seq, d = x.shape
    u = CTX_LEN + seq
    tm = TOK_TILE
    s = 1.0 / 8.0
    w = jnp.concatenate([w_in[:, 0:512] * s, _dup_heads(w_in[:, 512:640], 2), _dup_heads(w_in[:, 640:768], 2),
                         w_in[:, 768:1280] * s, w_in[:, 1280:1792], w_in[:, 1792:2304]], axis=1).astype(BF16)
    nw = w.shape[1]
    tok = lambda n: pl.BlockSpec((1, tm, n), lambda i, t: (i, t, 0))
    return pl.pallas_call(
        _proj0_kernel,
        grid=(b, u // tm),
        in_specs=[pl.BlockSpec((1, tm, d), lambda i, t: (i, 0, 0)),
                  pl.BlockSpec((1, tm, d), lambda i, t: (i, jnp.maximum(t - 1, 0), 0)),
                  pl.BlockSpec((1, 1, 6, d), lambda i, t: (i, jnp.minimum(t, 1), 0, 0)),
                  pl.BlockSpec((1, d), lambda i, t: (0, 0)),
                  pl.BlockSpec((d, nw), lambda i, t: (0, 0)),
                  pl.BlockSpec((tm, LANES), lambda i, t: (t, 0)),
                  pl.BlockSpec((tm, LANES), lambda i, t: (t, 0))],
        out_specs=[tok(512)] * 5,
        out_shape=[jax.ShapeDtypeStruct((b, u, 512), BF16)] * 5,
        compiler_params=_cparams(("arbitrary", "arbitrary")),
        name="proj0",
    )(ctx, x, mods, g_mix.reshape(1, d), w, cos, sin)


def _window_starts(n_tiles, tq, span, u):
    return np.clip(np.arange(n_tiles) * tq - A_WINDOW, CTX_LEN, u - span)


def _window_mask_table(u, tq, span):
    n_tiles = u // tq
    starts = _window_starts(n_tiles, tq, span, u)
    variants, ids = [], []
    for n in range(n_tiles):
        qpos = n * tq + np.arange(tq)[:, None]
        kpos = starts[n] + np.arange(span)[None, :]
        ok = (kpos >= CTX_LEN) & (np.abs(qpos - kpos) <= A_WINDOW) & (qpos >= CTX_LEN)
        key = ok.tobytes()
        if key not in [v[0] for v in variants]:
            variants.append((key, ok))
        ids.append([v[0] for v in variants].index(key))
    local = np.stack([np.where(v[1], 0.0, NEG_INF) for v in variants]).astype(np.float32)
    tab = np.concatenate([np.zeros(local.shape[:2] + (CTX_LEN,), np.float32), local], axis=-1)
    return jnp.asarray(tab), jnp.asarray(ids, I32)


def _window_kernel(var_ref, sink_ref, q_ref, kv_ref, mask_ref, o_ref, *, tq, span):
    u = kv_ref.shape[1]
    for sub in range(WINDOW_STEP_TILES):
        n = pl.program_id(1) * WINDOW_STEP_TILES + sub
        ls = pl.multiple_of(jnp.clip(n * tq - A_WINDOW, CTX_LEN, u - span), A_WINDOW)
        mask = mask_ref[var_ref[n]]
        kv_all = jnp.concatenate([kv_ref[0, 0:CTX_LEN, :], kv_ref[0, pl.ds(ls, span), :]], axis=0)
        outs = [None] * 8
        for g in range(2):
            q4 = jnp.concatenate(
                [_keep_half(q_ref[0, sub * tq:(sub + 1) * tq, ((4 * g + hh) // 2) * LANES:((4 * g + hh) // 2 + 1) * LANES],
                            hh % 2) for hh in range(4)], axis=0)
            s = _qk(q4, kv_all[:, g * LANES:(g + 1) * LANES])
            s = jnp.concatenate([s[hh * tq:(hh + 1) * tq] + mask for hh in range(4)], axis=0)
            o = _softmax_pv(s, kv_all[:, 256 + g * LANES:256 + (g + 1) * LANES],
                            [sink_ref[4 * g + hh] for hh in range(4)])
            for hh in range(4):
                outs[4 * g + hh] = o[hh * tq:(hh + 1) * tq]
        o_ref[0, sub * tq:(sub + 1) * tq, :] = jnp.concatenate(
            [_merge_halves(outs[2 * p], outs[2 * p + 1]) for p in range(4)], axis=1).astype(BF16)


def _window_attention(qa, kva, sink):
    b, u, _ = qa.shape
    tq = WINDOW_TILE
    span = tq + 2 * A_WINDOW
    assert CTX_LEN % tq == 0 and (u - span) % A_WINDOW == 0
    mask_tab, variant = _window_mask_table(u, tq, span)
    rows = tq * WINDOW_STEP_TILES
    assert u % rows == 0
    return pl.pallas_call(
        functools.partial(_window_kernel, tq=tq, span=span),
        grid=(b, u // rows),
        in_specs=[pl.BlockSpec(memory_space=pltpu.SMEM),
                  pl.BlockSpec(memory_space=pltpu.SMEM),
                  pl.BlockSpec((1, rows, 512), lambda i, n: (i, n, 0)),
                  pl.BlockSpec((1, u, 512), lambda i, n: (i, 0, 0)),
                  pl.BlockSpec(mask_tab.shape, lambda i, n: (0, 0, 0))],
        out_specs=pl.BlockSpec((1, rows, 512), lambda i, n: (i, n, 0)),
        out_shape=jax.ShapeDtypeStruct((b, u, 512), BF16),
        compiler_params=_cparams(("arbitrary", "arbitrary")),
        name="window_attn",
    )(variant, sink.astype(F32), qa, kva, mask_tab)


def _na_kernel(var_ref, q_ref, k_ref, v_ref, bias_ref, o_ref, *, rows):
    j = pl.program_id(1)
    rs0 = jnp.clip((j - 1) * NA_QROWS - NA_ROWS // 2, 0, rows - NA_BAND)
    band = NA_BAND * GRID_W
    start = pl.multiple_of(CTX_LEN + rs0 * GRID_W, GRID_W)
    k_all = jnp.concatenate([k_ref[0, 0:CTX_LEN, :], k_ref[0, pl.ds(start, band), :]], axis=0)
    v_all = jnp.concatenate([v_ref[0, 0:CTX_LEN, :], v_ref[0, pl.ds(start, band), :]], axis=0)
    tq = q_ref.shape[1]
    outs = []
    for p in range(4):
        q128 = q_ref[0, :, p * LANES:(p + 1) * LANES]
        q2 = jnp.concatenate([_keep_half(q128, 0), _keep_half(q128, 1)], axis=0)
        s = _qk(q2, k_all[:, p * LANES:(p + 1) * LANES])
        s = s + jnp.concatenate([bias_ref[0, 2 * p], bias_ref[0, 2 * p + 1]], axis=0)
        o = _softmax_pv(s, v_all[:, p * LANES:(p + 1) * LANES])
        outs.append(_merge_halves(o[:tq], o[tq:]))
    o_ref[0] = jnp.concatenate(outs, axis=1).astype(BF16)


def _na_attention(qb, kb, vb, rpb):
    b, u, _ = qb.shape
    rows = (u - CTX_LEN) // GRID_W
    assert rows >= NA_BAND and rows % NA_QROWS == 0
    bias_tab, variant = _na_bias_table(rpb, rows)
    tq = NA_QROWS * GRID_W
    assert tq == CTX_LEN
    nk = bias_tab.shape[-1]
    grid_spec = pltpu.PrefetchScalarGridSpec(
        num_scalar_prefetch=1,
        grid=(b, u // tq),
        in_specs=[pl.BlockSpec((1, tq, 512), lambda i, j, var: (i, j, 0)),
                  pl.BlockSpec((1, u, 512), lambda i, j, var: (i, 0, 0)),
                  pl.BlockSpec((1, u, 512), lambda i, j, var: (i, 0, 0)),
                  pl.BlockSpec((1, 8, tq, nk), lambda i, j, var: (var[j], 0, 0, 0))],
        out_specs=pl.BlockSpec((1, tq, 512), lambda i, j, var: (i, j, 0)),
    )
    return pl.pallas_call(
        functools.partial(_na_kernel, rows=rows),
        grid_spec=grid_spec,
        out_shape=jax.ShapeDtypeStruct((b, u, 512), BF16),
        compiler_params=_cparams(("arbitrary", "arbitrary")),
        name="na_attn",
    )(variant, qb, kb, vb, bias_tab)


def _proj1_kernel(x_ref, mod_ref, g_ref, w_ref, gq_ref, wuq_ref, gkv_ref, wkv_ref, gqn_ref, gkn_ref, ones_ref,
                  cosa_ref, sina_ref, cosq_ref, sinq_ref, cosk_ref, sink_ref,
                  qm_ref, qd_ref, km_ref, vm_ref, kvd_ref):
    h = _norm_mod(x_ref[0], g_ref[...], mod_ref[0, 0, 0:1, :], mod_ref[0, 0, 1:2, :])
    r = jnp.dot(h.astype(BF16), w_ref[...], preferred_element_type=F32)
    ones_bd = ones_ref[...]
    cosa, sina = cosa_ref[...], sina_ref[...]
    cq = _rms(r[:, 0:C_Q_LORA], gq_ref[...])
    qm = jnp.dot(cq.astype(BF16), wuq_ref[...], preferred_element_type=F32)
    cosq, sinq = cosq_ref[...], sinq_ref[...]
    qm_ref[0] = jnp.concatenate([_rope128(qm[:, i * LANES:(i + 1) * LANES], cosq, sinq) for i in range(8)],
                                axis=1).astype(BF16)
    gqn = gqn_ref[...]
    qd_ref[0] = jnp.concatenate(
        [_rope128(_head_norm(r[:, 384 + i * LANES:384 + (i + 1) * LANES], ones_bd, gqn), cosa, sina)
         for i in range(4)], axis=1).astype(BF16)
    ckv = _rms(r[:, 896:1152], gkv_ref[...])
    kr = _rope128(r[:, 1152:1280], cosk_ref[...], sink_ref[...])
    kv = jnp.dot(jnp.concatenate([ckv, kr], axis=1).astype(BF16), wkv_ref[...], preferred_element_type=F32)
    km_ref[0] = kv[:, 0:1024].astype(BF16)
    vm_ref[0] = kv[:, 1024:1536].astype(BF16)
    gkn = gkn_ref[...]
    kd = [_rope128(_head_norm(r[:, 1280 + i * LANES:1280 + (i + 1) * LANES], ones_bd, gkn), cosa, sina)
          for i in range(2)]
    kvd_ref[0] = jnp.concatenate(kd + [r[:, 1536:1792]], axis=1).astype(BF16)


def _proj1(xa, mods, g_mix, w_in, g_q_lora, w_uq, g_kv_lora, w_ukv, g_qn, g_kn, tabs):
    b, u, d = xa.shape
    tm = TOK_TILE
    zpad = jnp.zeros((d, LANES - C_ROPE), F32)
    w = jnp.concatenate([w_in[:, 0:896], w_in[:, 896:1152], w_in[:, 1152:1184], zpad,
                         _dup_heads(w_in[:, 1184:1312], 2), _dup_heads(w_in[:, 1312:1440], 2)], axis=1).astype(BF16)
    nw = w.shape[1]
    wuq = w_uq.reshape(C_Q_LORA, 8, C_NOPE + C_ROPE)
    wuq = jnp.concatenate([wuq, jnp.zeros((C_Q_LORA, 8, LANES - C_NOPE - C_ROPE), F32)], axis=-1)
    wuq = wuq.reshape(C_Q_LORA, 8 * LANES).astype(BF16)
    wukv = w_ukv.reshape(C_KV_LORA, 8, C_NOPE + C_V)
    wk = jnp.concatenate([wukv[:, :, :C_NOPE], jnp.zeros((C_KV_LORA, 8, LANES - C_NOPE), F32)], axis=-1)
    wk = wk.reshape(C_KV_LORA, 8 * LANES)
    wv = wukv[:, :, C_NOPE:].reshape(C_KV_LORA, 8 * C_V)
    place = np.zeros((LANES, 8, LANES), np.float32)
    for j in range(C_ROPE):
        place[j, :, C_NOPE + j] = 1.0
    place = jnp.asarray(place.reshape(LANES, 8 * LANES))
    wkv = jnp.concatenate([jnp.concatenate([wk, wv], axis=1),
                           jnp.concatenate([place, jnp.zeros((LANES, 8 * C_V), F32)], axis=1)], axis=0).astype(BF16)
    ones_bd = jnp.asarray(np.kron(np.eye(2, dtype=np.float32), np.ones((HEAD_DIM, HEAD_DIM), np.float32))).astype(BF16)
    gqn = jnp.tile(g_qn.astype(F32) * (1.0 / 8.0), 2).reshape(1, LANES)
    gkn = jnp.tile(g_kn.astype(F32), 2).reshape(1, LANES)
    cosa, sina, cosq, sinq, cosk, sink = tabs
    tok = lambda n: pl.BlockSpec((1, tm, n), lambda i, t: (i, t, 0))
    lat = lambda n: pl.BlockSpec((1, tm, n), lambda i, t: (i, jnp.maximum(t - 1, 0), 0))
    whole = lambda a: pl.BlockSpec(a.shape, lambda i, t: (0,) * a.ndim)
    tab = pl.BlockSpec((tm, LANES), lambda i, t: (t, 0))
    small = [g_mix.reshape(1, d), w, g_q_lora.reshape(1, -1).astype(F32), wuq, g_kv_lora.reshape(1, -1).astype(F32),
             wkv, gqn, gkn, ones_bd]
    return pl.pallas_call(
        _proj1_kernel,
        grid=(b, u // tm),
        in_specs=[tok(d), pl.BlockSpec((1, 1, 6, d), lambda i, t: (i, jnp.minimum(t, 1), 0, 0))]
                 + [whole(a) for a in small] + [tab] * 6,
        out_specs=[lat(1024), lat(512), tok(1024), tok(512), tok(512)],
        out_shape=[jax.ShapeDtypeStruct((b, u - CTX_LEN, n), BF16) for n in (1024, 512)]
                  + [jax.ShapeDtypeStruct((b, u, n), BF16) for n in (1024, 512, 512)],
        compiler_params=_cparams(("arbitrary", "arbitrary")),
        name="proj1",
    )(xa, mods, *small, cosa, sina, cosq, sinq, cosk, sink)


def _mla_kernel(q_ref, k_ref, v_ref, o_ref):
    outs = []
    for h in range(8):
        s = _qk(q_ref[0, :, h * LANES:(h + 1) * LANES], k_ref[0, :, h * LANES:(h + 1) * LANES])
        outs.append(_softmax_pv(s, v_ref[0, :, (h // 2) * LANES:(h // 2 + 1) * LANES]))
    o_ref[0] = jnp.concatenate([_merge_halves(outs[2 * p], outs[2 * p + 1]) for p in range(4)], axis=1).astype(BF16)


def _mla_attention(qm, km, vm):
    b, seq, _ = qm.shape
    u = km.shape[1]
    tq = ATTN_TILE
    return pl.pallas_call(
        _mla_kernel,
        grid=(b, seq // tq),
        in_specs=[pl.BlockSpec((1, tq, 1024), lambda i, n: (i, n, 0)),
                  pl.BlockSpec((1, u, 1024), lambda i, n: (i, 0, 0)),
                  pl.BlockSpec((1, u, 512), lambda i, n: (i, 0, 0))],
        out_specs=pl.BlockSpec((1, tq, 512), lambda i, n: (i, n, 0)),
        out_shape=jax.ShapeDtypeStruct((b, seq, 512), BF16),
        compiler_params=_cparams(("arbitrary", "arbitrary")),
        name="mla_attn",
    )(qm, km, vm)


def _gqa_kernel(q_ref, kv_ref, o_ref):
    outs = [None] * 8
    for g in range(2):
        k = kv_ref[0, :, g * LANES:(g + 1) * LANES]
        v = kv_ref[0, :, 256 + g * LANES:256 + (g + 1) * LANES]
        for hh in range(4):
            h = 4 * g + hh
            q = _keep_half(q_ref[0, :, (h // 2) * LANES:(h // 2 + 1) * LANES], h % 2)
            outs[h] = _softmax_pv(_qk(q, k), v)
    o_ref[0] = jnp.concatenate([_merge_halves(outs[2 * p], outs[2 * p + 1]) for p in range(4)], axis=1).astype(BF16)


def _gqa_attention(qd, kvd):
    b, seq, _ = qd.shape
    u = kvd.shape[1]
    tq = ATTN_TILE
    return pl.pallas_call(
        _gqa_kernel,
        grid=(b, seq // tq),
        in_specs=[pl.BlockSpec((1, tq, 512), lambda i, n: (i, n, 0)),
                  pl.BlockSpec((1, u, 512), lambda i, n: (i, 0, 0))],
        out_specs=pl.BlockSpec((1, tq, 512), lambda i, n: (i, n, 0)),
        out_shape=jax.ShapeDtypeStruct((b, seq, 512), BF16),
        compiler_params=_cparams(("arbitrary", "arbitrary")),
        name="gqa_attn",
    )(qd, kvd)


def _post_attn_kernel(ya_ref, yb_ref, w_ref, c_ref, x_ref, mod_ref, g_ref, wr_ref, br_ref,
                      xmid_ref, h2_ref, slot_ref, wts_ref, slott_ref, gtab_ref, tot_ref, run_ref, *, ctx_first):
    first = (pl.program_id(0) == 0) & (pl.program_id(1) == 0)
    x_in = jnp.where(pl.program_id(1) == 0, c_ref[0], x_ref[0]) if ctx_first else x_ref[0]

    @pl.when(first)
    def _():
        run_ref[...] = jnp.zeros_like(run_ref)

    y = (jnp.dot(ya_ref[0], w_ref[0:512, :], preferred_element_type=F32)
         + jnp.dot(yb_ref[0], w_ref[512:1024, :], preferred_element_type=F32))
    x1 = x_in + mod_ref[0, 0, 2:3, :] * y
    xmid_ref[0] = x1
    h2 = _norm_mod(x1, g_ref[...], mod_ref[0, 0, 3:4, :], mod_ref[0, 0, 4:5, :])
    h_hi = h2.astype(BF16)
    h2_ref[...] = h_hi

    h_lo = (h2 - h_hi.astype(F32)).astype(BF16)
    lhs = jnp.concatenate([h_hi, h_hi, h_lo], axis=1)
    scores = jax.nn.sigmoid(_qk(wr_ref[...], lhs))
    tm = scores.shape[1]
    e_id = lax.broadcasted_iota(I32, (LANES, tm), 0).astype(F32)
    biased = jnp.where(e_id < N_EXPERTS, scores + br_ref[...], -jnp.inf)
    picked = jnp.zeros_like(scores)
    sel_idx, sel_val = [], []
    for _k in range(TOP_K):
        m = jnp.max(biased, axis=0, keepdims=True)
        i_k = jnp.min(jnp.where(biased == m, e_id, float(LANES)), axis=0, keepdims=True)
        hit = e_id == i_k
        sel_idx.append(i_k)
        sel_val.append(jnp.sum(jnp.where(hit, scores, 0.0), axis=0, keepdims=True))
        picked = jnp.where(hit, 1.0, picked)
        biased = jnp.where(hit, -jnp.inf, biased)
    total = sel_val[0]
    for v in sel_val[1:]:
        total = total + v
    cnt = jnp.sum(picked, axis=1, keepdims=True)
    ng = jnp.floor((cnt + float(GRAN - 1)) * (1.0 / GRAN))
    e_i = lax.broadcasted_iota(I32, (LANES, LANES), 0)
    e_j = lax.broadcasted_iota(I32, (LANES, LANES), 1)
    lower = jnp.where(e_j < e_i, 1.0, 0.0).astype(BF16)
    loffg = jnp.dot(lower, jnp.broadcast_to(ng, (LANES, LANES)).astype(BF16), preferred_element_type=F32)[:, 0:1]
    t_i = lax.broadcasted_iota(I32, (tm, tm), 0)
    t_j = lax.broadcasted_iota(I32, (tm, tm), 1)
    before = jnp.dot(picked.astype(BF16), jnp.where(t_i < t_j, 1.0, 0.0).astype(BF16), preferred_element_type=F32)
    base = before + float(GRAN) * loffg
    slot_t = jnp.zeros((LANES, tm), F32)
    wts_t = jnp.zeros((LANES, tm), F32)
    for k in range(TOP_K):
        slot_k = jnp.sum(jnp.where(e_id == sel_idx[k], base, 0.0), axis=0, keepdims=True)
        slot_t = jnp.where(e_id == float(k), slot_k, slot_t)
        wts_t = jnp.where(e_id == float(k), ROUTED_SCALE * sel_val[k] / total, wts_t)
    slott_ref[0] = slot_t[:KPAD, :].astype(I32)
    slot_ref[...] = slot_t.T[:, :KPAD].astype(I32)
    wts_ref[...] = wts_t.T[:, :KPAD]

    run = run_ref[...]
    end_c = loffg + ng
    val_c = run - loffg
    jj = lax.broadcasted_iota(I32, (LANES, NGRAN), 1).astype(F32)
    ee = lax.broadcasted_iota(I32, (LANES, NGRAN), 0).astype(F32)
    e_of_j = jnp.sum(jnp.where(end_c <= jj, 1.0, 0.0), axis=0, keepdims=True)
    dst = jnp.sum(jnp.where(ee == e_of_j, val_c + jj, 0.0), axis=0, keepdims=True)
    used = jnp.broadcast_to(jnp.sum(ng, axis=0, keepdims=True), (1, NGRAN))
    gtab_ref[0] = jnp.concatenate([e_of_j, dst, used, jnp.zeros((5, NGRAN), F32)], axis=0).astype(I32)
    new_run = run + ng
    run_ref[...] = new_run
    tot_ref[...] = new_run


def _post_attn(ya, yb, w_out, ctx, x, x_tile_off, ctx_first, mods, mod_sel, g_moe, w_router, b_router):
    b, n, _ = ya.shape
    d = D_MODEL
    tm = TOK_TILE
    nt = n // tm
    t_tot = b * n
    wr = jnp.zeros((d, LANES), F32).at[:, :N_EXPERTS].set(w_router.astype(F32))
    wr_hi = wr.astype(BF16)
    wr_lo = (wr - wr_hi.astype(F32)).astype(BF16)
    wr = jnp.concatenate([wr_hi, wr_lo, wr_hi], axis=0).T
    br = jnp.zeros((LANES, 1), F32).at[:N_EXPERTS, 0].set(b_router.astype(F32))
    flat = lambda c: pl.BlockSpec((tm, c), lambda i, t: (i * nt + t, 0))
    outs = pl.pallas_call(
        functools.partial(_post_attn_kernel, ctx_first=ctx_first),
        grid=(b, nt),
        in_specs=[pl.BlockSpec((1, tm, 512), lambda i, t: (i, t, 0)),
                  pl.BlockSpec((1, tm, 512), lambda i, t: (i, t, 0)),
                  pl.BlockSpec((d, d), lambda i, t: (0, 0)),
                  pl.BlockSpec((1, tm, d), lambda i, t: (i, 0, 0)),
                  pl.BlockSpec((1, tm, d), lambda i, t: (i, jnp.maximum(t + x_tile_off, 0), 0)),
                  pl.BlockSpec((1, 1, 6, d), lambda i, t: (i, mod_sel(t), 0, 0)),
                  pl.BlockSpec((1, d), lambda i, t: (0, 0)),
                  pl.BlockSpec((LANES, 3 * d), lambda i, t: (0, 0)),
                  pl.BlockSpec((LANES, 1), lambda i, t: (0, 0))],
        out_specs=[pl.BlockSpec((1, tm, d), lambda i, t: (i, t, 0)),
                   flat(d), flat(KPAD), flat(KPAD),
                   pl.BlockSpec((1, KPAD, tm), lambda i, t: (i * nt + t, 0, 0)),
                   pl.BlockSpec((1, 8, NGRAN), lambda i, t: (i * nt + t, 0, 0)),
                   pl.BlockSpec((LANES, 1), lambda i, t: (0, 0))],
        out_shape=[jax.ShapeDtypeStruct((b, n, d), F32),
                   jax.ShapeDtypeStruct((t_tot, d), BF16),
                   jax.ShapeDtypeStruct((t_tot, KPAD), I32),
                   jax.ShapeDtypeStruct((t_tot, KPAD), F32),
                   jax.ShapeDtypeStruct((b * nt, KPAD, tm), I32),
                   jax.ShapeDtypeStruct((b * nt, 8, NGRAN), I32),
                   jax.ShapeDtypeStruct((LANES, 1), F32)],
        scratch_shapes=[pltpu.VMEM((LANES, 1), F32)],
        compiler_params=_cparams(("arbitrary", "arbitrary")),
        name="post_attn_route",
    )(ya, yb, w_out.astype(BF16), ctx, x, mods, g_moe.reshape(1, d), wr, br)
    return outs


def _granule_copy(src, src_g, dst, dst_g, sem):
    return pltpu.make_async_copy(src.at[pl.ds(pl.multiple_of(src_g * GRAN, GRAN), GRAN), :],
                                 dst.at[pl.ds(pl.multiple_of(dst_g * GRAN, GRAN), GRAN), :], sem)


def _for_each(lo, hi, body):
    trips = lax.shift_right_logical(hi - lo, jnp.int32(ISSUE_UNROLL.bit_length() - 1))

    def block(b, carry):
        for u in range(ISSUE_UNROLL):
            body(lo + b * ISSUE_UNROLL + u)
        return carry

    def single(j, carry):
        body(j)
        return carry

    lax.fori_loop(0, trips, block, 0)
    lax.fori_loop(lo + trips * ISSUE_UNROLL, hi, single, 0)


def _drain(src, dst, sem, n):
    rows = WAIT_BATCH * GRAN
    batches = lax.shift_right_logical(n, jnp.int32(WAIT_BATCH.bit_length() - 1))

    def big(j, carry):
        pltpu.make_async_copy(src.at[pl.ds(0, rows), :], dst.at[pl.ds(0, rows), :], sem).wait()
        return carry

    def small(j, carry):
        _granule_copy(src, 0, dst, 0, sem).wait()
        return carry

    lax.fori_loop(0, batches, big, 0)
    lax.fori_loop(0, n - batches * WAIT_BATCH, small, 0)


def _dispatch_kernel(fill_ref, gtab_ref, h_ref, slott_ref, xs_ref, xloc, gprev, sems, *, nt):
    i = pl.program_id(0)
    cur = i % 2
    used = gtab_ref[0, 2, 0]
    slott = slott_ref[0]
    h = h_ref[...]
    tm = h.shape[0]
    half = h.shape[1] // 2
    def sort_rows(r0, n):
        rows = r0 + lax.broadcasted_iota(I32, (n, tm), 0)
        p = jnp.zeros((n, tm), F32)
        for k in range(TOP_K):
            p = jnp.where(rows == slott[k:k + 1, :], 1.0, p)
        xc = jnp.dot(p.astype(BF16), h, preferred_element_type=F32)
        lo = lax.bitcast_convert_type(xc[:, :half], U32)
        hi = lax.bitcast_convert_type(xc[:, half:], U32)
        xloc[cur, r0:r0 + n, :] = (hi & jnp.uint32(0xFFFF0000)) | (lo >> 16)

    def send(j):
        _granule_copy(xloc.at[cur], j, xs_ref, gtab_ref[0, 0, j], sems.at[cur]).start()

    for r0 in range(0, SORT_ROWS, CHUNK):
        if r0 + CHUNK <= ALWAYS_ROWS:
            sort_rows(r0, CHUNK)
            for j in range(r0 // GRAN, (r0 + CHUNK) // GRAN):
                send(j)
        else:
            pl.when(r0 < used * GRAN)(functools.partial(sort_rows, r0, CHUNK))
    _for_each(ALWAYS_ROWS // GRAN, used, send)

    @pl.when(i > 0)
    def _():
        _drain(xloc.at[1 - cur], xs_ref, sems.at[1 - cur], gprev[0])

    gprev[0] = used

    @pl.when(i == nt - 1)
    def _():
        _drain(xloc.at[cur], xs_ref, sems.at[cur], used)
        zeros = xloc.at[1 - cur]
        zeros[0:SLOT_TILE, :] = jnp.zeros((SLOT_TILE, xloc.shape[2]), U32)

        def fill_expert(e, count):
            lo, hi = fill_ref[0, e], fill_ref[1, e]
            lax.fori_loop(lo, hi, lambda g, c: (_granule_copy(zeros, 0, xs_ref, g, sems.at[0]).start(), c)[1], 0)
            return count + (hi - lo)

        _drain(zeros, xs_ref, sems.at[0], lax.fori_loop(0, N_EXPERTS, fill_expert, jnp.int32(0)))

        def tile_copy(t):
            return pltpu.make_async_copy(zeros.at[pl.ds(0, SLOT_TILE), :],
                                         xs_ref.at[pl.ds(pl.multiple_of(t * SLOT_TILE, SLOT_TILE), SLOT_TILE), :],
                                         sems.at[1])

        first_free, n_tiles = fill_ref[2, 0], xs_ref.shape[0] // SLOT_TILE
        lax.fori_loop(first_free, n_tiles, lambda t, c: (tile_copy(t).start(), c)[1], 0)
        lax.fori_loop(first_free, n_tiles, lambda t, c: (tile_copy(0).wait(), c)[1], 0)


def _dispatch(h2, slott, gtab, fill, n_slots):
    t_tot, d = h2.shape
    tm = TOK_TILE
    nt = t_tot // tm
    return pl.pallas_call(
        functools.partial(_dispatch_kernel, nt=nt),
        grid=(nt,),
        in_specs=[pl.BlockSpec(memory_space=pltpu.SMEM),
                  pl.BlockSpec((1, 8, NGRAN), lambda i: (i, 0, 0), memory_space=pltpu.SMEM),
                  pl.BlockSpec((tm, d), lambda i: (i, 0)),
                  pl.BlockSpec((1, KPAD, tm), lambda i: (i, 0, 0))],
        out_specs=pl.BlockSpec(memory_space=pl.ANY),
        out_shape=jax.ShapeDtypeStruct((n_slots, d // 2), U32),
        scratch_shapes=[pltpu.VMEM((2, SORT_ROWS, d // 2), U32), pltpu.SMEM((1,), I32),
                        pltpu.SemaphoreType.DMA((2,))],
        compiler_params=_cparams(("arbitrary",)),
        name="moe_dispatch",
    )(fill, gtab, h2, slott)


def _expert_kernel(te_ref, nused_ref, xs_ref, wg_ref, wu_ref, wd_ref, ys_ref, wgu_s, wd_s):
    i = pl.program_id(0)
    changed = (i == 0) | (te_ref[i] != te_ref[jnp.maximum(i - 1, 0)])

    @pl.when(changed)
    def _():
        wgu_s[:, 0:EXPERT_FF] = wg_ref[0].astype(BF16)
        wgu_s[:, EXPERT_FF:2 * EXPERT_FF] = wu_ref[0].astype(BF16)
        wd_s[...] = wd_ref[0].astype(BF16)

    @pl.when(i < nused_ref[0])
    def _():
        lo, hi = _unpack_bf16_pair(xs_ref[...])
        x = jnp.concatenate([lo.astype(BF16), hi.astype(BF16)], axis=1)
        gu = jnp.dot(x, wgu_s[...], preferred_element_type=F32)
        gate, up = gu[:, :EXPERT_FF], gu[:, EXPERT_FF:]
        act = gate * jax.nn.sigmoid(gate) * up
        y = jnp.dot(act.astype(BF16), wd_s[...], preferred_element_type=F32)
        half = y.shape[1] // 2
        ys_ref[...] = _pack_bf16_pair(y[:, :half], y[:, half:])

    @pl.when(i >= nused_ref[0])
    def _():
        ys_ref[...] = jnp.zeros_like(ys_ref)


def _expert_ffn(xs, tile_expert, n_used, w_gate, w_up, w_down):
    n_slots, w = xs.shape
    ts = SLOT_TILE
    d, f = w_gate.shape[1], w_gate.shape[2]
    grid_spec = pltpu.PrefetchScalarGridSpec(
        num_scalar_prefetch=2,
        grid=(n_slots // ts,),
        in_specs=[pl.BlockSpec((ts, w), lambda i, te, nu: (jnp.minimum(i, nu[0] - 1), 0)),
                  pl.BlockSpec((1, d, f), lambda i, te, nu: (te[i], 0, 0)),
                  pl.BlockSpec((1, d, f), lambda i, te, nu: (te[i], 0, 0)),
                  pl.BlockSpec((1, f, d), lambda i, te, nu: (te[i], 0, 0))],
        out_specs=pl.BlockSpec((ts, w), lambda i, te, nu: (i, 0)),
        scratch_shapes=[pltpu.VMEM((d, 2 * f), BF16), pltpu.VMEM((f, d), BF16)],
    )
    return pl.pallas_call(
        _expert_kernel,
        grid_spec=grid_spec,
        out_shape=jax.ShapeDtypeStruct((n_slots, w), U32),
        compiler_params=_cparams(("arbitrary",)),
        name="moe_experts",
    )(tile_expert, n_used, xs, w_gate, w_up, w_down)


def _combine_kernel(gtab_ref, gnext_ref, ys_ref, slot_ref, wts_ref, h_ref, x_ref, mod_ref, wsgu_ref,
                    wsd_ref, gf_ref, o_ref, yloc, acc_ref, sems, *, nt, final_norm):
    i = pl.program_id(0)
    cur = i % 2

    @pl.when(i == 0)
    def _():
        _for_each(0, gtab_ref[0, 2, 0], lambda j: _granule_copy(ys_ref, gtab_ref[0, 0, j], yloc, j, sems.at[0]).start())

    n_always = ALWAYS_ROWS // CHUNK
    inline = ALWAYS_ROWS // GRAN
    pieces = [inline * p // (n_always + 1) for p in range(n_always + 2)]

    def fetch_next(j):
        _granule_copy(ys_ref, gnext_ref[0, 0, j], yloc, (1 - cur) * NGRAN + j, sems.at[1 - cur]).start()

    def fetch_piece(p):
        for j in range(pieces[p], pieces[p + 1]):
            fetch_next(j)

    fetch_piece(0)
    h = h_ref[...]
    tm = h.shape[0]
    gu = jnp.dot(h, wsgu_ref[...], preferred_element_type=F32)
    gate, up = gu[:, :SHARED_FF], gu[:, SHARED_FF:]
    shared = jnp.dot((gate * jax.nn.sigmoid(gate) * up).astype(BF16), wsd_ref[...], preferred_element_type=F32)

    used = gtab_ref[0, 2, 0]
    _drain(ys_ref, yloc, sems.at[cur], used)
    row0 = cur * SORT_ROWS

    slot_b = [jnp.broadcast_to(slot_ref[:, k:k + 1], (tm, LANES)) for k in range(TOP_K)]
    wts_b = [jnp.broadcast_to(wts_ref[:, k:k + 1], (tm, LANES)) for k in range(TOP_K)]
    lane = lax.broadcasted_iota(I32, (tm, LANES), 1)
    w = yloc.shape[1]

    def weights(c):
        blocks = []
        for j in range(CHUNK // LANES):
            cols = lane + (c * CHUNK + j * LANES)
            pw = jnp.zeros((tm, LANES), F32)
            for k in range(TOP_K):
                pw = jnp.where(cols == slot_b[k], wts_b[k], pw)
            blocks.append(pw.astype(BF16))
        return jnp.concatenate(blocks, axis=1)

    def values(c, masked):
        packed = yloc[pl.ds(pl.multiple_of(row0 + c * CHUNK, CHUNK), CHUNK), :]
        if masked:
            rows = c * CHUNK + lax.broadcasted_iota(I32, (CHUNK, w), 0)
            packed = jnp.where(rows < used * GRAN, packed, jnp.uint32(0))
        lo, hi = _unpack_bf16_pair(packed)
        return jnp.concatenate([lo.astype(BF16), hi.astype(BF16)], axis=1)

    acc = shared
    for c in range(n_always):
        acc = acc + jnp.dot(weights(c), values(c, False), preferred_element_type=F32)
        fetch_piece(c + 1)
    acc_ref[...] = acc
    for c in range(n_always, SORT_ROWS // CHUNK):
        @pl.when(c * CHUNK < used * GRAN)
        def _():
            acc_ref[...] += jnp.dot(weights(c), values(c, True), preferred_element_type=F32)

    out = x_ref[...] + mod_ref[0, 0, 5:6, :] * acc_ref[...]
    if final_norm:
        out = _rms(out, gf_ref[...])
    o_ref[...] = out

    used_next = gnext_ref[0, 2, 0]
    _for_each(inline, used_next, fetch_next)

    @pl.when(i == nt - 1)
    def _():
        _drain(ys_ref, yloc, sems.at[1 - cur], used_next)


def _combine(ys, slot, wts, gtab, h2, xmid, mods, mod_map, ws_gate, ws_up, ws_down, g_final, final_norm):
    t_tot, d = h2.shape
    tm = TOK_TILE
    nt = t_tot // tm
    wsgu = jnp.concatenate([ws_gate, ws_up], axis=1).astype(BF16)
    return pl.pallas_call(
        functools.partial(_combine_kernel, nt=nt, final_norm=final_norm),
        grid=(nt,),
        in_specs=[pl.BlockSpec((1, 8, NGRAN), lambda i: (i, 0, 0), memory_space=pltpu.SMEM),
                  pl.BlockSpec((1, 8, NGRAN), lambda i: (jnp.minimum(i + 1, nt - 1), 0, 0), memory_space=pltpu.SMEM),
                  pl.BlockSpec(memory_space=pl.ANY),
                  pl.BlockSpec((tm, KPAD), lambda i: (i, 0)),
                  pl.BlockSpec((tm, KPAD), lambda i: (i, 0)),
                  pl.BlockSpec((tm, d), lambda i: (i, 0)),
                  pl.BlockSpec((tm, d), lambda i: (i, 0)),
                  pl.BlockSpec((1, 1, 6, d), mod_map),
                  pl.BlockSpec((d, 2 * SHARED_FF), lambda i: (0, 0)),
                  pl.BlockSpec((SHARED_FF, d), lambda i: (0, 0)),
                  pl.BlockSpec((1, d), lambda i: (0, 0))],
        out_specs=pl.BlockSpec((tm, d), lambda i: (i, 0)),
        out_shape=jax.ShapeDtypeStruct((t_tot, d), F32),
        scratch_shapes=[pltpu.VMEM((2 * SORT_ROWS, d // 2), U32), pltpu.VMEM((tm, d), F32),
                        pltpu.SemaphoreType.DMA((2,))],
        compiler_params=_cparams(("arbitrary",)),
        name="moe_combine",
    )(gtab, gtab, ys, slot, wts, h2, xmid.reshape(t_tot, d), mods, wsgu, ws_down.astype(BF16),
      g_final.reshape(1, d).astype(F32))


def _moe(h2, slot, wts, slott, gtab, tot, xmid, mods, mod_map, moe_w, g_final, final_norm):
    (w_gate, w_up, w_down, ws_gate, ws_up, ws_down) = moe_w
    t_tot = h2.shape[0]
    ts = SLOT_TILE
    gpt = ts // GRAN
    nt = t_tot // TOK_TILE
    max_rows = t_tot * TOP_K + nt * N_EXPERTS * (GRAN - 1)
    n_tiles = -(-max_rows // ts) + N_EXPERTS
    totg = tot[:N_EXPERTS, 0].astype(I32)
    tiles_e = (totg + gpt - 1) // gpt
    ends = jnp.cumsum(tiles_e)
    poffg = jnp.zeros((LANES,), I32).at[:N_EXPERTS].set((ends - tiles_e) * gpt)
    n_used = ends[-1:]
    tile_ids = jnp.minimum(jnp.arange(n_tiles, dtype=I32), n_used[0] - 1)
    tile_expert = jnp.sum((ends[None, :] <= tile_ids[:, None]).astype(I32), axis=1)
    tile_expert = jnp.minimum(tile_expert, N_EXPERTS - 1)
    region = jnp.sum(jnp.where(gtab[:, 0, :, None] == jnp.arange(N_EXPERTS, dtype=I32), poffg[:N_EXPERTS], 0), axis=-1)
    gtab = gtab.at[:, 0, :].set(region + gtab[:, 1, :])
    fill = jnp.zeros((8, LANES), I32)
    fill = fill.at[0, :N_EXPERTS].set(poffg[:N_EXPERTS] + totg).at[1, :N_EXPERTS].set(ends * gpt).at[2, 0].set(ends[-1])
    xs = _dispatch(h2, slott, gtab, fill, n_tiles * ts)
    ys = _expert_ffn(xs, tile_expert, n_used.astype(I32), w_gate, w_up, w_down)
    return _combine(ys, slot, wts, gtab, h2, xmid, mods, mod_map, ws_gate, ws_up, ws_down, g_final, final_norm)


def kernel(x, c, ctx, c_ctx, l0_w_ada, l0_b_ada, l0_g_mix, l0_w_in, l0_sink, l0_rpb, l0_w_out, l0_g_moe, l0_w_router, l0_b_router, l0_w_gate, l0_w_up, l0_w_down, l0_ws_gate, l0_ws_up, l0_ws_down, l1_w_ada, l1_b_ada, l1_g_mix, l1_w_in, l1_g_q_lora, l1_w_uq, l1_g_kv_lora, l1_w_ukv, l1_g_qn, l1_g_kn, l1_w_out, l1_g_moe, l1_w_router, l1_b_router, l1_w_gate, l1_w_up, l1_w_down, l1_ws_gate, l1_ws_up, l1_ws_down, g_final):
    b, seq, d = x.shape
    assert d == D_MODEL and ctx.shape[1] == CTX_LEN and seq % TOK_TILE == 0
    u = CTX_LEN + seq
    tiles_u = u // TOK_TILE
    tiles_s = seq // TOK_TILE
    ctx, x = ctx.astype(F32), x.astype(F32)

    cos_a, sin_a = _rope_tables(seq, HEAD_DIM, 0, HEAD_DIM, 1.0)
    cos_q, sin_q = _rope_tables(seq, C_ROPE, C_NOPE, LANES, float((C_NOPE + C_ROPE) ** -0.5))
    cos_k, sin_k = _rope_tables(seq, C_ROPE, 0, LANES, 1.0)

    mods0 = _mods(c, c_ctx, l0_w_ada, l0_b_ada)
    qa, kva, qb, kb, vb = _proj0(ctx, x, mods0, l0_g_mix, l0_w_in, cos_a, sin_a)
    ya = _window_attention(qa, kva, l0_sink)
    yb = _na_attention(qb, kb, vb, l0_rpb)
    xmid, h2, slot, wts, slott, gtab, tot = _post_attn(ya, yb, l0_w_out, ctx, x, -1, True, mods0,
                                                   lambda t: jnp.minimum(t, 1), l0_g_moe, l0_w_router, l0_b_router)
    xa = _moe(h2, slot, wts, slott, gtab, tot, xmid, mods0,
              lambda i: (i // tiles_u, jnp.minimum(i % tiles_u, 1), 0, 0),
              (l0_w_gate, l0_w_up, l0_w_down, l0_ws_gate, l0_ws_up, l0_ws_down), g_final, False).reshape(b, u, d)

    mods1 = _mods(c, c_ctx, l1_w_ada, l1_b_ada)
    qm, qd, km, vm, kvd = _proj1(xa, mods1, l1_g_mix, l1_w_in, l1_g_q_lora, l1_w_uq, l1_g_kv_lora, l1_w_ukv,
                                 l1_g_qn, l1_g_kn, (cos_a, sin_a, cos_q, sin_q, cos_k, sin_k))
    ym = _mla_attention(qm, km, vm)
    yd = _gqa_attention(qd, kvd)
    xmid, h2, slot, wts, slott, gtab, tot = _post_attn(ym, yd, l1_w_out, xa, xa, CTX_LEN // TOK_TILE, False, mods1,
                                                   lambda t: 1, l1_g_moe, l1_w_router, l1_b_router)
    out = _moe(h2, slot, wts, slott, gtab, tot, xmid, mods1, lambda i: (i // tiles_s, 1, 0, 0),
               (l1_w_gate, l1_w_up, l1_w_down, l1_ws_gate, l1_ws_up, l1_ws_down), g_final, True)
    return out.reshape(b, seq, d)
```

```python
import functools

import numpy as np
import jax
import jax.numpy as jnp
from jax import lax
from jax.experimental import pallas as pl
from jax.experimental.pallas import tpu as pltpu

F32 = jnp.float32
BF16 = jnp.bfloat16
U32 = jnp.uint32
I32 = jnp.int32

D_MODEL = 1024
CTX_LEN = 256
GRID_W = 64
HEAD_DIM = 64
ROPE_THETA = 10000.0
NORM_EPS = 1e-6
NEG_INF = -1e30
LOG2E = 1.4426950408889634
A_WINDOW = 128
NA_ROWS = 8
NA_COLS = 16
NA_QROWS = 4
NA_BAND = NA_ROWS + NA_QROWS - 1
C_Q_LORA = 384
C_KV_LORA = 256
C_NOPE = 64
C_ROPE = 32
C_V = 64
N_EXPERTS = 64
TOP_K = 6
EXPERT_FF = 256
SHARED_FF = 256
ROUTED_SCALE = 2.5

LANES = 128
TOK_TILE = 256
WINDOW_TILE = 128
WINDOW_STEP_TILES = 2
ATTN_TILE = 512
SLOT_TILE = 1024
KPAD = 8
GRAN = 8
CHUNK = 256
SORT_ROWS = 2048
NGRAN = SORT_ROWS // GRAN
ALWAYS_ROWS = TOK_TILE * TOP_K
WAIT_BATCH = 16
ISSUE_UNROLL = 4
VMEM_LIMIT = 48 * 1024 * 1024


def _cparams(sem):
    return pltpu.CompilerParams(dimension_semantics=sem, vmem_limit_bytes=VMEM_LIMIT)


def _rms(x, g):
    return x * lax.rsqrt(jnp.mean(x * x, axis=-1, keepdims=True) + NORM_EPS) * g


def _norm_mod(x, g, shift, scale):
    return _rms(x, g) * (1.0 + scale) + shift


def _rope128(x, cos, sin):
    lane = lax.broadcasted_iota(I32, x.shape, 1)
    swapped = jnp.where(lane % 2 == 0, pltpu.roll(x, LANES - 1, 1), pltpu.roll(x, 1, 1))
    return x * cos + swapped * sin


def _group_sumsq(x, ones_bd):
    sq = x * x
    hi = sq.astype(BF16)
    lo = (sq - hi.astype(F32)).astype(BF16)
    return (jnp.dot(hi, ones_bd, preferred_element_type=F32) + jnp.dot(lo, ones_bd, preferred_element_type=F32))


def _head_norm(x, ones_bd, g):
    return x * lax.rsqrt(_group_sumsq(x, ones_bd) * (1.0 / HEAD_DIM) + NORM_EPS) * g


def _pack_bf16_pair(lo, hi):
    lo_bits = lax.bitcast_convert_type(lo.astype(BF16).astype(F32), U32)
    hi_bits = lax.bitcast_convert_type(hi.astype(BF16).astype(F32), U32)
    return (hi_bits & jnp.uint32(0xFFFF0000)) | (lo_bits >> 16)


def _unpack_bf16_pair(u):
    lo = lax.bitcast_convert_type(u << 16, F32)
    hi = lax.bitcast_convert_type(u & jnp.uint32(0xFFFF0000), F32)
    return lo, hi


def _keep_half(q128, half):
    lane = lax.broadcasted_iota(I32, q128.shape, 1)
    keep = (lane < HEAD_DIM) if half == 0 else (lane >= HEAD_DIM)
    return jnp.where(keep, q128, jnp.zeros_like(q128))


def _merge_halves(o_even, o_odd):
    lane = lax.broadcasted_iota(I32, o_even.shape, 1)
    return jnp.where(lane < HEAD_DIM, o_even, o_odd)


def _qk(q, k):
    return lax.dot_general(q, k, (((1,), (1,)), ((), ())), preferred_element_type=F32)


def _softmax_pv(s, v, sinks=None):
    m = jnp.max(s, axis=-1, keepdims=True)
    if sinks is not None:
        r = s.shape[0] // len(sinks)
        m = jnp.concatenate([jnp.maximum(m[i * r:(i + 1) * r], sinks[i]) for i in range(len(sinks))], axis=0)
    e = jnp.exp2(s - m)
    den = jnp.sum(e, axis=-1, keepdims=True)
    if sinks is not None:
        den = den + jnp.concatenate([jnp.exp2(sinks[i] - m[i * r:(i + 1) * r]) for i in range(len(sinks))], axis=0)
    return jnp.dot(e.astype(BF16), v, preferred_element_type=F32) / den


def _ada_kernel(c_ref, w_ref, b_ref, o_ref):
    c = c_ref[...]
    a = c * jax.nn.sigmoid(c)
    o_ref[...] = jnp.dot(a, w_ref[...], precision=lax.Precision.HIGHEST, preferred_element_type=F32) + b_ref[...]


def _ada(cond, w_ada, b_ada):
    n, d = cond.shape
    nout = w_ada.shape[1]
    bn = 512
    return pl.pallas_call(
        _ada_kernel,
        grid=(nout // bn,),
        in_specs=[pl.BlockSpec((n, d), lambda j: (0, 0)),
                  pl.BlockSpec((d, bn), lambda j: (0, j)),
                  pl.BlockSpec((1, bn), lambda j: (0, j))],
        out_specs=pl.BlockSpec((n, bn), lambda j: (0, j)),
        out_shape=jax.ShapeDtypeStruct((n, nout), F32),
        compiler_params=_cparams(("arbitrary",)),
        name="ada",
    )(cond, w_ada, b_ada.reshape(1, nout))


def _mods(c, c_ctx, w_ada, b_ada):
    b = c.shape[0]
    rows = ((b + 1 + 7) // 8) * 8
    cond = jnp.zeros((rows, D_MODEL), F32).at[:b].set(c).at[b].set(c_ctx)
    out = _ada(cond, w_ada, b_ada)
    lat = out[:b].reshape(b, 1, 6, D_MODEL)
    cx = jnp.broadcast_to(out[b].reshape(1, 1, 6, D_MODEL), (b, 1, 6, D_MODEL))
    return jnp.concatenate([cx, lat], axis=1)


def _axial_angles(n_tokens, rot_dim):
    t = np.arange(n_tokens)
    row = (t // GRID_W).astype(np.float32)
    col = (t % GRID_W).astype(np.float32)
    n_axis = rot_dim // 4
    inv_freq = (np.float32(ROPE_THETA) ** (-np.arange(n_axis, dtype=np.float32) / n_axis)).astype(np.float32)
    return jnp.concatenate([jnp.asarray(row[:, None] * inv_freq), jnp.asarray(col[:, None] * inv_freq)], axis=-1)


def _rope_tables(seq, rot_dim, lane_start, period, scale):
    ang = _axial_angles(seq, rot_dim)
    cos = jnp.repeat(jnp.cos(ang), 2, axis=-1)
    sin = jnp.repeat(jnp.sin(ang), 2, axis=-1) * jnp.tile(jnp.asarray([-1.0, 1.0], F32), rot_dim // 2)
    cos_p = jnp.ones((seq, period), F32).at[:, lane_start:lane_start + rot_dim].set(cos)
    sin_p = jnp.zeros((seq, period), F32).at[:, lane_start:lane_start + rot_dim].set(sin)
    cos_f = jnp.concatenate([jnp.ones((CTX_LEN, period), F32), cos_p], axis=0)
    sin_f = jnp.concatenate([jnp.zeros((CTX_LEN, period), F32), sin_p], axis=0)
    reps = LANES // period
    return jnp.tile(cos_f, (1, reps)) * scale, jnp.tile(sin_f, (1, reps)) * scale


def _na_band_start(block, rows):
    return np.clip(block * NA_QROWS - NA_ROWS // 2, 0, rows - NA_BAND)


def _na_bias_table(rpb, rows):
    h = rpb.shape[0]
    qc = np.arange(GRID_W)
    kc = np.arange(GRID_W)
    cstart = np.clip(qc - NA_COLS // 2, 0, GRID_W - NA_COLS)
    col_ok = (kc[None, :] >= cstart[:, None]) & (kc[None, :] < cstart[:, None] + NA_COLS)
    dc = np.clip(kc[None, :] - qc[:, None] + NA_COLS - 1, 0, 2 * NA_COLS - 2)
    col_sel = np.eye(2 * NA_COLS - 1, dtype=np.float32)[dc]
    variants, ids = [], [None]
    for blk in range(rows // NA_QROWS):
        rs0 = _na_band_start(blk, rows)
        r = blk * NA_QROWS + np.arange(NA_QROWS)
        rs = np.clip(r - NA_ROWS // 2, 0, rows - NA_ROWS)
        krow = rs0 + np.arange(NA_BAND)
        row_ok = (krow[None, :] >= rs[:, None]) & (krow[None, :] < rs[:, None] + NA_ROWS)
        dr = np.clip(krow[None, :] - r[:, None] + NA_ROWS - 1, 0, 2 * NA_ROWS - 2)
        key = (row_ok.tobytes(), dr.tobytes())
        if key not in [v[0] for v in variants]:
            variants.append((key, row_ok, dr))
        ids.append([v[0] for v in variants].index(key))
    ids[0] = len(variants)
    tabs = []
    for _, row_ok, dr in variants:
        row_sel = np.eye(2 * NA_ROWS - 1, dtype=np.float32)[dr]
        b = jnp.einsum('hrc,ijr,qkc->hiqjk', rpb.astype(F32), jnp.asarray(row_sel), jnp.asarray(col_sel),
                       precision=lax.Precision.HIGHEST)
        ok = row_ok[:, None, :, None] & col_ok[None, :, None, :]
        b = jnp.where(jnp.asarray(ok)[None], b * LOG2E, NEG_INF)
        tabs.append(b.reshape(h, NA_QROWS * GRID_W, NA_BAND * GRID_W))
    tabs.append(jnp.full((h, NA_QROWS * GRID_W, NA_BAND * GRID_W), NEG_INF, F32))
    loc = jnp.stack(tabs, axis=0)
    tab = jnp.concatenate([jnp.zeros(loc.shape[:3] + (CTX_LEN,), F32), loc], axis=-1)
    return tab, jnp.asarray(ids, I32)


def _dup_heads(w, n_heads):
    d = w.shape[0]
    w = w.reshape(d, n_heads, 1, HEAD_DIM)
    return jnp.broadcast_to(w, (d, n_heads, 2, HEAD_DIM)).reshape(d, n_heads * 2 * HEAD_DIM)


def _proj0_kernel(c_ref, x_ref, mod_ref, g_ref, w_ref, cos_ref, sin_ref, qa_ref, kva_ref, qb_ref, kb_ref, vb_ref):
    xt = jnp.where(pl.program_id(1) == 0, c_ref[0], x_ref[0])
    h = _norm_mod(xt, g_ref[...], mod_ref[0, 0, 0:1, :], mod_ref[0, 0, 1:2, :])
    r = jnp.dot(h.astype(BF16), w_ref[...], preferred_element_type=F32)
    cos, sin = cos_ref[...], sin_ref[...]
    roped = [_rope128(r[:, i * LANES:(i + 1) * LANES], cos, sin) for i in range(6)]
    qa_ref[0] = (jnp.concatenate(roped[0:4], axis=1) * LOG2E).astype(BF16)
    kva_ref[0] = jnp.concatenate(roped[4:6] + [r[:, 768:1024]], axis=1).astype(BF16)
    qb_ref[0] = (r[:, 1024:1536] * LOG2E).astype(BF16)
    kb_ref[0] = r[:, 1536:2048].astype(BF16)
    vb_ref[0] = r[:, 2048:2560].astype(BF16)


def _proj0(ctx, x, mods, g_mix, w_in, cos, sin):
    b, seq, d = x.shape
    u = CTX_LEN + seq
    tm = TOK_TILE
    s = 1.0 / 8.0
    w = jnp.concatenate([w_in[:, 0:512] * s, _dup_heads(w_in[:, 512:640], 2), _dup_heads(w_in[:, 640:768], 2),
                         w_in[:, 768:1280] * s, w_in[:, 1280:1792], w_in[:, 1792:2304]], axis=1).astype(BF16)
    nw = w.shape[1]
    tok = lambda n: pl.BlockSpec((1, tm, n), lambda i, t: (i, t, 0))
    return pl.pallas_call(
        _proj0_kernel,
        grid=(b, u // tm),
        in_specs=[pl.BlockSpec((1, tm, d), lambda i, t: (i, 0, 0)),
                  pl.BlockSpec((1, tm, d), lambda i, t: (i, jnp.maximum(t - 1, 0), 0)),
                  pl.BlockSpec((1, 1, 6, d), lambda i, t: (i, jnp.minimum(t, 1), 0, 0)),
                  pl.BlockSpec((1, d), lambda i, t: (0, 0)),
                  pl.BlockSpec((d, nw), lambda i, t: (0, 0)),
                  pl.BlockSpec((tm, LANES), lambda i, t: (t, 0)),
                  pl.BlockSpec((tm, LANES), lambda i, t: (t, 0))],
        out_specs=[tok(512)] * 5,
        out_shape=[jax.ShapeDtypeStruct((b, u, 512), BF16)] * 5,
        compiler_params=_cparams(("arbitrary", "arbitrary")),
        name="proj0",
    )(ctx, x, mods, g_mix.reshape(1, d), w, cos, sin)


def _window_starts(n_tiles, tq, span, u):
    return np.clip(np.arange(n_tiles) * tq - A_WINDOW, CTX_LEN, u - span)


def _window_mask_table(u, tq, span):
    n_tiles = u // tq
    starts = _window_starts(n_tiles, tq, span, u)
    variants, ids = [], []
    for n in range(n_tiles):
        qpos = n * tq + np.arange(tq)[:, None]
        kpos = starts[n] + np.arange(span)[None, :]
        ok = (kpos >= CTX_LEN) & (np.abs(qpos - kpos) <= A_WINDOW) & (qpos >= CTX_LEN)
        key = ok.tobytes()
        if key not in [v[0] for v in variants]:
            variants.append((key, ok))
        ids.append([v[0] for v in variants].index(key))
    local = np.stack([np.where(v[1], 0.0, NEG_INF) for v in variants]).astype(np.float32)
    tab = np.concatenate([np.zeros(local.shape[:2] + (CTX_LEN,), np.float32), local], axis=-1)
    return jnp.asarray(tab), jnp.asarray(ids, I32)


def _window_kernel(var_ref, sink_ref, q_ref, kv_ref, mask_ref, o_ref, *, tq, span):
    u = kv_ref.shape[1]
    for sub in range(WINDOW_STEP_TILES):
        n = pl.program_id(1) * WINDOW_STEP_TILES + sub
        ls = pl.multiple_of(jnp.clip(n * tq - A_WINDOW, CTX_LEN, u - span), A_WINDOW)
        mask = mask_ref[var_ref[n]]
        kv_all = jnp.concatenate([kv_ref[0, 0:CTX_LEN, :], kv_ref[0, pl.ds(ls, span), :]], axis=0)
        outs = [None] * 8
        for g in range(2):
            q4 = jnp.concatenate(
                [_keep_half(q_ref[0, sub * tq:(sub + 1) * tq, ((4 * g + hh) // 2) * LANES:((4 * g + hh) // 2 + 1) * LANES],
                            hh % 2) for hh in range(4)], axis=0)
            s = _qk(q4, kv_all[:, g * LANES:(g + 1) * LANES])
            s = jnp.concatenate([s[hh * tq:(hh + 1) * tq] + mask for hh in range(4)], axis=0)
            o = _softmax_pv(s, kv_all[:, 256 + g * LANES:256 + (g + 1) * LANES],
                            [sink_ref[4 * g + hh] for hh in range(4)])
            for hh in range(4):
                outs[4 * g + hh] = o[hh * tq:(hh + 1) * tq]
        o_ref[0, sub * tq:(sub + 1) * tq, :] = jnp.concatenate(
            [_merge_halves(outs[2 * p], outs[2 * p + 1]) for p in range(4)], axis=1).astype(BF16)


def _window_attention(qa, kva, sink):
    b, u, _ = qa.shape
    tq = WINDOW_TILE
    span = tq + 2 * A_WINDOW
    assert CTX_LEN % tq == 0 and (u - span) % A_WINDOW == 0
    mask_tab, variant = _window_mask_table(u, tq, span)
    rows = tq * WINDOW_STEP_TILES
    assert u % rows == 0
    return pl.pallas_call(
        functools.partial(_window_kernel, tq=tq, span=span),
        grid=(b, u // rows),
        in_specs=[pl.BlockSpec(memory_space=pltpu.SMEM),
                  pl.BlockSpec(memory_space=pltpu.SMEM),
                  pl.BlockSpec((1, rows, 512), lambda i, n: (i, n, 0)),
                  pl.BlockSpec((1, u, 512), lambda i, n: (i, 0, 0)),
                  pl.BlockSpec(mask_tab.shape, lambda i, n: (0, 0, 0))],
        out_specs=pl.BlockSpec((1, rows, 512), lambda i, n: (i, n, 0)),
        out_shape=jax.ShapeDtypeStruct((b, u, 512), BF16),
        compiler_params=_cparams(("arbitrary", "arbitrary")),
        name="window_attn",
    )(variant, sink.astype(F32) * LOG2E, qa, kva, mask_tab)


def _na_kernel(var_ref, q_ref, k_ref, v_ref, bias_ref, o_ref, *, rows):
    j = pl.program_id(1)
    rs0 = jnp.clip((j - 1) * NA_QROWS - NA_ROWS // 2, 0, rows - NA_BAND)
    band = NA_BAND * GRID_W
    start = pl.multiple_of(CTX_LEN + rs0 * GRID_W, GRID_W)
    k_all = jnp.concatenate([k_ref[0, 0:CTX_LEN, :], k_ref[0, pl.ds(start, band), :]], axis=0)
    v_all = jnp.concatenate([v_ref[0, 0:CTX_LEN, :], v_ref[0, pl.ds(start, band), :]], axis=0)
    tq = q_ref.shape[1]
    outs = []
    for p in range(4):
        q128 = q_ref[0, :, p * LANES:(p + 1) * LANES]
        q2 = jnp.concatenate([_keep_half(q128, 0), _keep_half(q128, 1)], axis=0)
        s = _qk(q2, k_all[:, p * LANES:(p + 1) * LANES])
        s = s + jnp.concatenate([bias_ref[0, 2 * p], bias_ref[0, 2 * p + 1]], axis=0)
        o = _softmax_pv(s, v_all[:, p * LANES:(p + 1) * LANES])
        outs.append(_merge_halves(o[:tq], o[tq:]))
    o_ref[0] = jnp.concatenate(outs, axis=1).astype(BF16)


def _na_attention(qb, kb, vb, rpb):
    b, u, _ = qb.shape
    rows = (u - CTX_LEN) // GRID_W
    assert rows >= NA_BAND and rows % NA_QROWS == 0
    bias_tab, variant = _na_bias_table(rpb, rows)
    tq = NA_QROWS * GRID_W
    assert tq == CTX_LEN
    nk = bias_tab.shape[-1]
    grid_spec = pltpu.PrefetchScalarGridSpec(
        num_scalar_prefetch=1,
        grid=(b, u // tq),
        in_specs=[pl.BlockSpec((1, tq, 512), lambda i, j, var: (i, j, 0)),
                  pl.BlockSpec((1, u, 512), lambda i, j, var: (i, 0, 0)),
                  pl.BlockSpec((1, u, 512), lambda i, j, var: (i, 0, 0)),
                  pl.BlockSpec((1, 8, tq, nk), lambda i, j, var: (var[j], 0, 0, 0))],
        out_specs=pl.BlockSpec((1, tq, 512), lambda i, j, var: (i, j, 0)),
    )
    return pl.pallas_call(
        functools.partial(_na_kernel, rows=rows),
        grid_spec=grid_spec,
        out_shape=jax.ShapeDtypeStruct((b, u, 512), BF16),
        compiler_params=_cparams(("arbitrary", "arbitrary")),
        name="na_attn",
    )(variant, qb, kb, vb, bias_tab)


def _proj1_kernel(x_ref, mod_ref, g_ref, w_ref, gq_ref, wuq_ref, gkv_ref, wkv_ref, gqn_ref, gkn_ref, ones_ref,
                  cosa_ref, sina_ref, cosq_ref, sinq_ref, cosk_ref, sink_ref,
                  qm_ref, qd_ref, km_ref, vm_ref, kvd_ref):
    h = _norm_mod(x_ref[0], g_ref[...], mod_ref[0, 0, 0:1, :], mod_ref[0, 0, 1:2, :])
    r = jnp.dot(h.astype(BF16), w_ref[...], preferred_element_type=F32)
    ones_bd = ones_ref[...]
    cosa, sina = cosa_ref[...], sina_ref[...]
    cq = _rms(r[:, 0:C_Q_LORA], gq_ref[...])
    qm = jnp.dot(cq.astype(BF16), wuq_ref[...], preferred_element_type=F32)
    cosq, sinq = cosq_ref[...], sinq_ref[...]
    qm_ref[0] = jnp.concatenate([_rope128(qm[:, i * LANES:(i + 1) * LANES], cosq, sinq) for i in range(8)],
                                axis=1).astype(BF16)
    gqn = gqn_ref[...]
    qd_ref[0] = jnp.concatenate(
        [_rope128(_head_norm(r[:, 384 + i * LANES:384 + (i + 1) * LANES], ones_bd, gqn), cosa, sina)
         for i in range(4)], axis=1).astype(BF16)
    ckv = _rms(r[:, 896:1152], gkv_ref[...])
    kr = _rope128(r[:, 1152:1280], cosk_ref[...], sink_ref[...])
    kv = jnp.dot(jnp.concatenate([ckv, kr], axis=1).astype(BF16), wkv_ref[...], preferred_element_type=F32)
    km_ref[0] = kv[:, 0:1024].astype(BF16)
    vm_ref[0] = kv[:, 1024:1536].astype(BF16)
    gkn = gkn_ref[...]
    kd = [_rope128(_head_norm(r[:, 1280 + i * LANES:1280 + (i + 1) * LANES], ones_bd, gkn), cosa, sina)
          for i in range(2)]
    kvd_ref[0] = jnp.concatenate(kd + [r[:, 1536:1792]], axis=1).astype(BF16)


def _proj1(xa, mods, g_mix, w_in, g_q_lora, w_uq, g_kv_lora, w_ukv, g_qn, g_kn, tabs):
    b, u, d = xa.shape
    tm = TOK_TILE
    zpad = jnp.zeros((d, LANES - C_ROPE), F32)
    w = jnp.concatenate([w_in[:, 0:896], w_in[:, 896:1152], w_in[:, 1152:1184], zpad,
                         _dup_heads(w_in[:, 1184:1312], 2), _dup_heads(w_in[:, 1312:1440], 2)], axis=1).astype(BF16)
    nw = w.shape[1]
    wuq = w_uq.reshape(C_Q_LORA, 8, C_NOPE + C_ROPE)
    wuq = jnp.concatenate([wuq, jnp.zeros((C_Q_LORA, 8, LANES - C_NOPE - C_ROPE), F32)], axis=-1)
    wuq = wuq.reshape(C_Q_LORA, 8 * LANES).astype(BF16)
    wukv = w_ukv.reshape(C_KV_LORA, 8, C_NOPE + C_V)
    wk = jnp.concatenate([wukv[:, :, :C_NOPE], jnp.zeros((C_KV_LORA, 8, LANES - C_NOPE), F32)], axis=-1)
    wk = wk.reshape(C_KV_LORA, 8 * LANES)
    wv = wukv[:, :, C_NOPE:].reshape(C_KV_LORA, 8 * C_V)
    place = np.zeros((LANES, 8, LANES), np.float32)
    for j in range(C_ROPE):
        place[j, :, C_NOPE + j] = 1.0
    place = jnp.asarray(place.reshape(LANES, 8 * LANES))
    wkv = jnp.concatenate([jnp.concatenate([wk, wv], axis=1),
                           jnp.concatenate([place, jnp.zeros((LANES, 8 * C_V), F32)], axis=1)], axis=0).astype(BF16)
    ones_bd = jnp.asarray(np.kron(np.eye(2, dtype=np.float32), np.ones((HEAD_DIM, HEAD_DIM), np.float32))).astype(BF16)
    gqn = jnp.tile(g_qn.astype(F32) * (LOG2E / 8.0), 2).reshape(1, LANES)
    gkn = jnp.tile(g_kn.astype(F32), 2).reshape(1, LANES)
    cosa, sina, cosq, sinq, cosk, sink = tabs
    tok = lambda n: pl.BlockSpec((1, tm, n), lambda i, t: (i, t, 0))
    lat = lambda n: pl.BlockSpec((1, tm, n), lambda i, t: (i, jnp.maximum(t - 1, 0), 0))
    whole = lambda a: pl.BlockSpec(a.shape, lambda i, t: (0,) * a.ndim)
    tab = pl.BlockSpec((tm, LANES), lambda i, t: (t, 0))
    small = [g_mix.reshape(1, d), w, g_q_lora.reshape(1, -1).astype(F32), wuq, g_kv_lora.reshape(1, -1).astype(F32),
             wkv, gqn, gkn, ones_bd]
    return pl.pallas_call(
        _proj1_kernel,
        grid=(b, u // tm),
        in_specs=[tok(d), pl.BlockSpec((1, 1, 6, d), lambda i, t: (i, jnp.minimum(t, 1), 0, 0))]
                 + [whole(a) for a in small] + [tab] * 6,
        out_specs=[lat(1024), lat(512), tok(1024), tok(512), tok(512)],
        out_shape=[jax.ShapeDtypeStruct((b, u - CTX_LEN, n), BF16) for n in (1024, 512)]
                  + [jax.ShapeDtypeStruct((b, u, n), BF16) for n in (1024, 512, 512)],
        compiler_params=_cparams(("arbitrary", "arbitrary")),
        name="proj1",
    )(xa, mods, *small, cosa, sina, cosq, sinq, cosk, sink)


def _mla_kernel(q_ref, k_ref, v_ref, o_ref):
    outs = []
    for h in range(8):
        s = _qk(q_ref[0, :, h * LANES:(h + 1) * LANES], k_ref[0, :, h * LANES:(h + 1) * LANES])
        outs.append(_softmax_pv(s, v_ref[0, :, (h // 2) * LANES:(h // 2 + 1) * LANES]))
    o_ref[0] = jnp.concatenate([_merge_halves(outs[2 * p], outs[2 * p + 1]) for p in range(4)], axis=1).astype(BF16)


def _mla_attention(qm, km, vm):
    b, seq, _ = qm.shape
    u = km.shape[1]
    tq = ATTN_TILE
    return pl.pallas_call(
        _mla_kernel,
        grid=(b, seq // tq),
        in_specs=[pl.BlockSpec((1, tq, 1024), lambda i, n: (i, n, 0)),
                  pl.BlockSpec((1, u, 1024), lambda i, n: (i, 0, 0)),
                  pl.BlockSpec((1, u, 512), lambda i, n: (i, 0, 0))],
        out_specs=pl.BlockSpec((1, tq, 512), lambda i, n: (i, n, 0)),
        out_shape=jax.ShapeDtypeStruct((b, seq, 512), BF16),
        compiler_params=_cparams(("arbitrary", "arbitrary")),
        name="mla_attn",
    )(qm, km, vm)


def _gqa_kernel(q_ref, kv_ref, o_ref):
    outs = [None] * 8
    for g in range(2):
        k = kv_ref[0, :, g * LANES:(g + 1) * LANES]
        v = kv_ref[0, :, 256 + g * LANES:256 + (g + 1) * LANES]
        for hh in range(4):
            h = 4 * g + hh
            q = _keep_half(q_ref[0, :, (h // 2) * LANES:(h // 2 + 1) * LANES], h % 2)
            outs[h] = _softmax_pv(_qk(q, k), v)
    o_ref[0] = jnp.concatenate([_merge_halves(outs[2 * p], outs[2 * p + 1]) for p in range(4)], axis=1).astype(BF16)


def _gqa_attention(qd, kvd):
    b, seq, _ = qd.shape
    u = kvd.shape[1]
    tq = ATTN_TILE
    return pl.pallas_call(
        _gqa_kernel,
        grid=(b, seq // tq),
        in_specs=[pl.BlockSpec((1, tq, 512), lambda i, n: (i, n, 0)),
                  pl.BlockSpec((1, u, 512), lambda i, n: (i, 0, 0))],
        out_specs=pl.BlockSpec((1, tq, 512), lambda i, n: (i, n, 0)),
        out_shape=jax.ShapeDtypeStruct((b, seq, 512), BF16),
        compiler_params=_cparams(("arbitrary", "arbitrary")),
        name="gqa_attn",
    )(qd, kvd)


def _post_attn_kernel(ya_ref, yb_ref, w_ref, c_ref, x_ref, mod_ref, g_ref, wr_ref, br_ref,
                      xmid_ref, h2_ref, slot_ref, wts_ref, slott_ref, gtab_ref, tot_ref, run_ref, *, ctx_first):
    first = (pl.program_id(0) == 0) & (pl.program_id(1) == 0)
    x_in = jnp.where(pl.program_id(1) == 0, c_ref[0], x_ref[0]) if ctx_first else x_ref[0]

    @pl.when(first)
    def _():
        run_ref[...] = jnp.zeros_like(run_ref)

    y = (jnp.dot(ya_ref[0], w_ref[0:512, :], preferred_element_type=F32)
         + jnp.dot(yb_ref[0], w_ref[512:1024, :], preferred_element_type=F32))
    x1 = x_in + mod_ref[0, 0, 2:3, :] * y
    xmid_ref[0] = x1
    h2 = _norm_mod(x1, g_ref[...], mod_ref[0, 0, 3:4, :], mod_ref[0, 0, 4:5, :])
    h_hi = h2.astype(BF16)
    h2_ref[...] = h_hi

    h_lo = (h2 - h_hi.astype(F32)).astype(BF16)
    lhs = jnp.concatenate([h_hi, h_hi, h_lo], axis=1)
    scores = jax.nn.sigmoid(_qk(wr_ref[...], lhs))
    tm = scores.shape[1]
    e_id = lax.broadcasted_iota(I32, (LANES, tm), 0).astype(F32)
    biased = jnp.where(e_id < N_EXPERTS, scores + br_ref[...], -jnp.inf)
    picked = jnp.zeros_like(scores)
    sel_idx, sel_val = [], []
    for _k in range(TOP_K):
        m = jnp.max(biased, axis=0, keepdims=True)
        i_k = jnp.min(jnp.where(biased == m, e_id, float(LANES)), axis=0, keepdims=True)
        hit = e_id == i_k
        sel_idx.append(i_k)
        sel_val.append(jnp.sum(jnp.where(hit, scores, 0.0), axis=0, keepdims=True))
        picked = jnp.where(hit, 1.0, picked)
        biased = jnp.where(hit, -jnp.inf, biased)
    total = sel_val[0]
    for v in sel_val[1:]:
        total = total + v
    cnt = jnp.sum(picked, axis=1, keepdims=True)
    ng = jnp.floor((cnt + float(GRAN - 1)) * (1.0 / GRAN))
    e_i = lax.broadcasted_iota(I32, (LANES, LANES), 0)
    e_j = lax.broadcasted_iota(I32, (LANES, LANES), 1)
    lower = jnp.where(e_j < e_i, 1.0, 0.0).astype(BF16)
    loffg = jnp.dot(lower, jnp.broadcast_to(ng, (LANES, LANES)).astype(BF16), preferred_element_type=F32)[:, 0:1]
    t_i = lax.broadcasted_iota(I32, (tm, tm), 0)
    t_j = lax.broadcasted_iota(I32, (tm, tm), 1)
    before = jnp.dot(picked.astype(BF16), jnp.where(t_i < t_j, 1.0, 0.0).astype(BF16), preferred_element_type=F32)
    base = before + float(GRAN) * loffg
    slot_t = jnp.zeros((LANES, tm), F32)
    wts_t = jnp.zeros((LANES, tm), F32)
    for k in range(TOP_K):
        slot_k = jnp.sum(jnp.where(e_id == sel_idx[k], base, 0.0), axis=0, keepdims=True)
        slot_t = jnp.where(e_id == float(k), slot_k, slot_t)
        wts_t = jnp.where(e_id == float(k), ROUTED_SCALE * sel_val[k] / total, wts_t)
    slott_ref[0] = slot_t[:KPAD, :].astype(I32)
    slot_ref[...] = slot_t.T[:, :KPAD].astype(I32)
    wts_ref[...] = wts_t.T[:, :KPAD]

    run = run_ref[...]
    end_c = loffg + ng
    val_c = run - loffg
    jj = lax.broadcasted_iota(I32, (LANES, NGRAN), 1).astype(F32)
    ee = lax.broadcasted_iota(I32, (LANES, NGRAN), 0).astype(F32)
    e_of_j = jnp.sum(jnp.where(end_c <= jj, 1.0, 0.0), axis=0, keepdims=True)
    dst = jnp.sum(jnp.where(ee == e_of_j, val_c + jj, 0.0), axis=0, keepdims=True)
    used = jnp.broadcast_to(jnp.sum(ng, axis=0, keepdims=True), (1, NGRAN))
    gtab_ref[0] = jnp.concatenate([e_of_j, dst, used, jnp.zeros((5, NGRAN), F32)], axis=0).astype(I32)
    new_run = run + ng
    run_ref[...] = new_run
    tot_ref[...] = new_run


def _post_attn(ya, yb, w_out, ctx, x, x_tile_off, ctx_first, mods, mod_sel, g_moe, w_router, b_router):
    b, n, _ = ya.shape
    d = D_MODEL
    tm = TOK_TILE
    nt = n // tm
    t_tot = b * n
    wr = jnp.zeros((d, LANES), F32).at[:, :N_EXPERTS].set(w_router.astype(F32))
    wr_hi = wr.astype(BF16)
    wr_lo = (wr - wr_hi.astype(F32)).astype(BF16)
    wr = jnp.concatenate([wr_hi, wr_lo, wr_hi], axis=0).T
    br = jnp.zeros((LANES, 1), F32).at[:N_EXPERTS, 0].set(b_router.astype(F32))
    flat = lambda c: pl.BlockSpec((tm, c), lambda i, t: (i * nt + t, 0))
    outs = pl.pallas_call(
        functools.partial(_post_attn_kernel, ctx_first=ctx_first),
        grid=(b, nt),
        in_specs=[pl.BlockSpec((1, tm, 512), lambda i, t: (i, t, 0)),
                  pl.BlockSpec((1, tm, 512), lambda i, t: (i, t, 0)),
                  pl.BlockSpec((d, d), lambda i, t: (0, 0)),
                  pl.BlockSpec((1, tm, d), lambda i, t: (i, 0, 0)),
                  pl.BlockSpec((1, tm, d), lambda i, t: (i, jnp.maximum(t + x_tile_off, 0), 0)),
                  pl.BlockSpec((1, 1, 6, d), lambda i, t: (i, mod_sel(t), 0, 0)),
                  pl.BlockSpec((1, d), lambda i, t: (0, 0)),
                  pl.BlockSpec((LANES, 3 * d), lambda i, t: (0, 0)),
                  pl.BlockSpec((LANES, 1), lambda i, t: (0, 0))],
        out_specs=[pl.BlockSpec((1, tm, d), lambda i, t: (i, t, 0)),
                   flat(d), flat(KPAD), flat(KPAD),
                   pl.BlockSpec((1, KPAD, tm), lambda i, t: (i * nt + t, 0, 0)),
                   pl.BlockSpec((1, 8, NGRAN), lambda i, t: (i * nt + t, 0, 0)),
                   pl.BlockSpec((LANES, 1), lambda i, t: (0, 0))],
        out_shape=[jax.ShapeDtypeStruct((b, n, d), F32),
                   jax.ShapeDtypeStruct((t_tot, d), BF16),
                   jax.ShapeDtypeStruct((t_tot, KPAD), I32),
                   jax.ShapeDtypeStruct((t_tot, KPAD), F32),
                   jax.ShapeDtypeStruct((b * nt, KPAD, tm), I32),
                   jax.ShapeDtypeStruct((b * nt, 8, NGRAN), I32),
                   jax.ShapeDtypeStruct((LANES, 1), F32)],
        scratch_shapes=[pltpu.VMEM((LANES, 1), F32)],
        compiler_params=_cparams(("arbitrary", "arbitrary")),
        name="post_attn_route",
    )(ya, yb, w_out.astype(BF16), ctx, x, mods, g_moe.reshape(1, d), wr, br)
    return outs


def _granule_copy(src, src_g, dst, dst_g, sem):
    return pltpu.make_async_copy(src.at[pl.ds(pl.multiple_of(src_g * GRAN, GRAN), GRAN), :],
                                 dst.at[pl.ds(pl.multiple_of(dst_g * GRAN, GRAN), GRAN), :], sem)


def _for_each(lo, hi, body):
    trips = lax.shift_right_logical(hi - lo, jnp.int32(ISSUE_UNROLL.bit_length() - 1))

    def block(b, carry):
        for u in range(ISSUE_UNROLL):
            body(lo + b * ISSUE_UNROLL + u)
        return carry

    def single(j, carry):
        body(j)
        return carry

    lax.fori_loop(0, trips, block, 0)
    lax.fori_loop(lo + trips * ISSUE_UNROLL, hi, single, 0)


def _drain(src, dst, sem, n):
    rows = WAIT_BATCH * GRAN
    batches = lax.shift_right_logical(n, jnp.int32(WAIT_BATCH.bit_length() - 1))

    def big(j, carry):
        pltpu.make_async_copy(src.at[pl.ds(0, rows), :], dst.at[pl.ds(0, rows), :], sem).wait()
        return carry

    def small(j, carry):
        _granule_copy(src, 0, dst, 0, sem).wait()
        return carry

    lax.fori_loop(0, batches, big, 0)
    lax.fori_loop(0, n - batches * WAIT_BATCH, small, 0)


def _dispatch_kernel(fill_ref, gtab_ref, h_ref, slott_ref, xs_ref, xloc, gprev, sems, *, nt):
    i = pl.program_id(0)
    cur = i % 2
    used = gtab_ref[0, 2, 0]
    slott = slott_ref[0]
    h = h_ref[...]
    tm = h.shape[0]
    half = h.shape[1] // 2
    def sort_rows(r0, n):
        rows = r0 + lax.broadcasted_iota(I32, (n, tm), 0)
        p = jnp.zeros((n, tm), F32)
        for k in range(TOP_K):
            p = jnp.where(rows == slott[k:k + 1, :], 1.0, p)
        xc = jnp.dot(p.astype(BF16), h, preferred_element_type=F32)
        lo = lax.bitcast_convert_type(xc[:, :half], U32)
        hi = lax.bitcast_convert_type(xc[:, half:], U32)
        xloc[cur, r0:r0 + n, :] = (hi & jnp.uint32(0xFFFF0000)) | (lo >> 16)

    def send(j):
        _granule_copy(xloc.at[cur], j, xs_ref, gtab_ref[0, 0, j], sems.at[cur]).start()

    for r0 in range(0, SORT_ROWS, CHUNK):
        if r0 + CHUNK <= ALWAYS_ROWS:
            sort_rows(r0, CHUNK)
            for j in range(r0 // GRAN, (r0 + CHUNK) // GRAN):
                send(j)
        else:
            pl.when(r0 < used * GRAN)(functools.partial(sort_rows, r0, CHUNK))
    _for_each(ALWAYS_ROWS // GRAN, used, send)

    @pl.when(i > 0)
    def _():
        _drain(xloc.at[1 - cur], xs_ref, sems.at[1 - cur], gprev[0])

    gprev[0] = used

    @pl.when(i == nt - 1)
    def _():
        _drain(xloc.at[cur], xs_ref, sems.at[cur], used)
        zeros = xloc.at[1 - cur]
        zeros[0:SLOT_TILE, :] = jnp.zeros((SLOT_TILE, xloc.shape[2]), U32)

        def fill_expert(e, count):
            lo, hi = fill_ref[0, e], fill_ref[1, e]
            lax.fori_loop(lo, hi, lambda g, c: (_granule_copy(zeros, 0, xs_ref, g, sems.at[0]).start(), c)[1], 0)
            return count + (hi - lo)

        _drain(zeros, xs_ref, sems.at[0], lax.fori_loop(0, N_EXPERTS, fill_expert, jnp.int32(0)))

        def tile_copy(t):
            return pltpu.make_async_copy(zeros.at[pl.ds(0, SLOT_TILE), :],
                                         xs_ref.at[pl.ds(pl.multiple_of(t * SLOT_TILE, SLOT_TILE), SLOT_TILE), :],
                                         sems.at[1])

        first_free, n_tiles = fill_ref[2, 0], xs_ref.shape[0] // SLOT_TILE
        lax.fori_loop(first_free, n_tiles, lambda t, c: (tile_copy(t).start(), c)[1], 0)
        lax.fori_loop(first_free, n_tiles, lambda t, c: (tile_copy(0).wait(), c)[1], 0)


def _dispatch(h2, slott, gtab, fill, n_slots):
    t_tot, d = h2.shape
    tm = TOK_TILE
    nt = t_tot // tm
    return pl.pallas_call(
        functools.partial(_dispatch_kernel, nt=nt),
        grid=(nt,),
        in_specs=[pl.BlockSpec(memory_space=pltpu.SMEM),
                  pl.BlockSpec((1, 8, NGRAN), lambda i: (i, 0, 0), memory_space=pltpu.SMEM),
                  pl.BlockSpec((tm, d), lambda i: (i, 0)),
                  pl.BlockSpec((1, KPAD, tm), lambda i: (i, 0, 0))],
        out_specs=pl.BlockSpec(memory_space=pl.ANY),
        out_shape=jax.ShapeDtypeStruct((n_slots, d // 2), U32),
        scratch_shapes=[pltpu.VMEM((2, SORT_ROWS, d // 2), U32), pltpu.SMEM((1,), I32),
                        pltpu.SemaphoreType.DMA((2,))],
        compiler_params=_cparams(("arbitrary",)),
        name="moe_dispatch",
    )(fill, gtab, h2, slott)


def _expert_kernel(te_ref, nused_ref, xs_ref, wg_ref, wu_ref, wd_ref, ys_ref, wgu_s, wd_s):
    i = pl.program_id(0)
    changed = (i == 0) | (te_ref[i] != te_ref[jnp.maximum(i - 1, 0)])

    @pl.when(changed)
    def _():
        wgu_s[:, 0:EXPERT_FF] = wg_ref[0].astype(BF16)
        wgu_s[:, EXPERT_FF:2 * EXPERT_FF] = wu_ref[0].astype(BF16)
        wd_s[...] = wd_ref[0].astype(BF16)

    @pl.when(i < nused_ref[0])
    def _():
        lo, hi = _unpack_bf16_pair(xs_ref[...])
        x = jnp.concatenate([lo.astype(BF16), hi.astype(BF16)], axis=1)
        gu = jnp.dot(x, wgu_s[...], preferred_element_type=F32)
        gate, up = gu[:, :EXPERT_FF], gu[:, EXPERT_FF:]
        act = gate * jax.nn.sigmoid(gate) * up
        y = jnp.dot(act.astype(BF16), wd_s[...], preferred_element_type=F32)
        half = y.shape[1] // 2
        ys_ref[...] = _pack_bf16_pair(y[:, :half], y[:, half:])

    @pl.when(i >= nused_ref[0])
    def _():
        ys_ref[...] = jnp.zeros_like(ys_ref)


def _expert_ffn(xs, tile_expert, n_used, w_gate, w_up, w_down):
    n_slots, w = xs.shape
    ts = SLOT_TILE
    d, f = w_gate.shape[1], w_gate.shape[2]
    grid_spec = pltpu.PrefetchScalarGridSpec(
        num_scalar_prefetch=2,
        grid=(n_slots // ts,),
        in_specs=[pl.BlockSpec((ts, w), lambda i, te, nu: (jnp.minimum(i, nu[0] - 1), 0)),
                  pl.BlockSpec((1, d, f), lambda i, te, nu: (te[i], 0, 0)),
                  pl.BlockSpec((1, d, f), lambda i, te, nu: (te[i], 0, 0)),
                  pl.BlockSpec((1, f, d), lambda i, te, nu: (te[i], 0, 0))],
        out_specs=pl.BlockSpec((ts, w), lambda i, te, nu: (i, 0)),
        scratch_shapes=[pltpu.VMEM((d, 2 * f), BF16), pltpu.VMEM((f, d), BF16)],
    )
    return pl.pallas_call(
        _expert_kernel,
        grid_spec=grid_spec,
        out_shape=jax.ShapeDtypeStruct((n_slots, w), U32),
        compiler_params=_cparams(("arbitrary",)),
        name="moe_experts",
    )(tile_expert, n_used, xs, w_gate, w_up, w_down)


def _combine_kernel(gtab_ref, gnext_ref, ys_ref, slot_ref, wts_ref, h_ref, x_ref, mod_ref, wsgu_ref,
                    wsd_ref, gf_ref, o_ref, yloc, acc_ref, sems, *, nt, final_norm):
    i = pl.program_id(0)
    cur = i % 2

    @pl.when(i == 0)
    def _():
        _for_each(0, gtab_ref[0, 2, 0], lambda j: _granule_copy(ys_ref, gtab_ref[0, 0, j], yloc, j, sems.at[0]).start())

    n_always = ALWAYS_ROWS // CHUNK
    inline = ALWAYS_ROWS // GRAN
    pieces = [inline * p // (n_always + 1) for p in range(n_always + 2)]

    def fetch_next(j):
        _granule_copy(ys_ref, gnext_ref[0, 0, j], yloc, (1 - cur) * NGRAN + j, sems.at[1 - cur]).start()

    def fetch_piece(p):
        for j in range(pieces[p], pieces[p + 1]):
            fetch_next(j)

    fetch_piece(0)
    h = h_ref[...]
    tm = h.shape[0]
    gu = jnp.dot(h, wsgu_ref[...], preferred_element_type=F32)
    gate, up = gu[:, :SHARED_FF], gu[:, SHARED_FF:]
    shared = jnp.dot((gate * jax.nn.sigmoid(gate) * up).astype(BF16), wsd_ref[...], preferred_element_type=F32)

    used = gtab_ref[0, 2, 0]
    _drain(ys_ref, yloc, sems.at[cur], used)
    row0 = cur * SORT_ROWS

    slot_b = [jnp.broadcast_to(slot_ref[:, k:k + 1], (tm, LANES)) for k in range(TOP_K)]
    wts_b = [jnp.broadcast_to(wts_ref[:, k:k + 1], (tm, LANES)) for k in range(TOP_K)]
    lane = lax.broadcasted_iota(I32, (tm, LANES), 1)
    w = yloc.shape[1]

    def weights(c):
        blocks = []
        for j in range(CHUNK // LANES):
            cols = lane + (c * CHUNK + j * LANES)
            pw = jnp.zeros((tm, LANES), F32)
            for k in range(TOP_K):
                pw = jnp.where(cols == slot_b[k], wts_b[k], pw)
            blocks.append(pw.astype(BF16))
        return jnp.concatenate(blocks, axis=1)

    def values(c, masked):
        packed = yloc[pl.ds(pl.multiple_of(row0 + c * CHUNK, CHUNK), CHUNK), :]
        if masked:
            rows = c * CHUNK + lax.broadcasted_iota(I32, (CHUNK, w), 0)
            packed = jnp.where(rows < used * GRAN, packed, jnp.uint32(0))
        lo, hi = _unpack_bf16_pair(packed)
        return jnp.concatenate([lo.astype(BF16), hi.astype(BF16)], axis=1)

    acc = shared
    for c in range(n_always):
        acc = acc + jnp.dot(weights(c), values(c, False), preferred_element_type=F32)
        fetch_piece(c + 1)
    acc_ref[...] = acc
    for c in range(n_always, SORT_ROWS // CHUNK):
        @pl.when(c * CHUNK < used * GRAN)
        def _():
            acc_ref[...] += jnp.dot(weights(c), values(c, True), preferred_element_type=F32)

    out = x_ref[...] + mod_ref[0, 0, 5:6, :] * acc_ref[...]
    if final_norm:
        out = _rms(out, gf_ref[...])
    o_ref[...] = out

    used_next = gnext_ref[0, 2, 0]
    _for_each(inline, used_next, fetch_next)

    @pl.when(i == nt - 1)
    def _():
        _drain(ys_ref, yloc, sems.at[1 - cur], used_next)


def _combine(ys, slot, wts, gtab, h2, xmid, mods, mod_map, ws_gate, ws_up, ws_down, g_final, final_norm):
    t_tot, d = h2.shape
    tm = TOK_TILE
    nt = t_tot // tm
    wsgu = jnp.concatenate([ws_gate, ws_up], axis=1).astype(BF16)
    return pl.pallas_call(
        functools.partial(_combine_kernel, nt=nt, final_norm=final_norm),
        grid=(nt,),
        in_specs=[pl.BlockSpec((1, 8, NGRAN), lambda i: (i, 0, 0), memory_space=pltpu.SMEM),
                  pl.BlockSpec((1, 8, NGRAN), lambda i: (jnp.minimum(i + 1, nt - 1), 0, 0), memory_space=pltpu.SMEM),
                  pl.BlockSpec(memory_space=pl.ANY),
                  pl.BlockSpec((tm, KPAD), lambda i: (i, 0)),
                  pl.BlockSpec((tm, KPAD), lambda i: (i, 0)),
                  pl.BlockSpec((tm, d), lambda i: (i, 0)),
                  pl.BlockSpec((tm, d), lambda i: (i, 0)),
                  pl.BlockSpec((1, 1, 6, d), mod_map),
                  pl.BlockSpec((d, 2 * SHARED_FF), lambda i: (0, 0)),
                  pl.BlockSpec((SHARED_FF, d), lambda i: (0, 0)),
                  pl.BlockSpec((1, d), lambda i: (0, 0))],
        out_specs=pl.BlockSpec((tm, d), lambda i: (i, 0)),
        out_shape=jax.ShapeDtypeStruct((t_tot, d), F32),
        scratch_shapes=[pltpu.VMEM((2 * SORT_ROWS, d // 2), U32), pltpu.VMEM((tm, d), F32),
                        pltpu.SemaphoreType.DMA((2,))],
        compiler_params=_cparams(("arbitrary",)),
        name="moe_combine",
    )(gtab, gtab, ys, slot, wts, h2, xmid.reshape(t_tot, d), mods, wsgu, ws_down.astype(BF16),
      g_final.reshape(1, d).astype(F32))


def _moe(h2, slot, wts, slott, gtab, tot, xmid, mods, mod_map, moe_w, g_final, final_norm):
    (w_gate, w_up, w_down, ws_gate, ws_up, ws_down) = moe_w
    t_tot = h2.shape[0]
    ts = SLOT_TILE
    gpt = ts // GRAN
    nt = t_tot // TOK_TILE
    max_rows = t_tot * TOP_K + nt * N_EXPERTS * (GRAN - 1)
    n_tiles = -(-max_rows // ts) + N_EXPERTS
    totg = tot[:N_EXPERTS, 0].astype(I32)
    tiles_e = (totg + gpt - 1) // gpt
    ends = jnp.cumsum(tiles_e)
    poffg = jnp.zeros((LANES,), I32).at[:N_EXPERTS].set((ends - tiles_e) * gpt)
    n_used = ends[-1:]
    tile_ids = jnp.minimum(jnp.arange(n_tiles, dtype=I32), n_used[0] - 1)
    tile_expert = jnp.sum((ends[None, :] <= tile_ids[:, None]).astype(I32), axis=1)
    tile_expert = jnp.minimum(tile_expert, N_EXPERTS - 1)
    region = jnp.sum(jnp.where(gtab[:, 0, :, None] == jnp.arange(N_EXPERTS, dtype=I32), poffg[:N_EXPERTS], 0), axis=-1)
    gtab = gtab.at[:, 0, :].set(region + gtab[:, 1, :])
    fill = jnp.zeros((8, LANES), I32)
    fill = fill.at[0, :N_EXPERTS].set(poffg[:N_EXPERTS] + totg).at[1, :N_EXPERTS].set(ends * gpt).at[2, 0].set(ends[-1])
    xs = _dispatch(h2, slott, gtab, fill, n_tiles * ts)
    ys = _expert_ffn(xs, tile_expert, n_used.astype(I32), w_gate, w_up, w_down)
    return _combine(ys, slot, wts, gtab, h2, xmid, mods, mod_map, ws_gate, ws_up, ws_down, g_final, final_norm)


def kernel(x, c, ctx, c_ctx, l0_w_ada, l0_b_ada, l0_g_mix, l0_w_in, l0_sink, l0_rpb, l0_w_out, l0_g_moe, l0_w_router, l0_b_router, l0_w_gate, l0_w_up, l0_w_down, l0_ws_gate, l0_ws_up, l0_ws_down, l1_w_ada, l1_b_ada, l1_g_mix, l1_w_in, l1_g_q_lora, l1_w_uq, l1_g_kv_lora, l1_w_ukv, l1_g_qn, l1_g_kn, l1_w_out, l1_g_moe, l1_w_router, l1_b_router, l1_w_gate, l1_w_up, l1_w_down, l1_ws_gate, l1_ws_up, l1_ws_down, g_final):
    b, seq, d = x.shape
    assert d == D_MODEL and ctx.shape[1] == CTX_LEN and seq % TOK_TILE == 0
    u = CTX_LEN + seq
    tiles_u = u // TOK_TILE
    tiles_s = seq // TOK_TILE
    ctx, x = ctx.astype(F32), x.astype(F32)

    cos_a, sin_a = _rope_tables(seq, HEAD_DIM, 0, HEAD_DIM, 1.0)
    cos_q, sin_q = _rope_tables(seq, C_ROPE, C_NOPE, LANES, LOG2E * float((C_NOPE + C_ROPE) ** -0.5))
    cos_k, sin_k = _rope_tables(seq, C_ROPE, 0, LANES, 1.0)

    mods0 = _mods(c, c_ctx, l0_w_ada, l0_b_ada)
    qa, kva, qb, kb, vb = _proj0(ctx, x, mods0, l0_g_mix, l0_w_in, cos_a, sin_a)
    ya = _window_attention(qa, kva, l0_sink)
    yb = _na_attention(qb, kb, vb, l0_rpb)
    xmid, h2, slot, wts, slott, gtab, tot = _post_attn(ya, yb, l0_w_out, ctx, x, -1, True, mods0,
                                                   lambda t: jnp.minimum(t, 1), l0_g_moe, l0_w_router, l0_b_router)
    xa = _moe(h2, slot, wts, slott, gtab, tot, xmid, mods0,
              lambda i: (i // tiles_u, jnp.minimum(i % tiles_u, 1), 0, 0),
              (l0_w_gate, l0_w_up, l0_w_down, l0_ws_gate, l0_ws_up, l0_ws_down), g_final, False).reshape(b, u, d)

    mods1 = _mods(c, c_ctx, l1_w_ada, l1_b_ada)
    qm, qd, km, vm, kvd = _proj1(xa, mods1, l1_g_mix, l1_w_in, l1_g_q_lora, l1_w_uq, l1_g_kv_lora, l1_w_ukv,
                                 l1_g_qn, l1_g_kn, (cos_a, sin_a, cos_q, sin_q, cos_k, sin_k))
    ym = _mla_attention(qm, km, vm)
    yd = _gqa_attention(qd, kvd)
    xmid, h2, slot, wts, slott, gtab, tot = _post_attn(ym, yd, l1_w_out, xa, xa, CTX_LEN // TOK_TILE, False, mods1,
                                                   lambda t: 1, l1_g_moe, l1_w_router, l1_b_router)
    out = _moe(h2, slot, wts, slott, gtab, tot, xmid, mods1, lambda i: (i // tiles_s, 1, 0, 0),
               (l1_w_gate, l1_w_up, l1_w_down, l1_ws_gate, l1_ws_up, l1_ws_down), g_final, True)
    return out.reshape(b, seq, d)
```

```python
import functools

import numpy as np
import jax
import jax.numpy as jnp
from jax import lax
from jax.experimental import pallas as pl
from jax.experimental.pallas import tpu as pltpu

F32 = jnp.float32
BF16 = jnp.bfloat16
I32 = jnp.int32

D_MODEL = 1024
CTX_LEN = 256
GRID_W = 64
HEAD_DIM = 64
ROPE_THETA = 10000.0
NORM_EPS = 1e-6
NEG_INF = -1e30
LOG2E = 1.4426950408889634
A_WINDOW = 128
NA_ROWS = 8
NA_COLS = 16
NA_QROWS = 4
NA_BAND = NA_ROWS + NA_QROWS - 1
C_Q_LORA = 384
C_KV_LORA = 256
C_NOPE = 64
C_ROPE = 32
C_V = 64
N_EXPERTS = 64
TOP_K = 6
EXPERT_FF = 256
SHARED_FF = 256
ROUTED_SCALE = 2.5

LANES = 128
TOK_TILE = 256
WINDOW_TILE = 128
WINDOW_STEP_TILES = 2
ATTN_TILE = 512
SLOT_TILE = 1024
KPAD = 8
GRAN = 8
CHUNK = 256
SORT_ROWS = 2048
NGRAN = SORT_ROWS // GRAN
ALWAYS_ROWS = TOK_TILE * TOP_K
WAIT_BATCH = 16
ISSUE_UNROLL = 4
VMEM_LIMIT = 48 * 1024 * 1024


def _cparams(sem):
    return pltpu.CompilerParams(dimension_semantics=sem, vmem_limit_bytes=VMEM_LIMIT)


def _rms(x, g):
    return x * lax.rsqrt(jnp.mean(x * x, axis=-1, keepdims=True) + NORM_EPS) * g


def _norm_mod(x, g, shift, scale):
    return _rms(x, g) * (1.0 + scale) + shift


def _rope128(x, cos, sin):
    lane = lax.broadcasted_iota(I32, x.shape, 1)
    swapped = jnp.where(lane % 2 == 0, pltpu.roll(x, LANES - 1, 1), pltpu.roll(x, 1, 1))
    return x * cos + swapped * sin


def _group_sumsq(x, ones_bd):
    sq = x * x
    hi = sq.astype(BF16)
    lo = (sq - hi.astype(F32)).astype(BF16)
    return (jnp.dot(hi, ones_bd, preferred_element_type=F32) + jnp.dot(lo, ones_bd, preferred_element_type=F32))


def _head_norm(x, ones_bd, g):
    return x * lax.rsqrt(_group_sumsq(x, ones_bd) * (1.0 / HEAD_DIM) + NORM_EPS) * g


def _keep_half(q128, half):
    lane = lax.broadcasted_iota(I32, q128.shape, 1)
    keep = (lane < HEAD_DIM) if half == 0 else (lane >= HEAD_DIM)
    return jnp.where(keep, q128, jnp.zeros_like(q128))


def _merge_halves(o_even, o_odd):
    lane = lax.broadcasted_iota(I32, o_even.shape, 1)
    return jnp.where(lane < HEAD_DIM, o_even, o_odd)


def _qk(q, k):
    return lax.dot_general(q, k, (((1,), (1,)), ((), ())), preferred_element_type=F32)


def _softmax_pv(s, v, sinks=None):
    m = jnp.max(s, axis=-1, keepdims=True)
    if sinks is not None:
        r = s.shape[0] // len(sinks)
        m = jnp.concatenate([jnp.maximum(m[i * r:(i + 1) * r], sinks[i]) for i in range(len(sinks))], axis=0)
    e = jnp.exp2(s - m)
    den = jnp.sum(e, axis=-1, keepdims=True)
    if sinks is not None:
        den = den + jnp.concatenate([jnp.exp2(sinks[i] - m[i * r:(i + 1) * r]) for i in range(len(sinks))], axis=0)
    return jnp.dot(e.astype(BF16), v, preferred_element_type=F32) / den


def _ada_kernel(c_ref, w_ref, b_ref, o_ref):
    c = c_ref[...]
    a = c * jax.nn.sigmoid(c)
    o_ref[...] = jnp.dot(a, w_ref[...], precision=lax.Precision.HIGHEST, preferred_element_type=F32) + b_ref[...]


def _ada(cond, w_ada, b_ada):
    n, d = cond.shape
    nout = w_ada.shape[1]
    bn = 512
    return pl.pallas_call(
        _ada_kernel,
        grid=(nout // bn,),
        in_specs=[pl.BlockSpec((n, d), lambda j: (0, 0)),
                  pl.BlockSpec((d, bn), lambda j: (0, j)),
                  pl.BlockSpec((1, bn), lambda j: (0, j))],
        out_specs=pl.BlockSpec((n, bn), lambda j: (0, j)),
        out_shape=jax.ShapeDtypeStruct((n, nout), F32),
        compiler_params=_cparams(("arbitrary",)),
        name="ada",
    )(cond, w_ada, b_ada.reshape(1, nout))


def _mods(c, c_ctx, w_ada, b_ada):
    b = c.shape[0]
    rows = ((b + 1 + 7) // 8) * 8
    cond = jnp.zeros((rows, D_MODEL), F32).at[:b].set(c).at[b].set(c_ctx)
    out = _ada(cond, w_ada, b_ada)
    lat = out[:b].reshape(b, 1, 6, D_MODEL)
    cx = jnp.broadcast_to(out[b].reshape(1, 1, 6, D_MODEL), (b, 1, 6, D_MODEL))
    return jnp.concatenate([cx, lat], axis=1)


def _axial_angles(n_tokens, rot_dim):
    t = np.arange(n_tokens)
    row = (t // GRID_W).astype(np.float32)
    col = (t % GRID_W).astype(np.float32)
    n_axis = rot_dim // 4
    inv_freq = (np.float32(ROPE_THETA) ** (-np.arange(n_axis, dtype=np.float32) / n_axis)).astype(np.float32)
    return jnp.concatenate([jnp.asarray(row[:, None] * inv_freq), jnp.asarray(col[:, None] * inv_freq)], axis=-1)


def _rope_tables(seq, rot_dim, lane_start, period, scale):
    ang = _axial_angles(seq, rot_dim)
    cos = jnp.repeat(jnp.cos(ang), 2, axis=-1)
    sin = jnp.repeat(jnp.sin(ang), 2, axis=-1) * jnp.tile(jnp.asarray([-1.0, 1.0], F32), rot_dim // 2)
    cos_p = jnp.ones((seq, period), F32).at[:, lane_start:lane_start + rot_dim].set(cos)
    sin_p = jnp.zeros((seq, period), F32).at[:, lane_start:lane_start + rot_dim].set(sin)
    cos_f = jnp.concatenate([jnp.ones((CTX_LEN, period), F32), cos_p], axis=0)
    sin_f = jnp.concatenate([jnp.zeros((CTX_LEN, period), F32), sin_p], axis=0)
    reps = LANES // period
    return jnp.tile(cos_f, (1, reps)) * scale, jnp.tile(sin_f, (1, reps)) * scale


def _na_band_start(block, rows):
    return np.clip(block * NA_QROWS - NA_ROWS // 2, 0, rows - NA_BAND)


def _na_bias_table(rpb, rows):
    h = rpb.shape[0]
    qc = np.arange(GRID_W)
    kc = np.arange(GRID_W)
    cstart = np.clip(qc - NA_COLS // 2, 0, GRID_W - NA_COLS)
    col_ok = (kc[None, :] >= cstart[:, None]) & (kc[None, :] < cstart[:, None] + NA_COLS)
    dc = np.clip(kc[None, :] - qc[:, None] + NA_COLS - 1, 0, 2 * NA_COLS - 2)
    col_sel = np.eye(2 * NA_COLS - 1, dtype=np.float32)[dc]
    variants, ids = [], [None]
    for blk in range(rows // NA_QROWS):
        rs0 = _na_band_start(blk, rows)
        r = blk * NA_QROWS + np.arange(NA_QROWS)
        rs = np.clip(r - NA_ROWS // 2, 0, rows - NA_ROWS)
        krow = rs0 + np.arange(NA_BAND)
        row_ok = (krow[None, :] >= rs[:, None]) & (krow[None, :] < rs[:, None] + NA_ROWS)
        dr = np.clip(krow[None, :] - r[:, None] + NA_ROWS - 1, 0, 2 * NA_ROWS - 2)
        key = (row_ok.tobytes(), dr.tobytes())
        if key not in [v[0] for v in variants]:
            variants.append((key, row_ok, dr))
        ids.append([v[0] for v in variants].index(key))
    ids[0] = len(variants)
    tabs = []
    for _, row_ok, dr in variants:
        row_sel = np.eye(2 * NA_ROWS - 1, dtype=np.float32)[dr]
        b = jnp.einsum('hrc,ijr,qkc->hiqjk', rpb.astype(F32), jnp.asarray(row_sel), jnp.asarray(col_sel),
                       precision=lax.Precision.HIGHEST)
        ok = row_ok[:, None, :, None] & col_ok[None, :, None, :]
        b = jnp.where(jnp.asarray(ok)[None], b * LOG2E, NEG_INF)
        tabs.append(b.reshape(h, NA_QROWS * GRID_W, NA_BAND * GRID_W))
    tabs.append(jnp.full((h, NA_QROWS * GRID_W, NA_BAND * GRID_W), NEG_INF, F32))
    loc = jnp.stack(tabs, axis=0)
    tab = jnp.concatenate([jnp.zeros(loc.shape[:3] + (CTX_LEN,), F32), loc], axis=-1)
    return tab, jnp.asarray(ids, I32)


def _dup_heads(w, n_heads):
    d = w.shape[0]
    w = w.reshape(d, n_heads, 1, HEAD_DIM)
    return jnp.broadcast_to(w, (d, n_heads, 2, HEAD_DIM)).reshape(d, n_heads * 2 * HEAD_DIM)


def _proj0_kernel(c_ref, x_ref, mod_ref, g_ref, w_ref, cos_ref, sin_ref, qa_ref, kva_ref, qb_ref, kb_ref, vb_ref):
    xt = jnp.where(pl.program_id(1) == 0, c_ref[0], x_ref[0])
    h = _norm_mod(xt, g_ref[...], mod_ref[0, 0, 0:1, :], mod_ref[0, 0, 1:2, :])
    r = jnp.dot(h.astype(BF16), w_ref[...], preferred_element_type=F32)
    cos, sin = cos_ref[...], sin_ref[...]
    roped = [_rope128(r[:, i * LANES:(i + 1) * LANES], cos, sin) for i in range(6)]
    qa_ref[0] = (jnp.concatenate(roped[0:4], axis=1) * LOG2E).astype(BF16)
    kva_ref[0] = jnp.concatenate(roped[4:6] + [r[:, 768:1024]], axis=1).astype(BF16)
    qb_ref[0] = (r[:, 1024:1536] * LOG2E).astype(BF16)
    kb_ref[0] = r[:, 1536:2048].astype(BF16)
    vb_ref[0] = r[:, 2048:2560].astype(BF16)


def _proj0(ctx, x, mods, g_mix, w_in, cos, sin):
    b, seq, d = x.shape
    u = CTX_LEN + seq
    tm = TOK_TILE
    s = 1.0 / 8.0
    w = jnp.concatenate([w_in[:, 0:512] * s, _dup_heads(w_in[:, 512:640], 2), _dup_heads(w_in[:, 640:768], 2),
                         w_in[:, 768:1280] * s, w_in[:, 1280:1792], w_in[:, 1792:2304]], axis=1).astype(BF16)
    nw = w.shape[1]
    tok = lambda n: pl.BlockSpec((1, tm, n), lambda i, t: (i, t, 0))
    return pl.pallas_call(
        _proj0_kernel,
        grid=(b, u // tm),
        in_specs=[pl.BlockSpec((1, tm, d), lambda i, t: (i, 0, 0)),
                  pl.BlockSpec((1, tm, d), lambda i, t: (i, jnp.maximum(t - 1, 0), 0)),
                  pl.BlockSpec((1, 1, 6, d), lambda i, t: (i, jnp.minimum(t, 1), 0, 0)),
                  pl.BlockSpec((1, d), lambda i, t: (0, 0)),
                  pl.BlockSpec((d, nw), lambda i, t: (0, 0)),
                  pl.BlockSpec((tm, LANES), lambda i, t: (t, 0)),
                  pl.BlockSpec((tm, LANES), lambda i, t: (t, 0))],
        out_specs=[tok(512)] * 5,
        out_shape=[jax.ShapeDtypeStruct((b, u, 512), BF16)] * 5,
        compiler_params=_cparams(("arbitrary", "arbitrary")),
        name="proj0",
    )(ctx, x, mods, g_mix.reshape(1, d), w, cos, sin)


def _window_starts(n_tiles, tq, span, u):
    return np.clip(np.arange(n_tiles) * tq - A_WINDOW, CTX_LEN, u - span)


def _window_mask_table(u, tq, span):
    n_tiles = u // tq
    starts = _window_starts(n_tiles, tq, span, u)
    variants, ids = [], []
    for n in range(n_tiles):
        qpos = n * tq + np.arange(tq)[:, None]
        kpos = starts[n] + np.arange(span)[None, :]
        ok = (kpos >= CTX_LEN) & (np.abs(qpos - kpos) <= A_WINDOW) & (qpos >= CTX_LEN)
        key = ok.tobytes()
        if key not in [v[0] for v in variants]:
            variants.append((key, ok))
        ids.append([v[0] for v in variants].index(key))
    local = np.stack([np.where(v[1], 0.0, NEG_INF) for v in variants]).astype(np.float32)
    tab = np.concatenate([np.zeros(local.shape[:2] + (CTX_LEN,), np.float32), local], axis=-1)
    return jnp.asarray(tab), jnp.asarray(ids, I32)


def _window_kernel(var_ref, sink_ref, q_ref, kv_ref, mask_ref, o_ref, *, tq, span):
    u = kv_ref.shape[1]
    for sub in range(WINDOW_STEP_TILES):
        n = pl.program_id(1) * WINDOW_STEP_TILES + sub
        ls = pl.multiple_of(jnp.clip(n * tq - A_WINDOW, CTX_LEN, u - span), A_WINDOW)
        mask = mask_ref[var_ref[n]]
        kv_all = jnp.concatenate([kv_ref[0, 0:CTX_LEN, :], kv_ref[0, pl.ds(ls, span), :]], axis=0)
        outs = [None] * 8
        for g in range(2):
            q4 = jnp.concatenate(
                [_keep_half(q_ref[0, sub * tq:(sub + 1) * tq, ((4 * g + hh) // 2) * LANES:((4 * g + hh) // 2 + 1) * LANES],
                            hh % 2) for hh in range(4)], axis=0)
            s = _qk(q4, kv_all[:, g * LANES:(g + 1) * LANES])
            s = jnp.concatenate([s[hh * tq:(hh + 1) * tq] + mask for hh in range(4)], axis=0)
            o = _softmax_pv(s, kv_all[:, 256 + g * LANES:256 + (g + 1) * LANES],
                            [sink_ref[4 * g + hh] for hh in range(4)])
            for hh in range(4):
                outs[4 * g + hh] = o[hh * tq:(hh + 1) * tq]
        o_ref[0, sub * tq:(sub + 1) * tq, :] = jnp.concatenate(
            [_merge_halves(outs[2 * p], outs[2 * p + 1]) for p in range(4)], axis=1).astype(BF16)


def _window_attention(qa, kva, sink):
    b, u, _ = qa.shape
    tq = WINDOW_TILE
    span = tq + 2 * A_WINDOW
    assert CTX_LEN % tq == 0 and (u - span) % A_WINDOW == 0
    mask_tab, variant = _window_mask_table(u, tq, span)
    rows = tq * WINDOW_STEP_TILES
    assert u % rows == 0
    return pl.pallas_call(
        functools.partial(_window_kernel, tq=tq, span=span),
        grid=(b, u // rows),
        in_specs=[pl.BlockSpec(memory_space=pltpu.SMEM),
                  pl.BlockSpec(memory_space=pltpu.SMEM),
                  pl.BlockSpec((1, rows, 512), lambda i, n: (i, n, 0)),
                  pl.BlockSpec((1, u, 512), lambda i, n: (i, 0, 0)),
                  pl.BlockSpec(mask_tab.shape, lambda i, n: (0, 0, 0))],
        out_specs=pl.BlockSpec((1, rows, 512), lambda i, n: (i, n, 0)),
        out_shape=jax.ShapeDtypeStruct((b, u, 512), BF16),
        compiler_params=_cparams(("arbitrary", "arbitrary")),
        name="window_attn",
    )(variant, sink.astype(F32) * LOG2E, qa, kva, mask_tab)


def _na_kernel(var_ref, q_ref, k_ref, v_ref, bias_ref, o_ref, *, rows):
    j = pl.program_id(1)
    rs0 = jnp.clip((j - 1) * NA_QROWS - NA_ROWS // 2, 0, rows - NA_BAND)
    band = NA_BAND * GRID_W
    start = pl.multiple_of(CTX_LEN + rs0 * GRID_W, GRID_W)
    k_all = jnp.concatenate([k_ref[0, 0:CTX_LEN, :], k_ref[0, pl.ds(start, band), :]], axis=0)
    v_all = jnp.concatenate([v_ref[0, 0:CTX_LEN, :], v_ref[0, pl.ds(start, band), :]], axis=0)
    tq = q_ref.shape[1]
    outs = []
    for p in range(4):
        q128 = q_ref[0, :, p * LANES:(p + 1) * LANES]
        q2 = jnp.concatenate([_keep_half(q128, 0), _keep_half(q128, 1)], axis=0)
        s = _qk(q2, k_all[:, p * LANES:(p + 1) * LANES])
        s = s + jnp.concatenate([bias_ref[0, 2 * p], bias_ref[0, 2 * p + 1]], axis=0)
        o = _softmax_pv(s, v_all[:, p * LANES:(p + 1) * LANES])
        outs.append(_merge_halves(o[:tq], o[tq:]))
    o_ref[0] = jnp.concatenate(outs, axis=1).astype(BF16)


def _na_attention(qb, kb, vb, rpb):
    b, u, _ = qb.shape
    rows = (u - CTX_LEN) // GRID_W
    assert rows >= NA_BAND and rows % NA_QROWS == 0
    bias_tab, variant = _na_bias_table(rpb, rows)
    tq = NA_QROWS * GRID_W
    assert tq == CTX_LEN
    nk = bias_tab.shape[-1]
    grid_spec = pltpu.PrefetchScalarGridSpec(
        num_scalar_prefetch=1,
        grid=(b, u // tq),
        in_specs=[pl.BlockSpec((1, tq, 512), lambda i, j, var: (i, j, 0)),
                  pl.BlockSpec((1, u, 512), lambda i, j, var: (i, 0, 0)),
                  pl.BlockSpec((1, u, 512), lambda i, j, var: (i, 0, 0)),
                  pl.BlockSpec((1, 8, tq, nk), lambda i, j, var: (var[j], 0, 0, 0))],
        out_specs=pl.BlockSpec((1, tq, 512), lambda i, j, var: (i, j, 0)),
    )
    return pl.pallas_call(
        functools.partial(_na_kernel, rows=rows),
        grid_spec=grid_spec,
        out_shape=jax.ShapeDtypeStruct((b, u, 512), BF16),
        compiler_params=_cparams(("arbitrary", "arbitrary")),
        name="na_attn",
    )(variant, qb, kb, vb, bias_tab)


def _proj1_kernel(x_ref, mod_ref, g_ref, w_ref, gq_ref, wuq_ref, gkv_ref, wkv_ref, gqn_ref, gkn_ref, ones_ref,
                  cosa_ref, sina_ref, cosq_ref, sinq_ref, cosk_ref, sink_ref,
                  qm_ref, qd_ref, km_ref, vm_ref, kvd_ref):
    h = _norm_mod(x_ref[0], g_ref[...], mod_ref[0, 0, 0:1, :], mod_ref[0, 0, 1:2, :])
    r = jnp.dot(h.astype(BF16), w_ref[...], preferred_element_type=F32)
    ones_bd = ones_ref[...]
    cosa, sina = cosa_ref[...], sina_ref[...]
    cq = _rms(r[:, 0:C_Q_LORA], gq_ref[...])
    qm = jnp.dot(cq.astype(BF16), wuq_ref[...], preferred_element_type=F32)
    cosq, sinq = cosq_ref[...], sinq_ref[...]
    qm_ref[0] = jnp.concatenate([_rope128(qm[:, i * LANES:(i + 1) * LANES], cosq, sinq) for i in range(8)],
                                axis=1).astype(BF16)
    gqn = gqn_ref[...]
    qd_ref[0] = jnp.concatenate(
        [_rope128(_head_norm(r[:, 384 + i * LANES:384 + (i + 1) * LANES], ones_bd, gqn), cosa, sina)
         for i in range(4)], axis=1).astype(BF16)
    ckv = _rms(r[:, 896:1152], gkv_ref[...])
    kr = _rope128(r[:, 1152:1280], cosk_ref[...], sink_ref[...])
    kv = jnp.dot(jnp.concatenate([ckv, kr], axis=1).astype(BF16), wkv_ref[...], preferred_element_type=F32)
    km_ref[0] = kv[:, 0:1024].astype(BF16)
    vm_ref[0] = kv[:, 1024:1536].astype(BF16)
    gkn = gkn_ref[...]
    kd = [_rope128(_head_norm(r[:, 1280 + i * LANES:1280 + (i + 1) * LANES], ones_bd, gkn), cosa, sina)
          for i in range(2)]
    kvd_ref[0] = jnp.concatenate(kd + [r[:, 1536:1792]], axis=1).astype(BF16)


def _proj1(xa, mods, g_mix, w_in, g_q_lora, w_uq, g_kv_lora, w_ukv, g_qn, g_kn, tabs):
    b, u, d = xa.shape
    tm = TOK_TILE
    zpad = jnp.zeros((d, LANES - C_ROPE), F32)
    w = jnp.concatenate([w_in[:, 0:896], w_in[:, 896:1152], w_in[:, 1152:1184], zpad,
                         _dup_heads(w_in[:, 1184:1312], 2), _dup_heads(w_in[:, 1312:1440], 2)], axis=1).astype(BF16)
    nw = w.shape[1]
    wuq = w_uq.reshape(C_Q_LORA, 8, C_NOPE + C_ROPE)
    wuq = jnp.concatenate([wuq, jnp.zeros((C_Q_LORA, 8, LANES - C_NOPE - C_ROPE), F32)], axis=-1)
    wuq = wuq.reshape(C_Q_LORA, 8 * LANES).astype(BF16)
    wukv = w_ukv.reshape(C_KV_LORA, 8, C_NOPE + C_V)
    wk = jnp.concatenate([wukv[:, :, :C_NOPE], jnp.zeros((C_KV_LORA, 8, LANES - C_NOPE), F32)], axis=-1)
    wk = wk.reshape(C_KV_LORA, 8 * LANES)
    wv = wukv[:, :, C_NOPE:].reshape(C_KV_LORA, 8 * C_V)
    place = np.zeros((LANES, 8, LANES), np.float32)
    for j in range(C_ROPE):
        place[j, :, C_NOPE + j] = 1.0
    place = jnp.asarray(place.reshape(LANES, 8 * LANES))
    wkv = jnp.concatenate([jnp.concatenate([wk, wv], axis=1),
                           jnp.concatenate([place, jnp.zeros((LANES, 8 * C_V), F32)], axis=1)], axis=0).astype(BF16)
    ones_bd = jnp.asarray(np.kron(np.eye(2, dtype=np.float32), np.ones((HEAD_DIM, HEAD_DIM), np.float32))).astype(BF16)
    gqn = jnp.tile(g_qn.astype(F32) * (LOG2E / 8.0), 2).reshape(1, LANES)
    gkn = jnp.tile(g_kn.astype(F32), 2).reshape(1, LANES)
    cosa, sina, cosq, sinq, cosk, sink = tabs
    tok = lambda n: pl.BlockSpec((1, tm, n), lambda i, t: (i, t, 0))
    lat = lambda n: pl.BlockSpec((1, tm, n), lambda i, t: (i, jnp.maximum(t - 1, 0), 0))
    whole = lambda a: pl.BlockSpec(a.shape, lambda i, t: (0,) * a.ndim)
    tab = pl.BlockSpec((tm, LANES), lambda i, t: (t, 0))
    small = [g_mix.reshape(1, d), w, g_q_lora.reshape(1, -1).astype(F32), wuq, g_kv_lora.reshape(1, -1).astype(F32),
             wkv, gqn, gkn, ones_bd]
    return pl.pallas_call(
        _proj1_kernel,
        grid=(b, u // tm),
        in_specs=[tok(d), pl.BlockSpec((1, 1, 6, d), lambda i, t: (i, jnp.minimum(t, 1), 0, 0))]
                 + [whole(a) for a in small] + [tab] * 6,
        out_specs=[lat(1024), lat(512), tok(1024), tok(512), tok(512)],
        out_shape=[jax.ShapeDtypeStruct((b, u - CTX_LEN, n), BF16) for n in (1024, 512)]
                  + [jax.ShapeDtypeStruct((b, u, n), BF16) for n in (1024, 512, 512)],
        compiler_params=_cparams(("arbitrary", "arbitrary")),
        name="proj1",
    )(xa, mods, *small, cosa, sina, cosq, sinq, cosk, sink)


def _mla_kernel(q_ref, k_ref, v_ref, o_ref):
    outs = []
    for h in range(8):
        s = _qk(q_ref[0, :, h * LANES:(h + 1) * LANES], k_ref[0, :, h * LANES:(h + 1) * LANES])
        outs.append(_softmax_pv(s, v_ref[0, :, (h // 2) * LANES:(h // 2 + 1) * LANES]))
    o_ref[0] = jnp.concatenate([_merge_halves(outs[2 * p], outs[2 * p + 1]) for p in range(4)], axis=1).astype(BF16)


def _mla_attention(qm, km, vm):
    b, seq, _ = qm.shape
    u = km.shape[1]
    tq = ATTN_TILE
    return pl.pallas_call(
        _mla_kernel,
        grid=(b, seq // tq),
        in_specs=[pl.BlockSpec((1, tq, 1024), lambda i, n: (i, n, 0)),
                  pl.BlockSpec((1, u, 1024), lambda i, n: (i, 0, 0)),
                  pl.BlockSpec((1, u, 512), lambda i, n: (i, 0, 0))],
        out_specs=pl.BlockSpec((1, tq, 512), lambda i, n: (i, n, 0)),
        out_shape=jax.ShapeDtypeStruct((b, seq, 512), BF16),
        compiler_params=_cparams(("arbitrary", "arbitrary")),
        name="mla_attn",
    )(qm, km, vm)


def _gqa_kernel(q_ref, kv_ref, o_ref):
    outs = [None] * 8
    for g in range(2):
        k = kv_ref[0, :, g * LANES:(g + 1) * LANES]
        v = kv_ref[0, :, 256 + g * LANES:256 + (g + 1) * LANES]
        for hh in range(4):
            h = 4 * g + hh
            q = _keep_half(q_ref[0, :, (h // 2) * LANES:(h // 2 + 1) * LANES], h % 2)
            outs[h] = _softmax_pv(_qk(q, k), v)
    o_ref[0] = jnp.concatenate([_merge_halves(outs[2 * p], outs[2 * p + 1]) for p in range(4)], axis=1).astype(BF16)


def _gqa_attention(qd, kvd):
    b, seq, _ = qd.shape
    u = kvd.shape[1]
    tq = ATTN_TILE
    return pl.pallas_call(
        _gqa_kernel,
        grid=(b, seq // tq),
        in_specs=[pl.BlockSpec((1, tq, 512), lambda i, n: (i, n, 0)),
                  pl.BlockSpec((1, u, 512), lambda i, n: (i, 0, 0))],
        out_specs=pl.BlockSpec((1, tq, 512), lambda i, n: (i, n, 0)),
        out_shape=jax.ShapeDtypeStruct((b, seq, 512), BF16),
        compiler_params=_cparams(("arbitrary", "arbitrary")),
        name="gqa_attn",
    )(qd, kvd)


def _post_attn_kernel(ya_ref, yb_ref, w_ref, c_ref, x_ref, mod_ref, g_ref, wr_ref, br_ref,
                      xmid_ref, h2_ref, slot_ref, wts_ref, slott_ref, gtab_ref, tot_ref, run_ref, *, ctx_first):
    first = (pl.program_id(0) == 0) & (pl.program_id(1) == 0)
    x_in = jnp.where(pl.program_id(1) == 0, c_ref[0], x_ref[0]) if ctx_first else x_ref[0]

    @pl.when(first)
    def _():
        run_ref[...] = jnp.zeros_like(run_ref)

    y = (jnp.dot(ya_ref[0], w_ref[0:512, :], preferred_element_type=F32)
         + jnp.dot(yb_ref[0], w_ref[512:1024, :], preferred_element_type=F32))
    x1 = x_in + mod_ref[0, 0, 2:3, :] * y
    xmid_ref[0] = x1
    h2 = _norm_mod(x1, g_ref[...], mod_ref[0, 0, 3:4, :], mod_ref[0, 0, 4:5, :])
    h_hi = h2.astype(BF16)
    h2_ref[...] = h_hi

    h_lo = (h2 - h_hi.astype(F32)).astype(BF16)
    lhs = jnp.concatenate([h_hi, h_hi, h_lo], axis=1)
    scores = jax.nn.sigmoid(_qk(wr_ref[...], lhs))
    tm = scores.shape[1]
    e_id = lax.broadcasted_iota(I32, (LANES, tm), 0).astype(F32)
    biased = jnp.where(e_id < N_EXPERTS, scores + br_ref[...], -jnp.inf)
    picked = jnp.zeros_like(scores)
    sel_idx, sel_val = [], []
    for _k in range(TOP_K):
        m = jnp.max(biased, axis=0, keepdims=True)
        i_k = jnp.min(jnp.where(biased == m, e_id, float(LANES)), axis=0, keepdims=True)
        hit = e_id == i_k
        sel_idx.append(i_k)
        sel_val.append(jnp.sum(jnp.where(hit, scores, 0.0), axis=0, keepdims=True))
        picked = jnp.where(hit, 1.0, picked)
        biased = jnp.where(hit, -jnp.inf, biased)
    total = sel_val[0]
    for v in sel_val[1:]:
        total = total + v
    cnt = jnp.sum(picked, axis=1, keepdims=True)
    ng = jnp.floor((cnt + float(GRAN - 1)) * (1.0 / GRAN))
    e_i = lax.broadcasted_iota(I32, (LANES, LANES), 0)
    e_j = lax.broadcasted_iota(I32, (LANES, LANES), 1)
    lower = jnp.where(e_j < e_i, 1.0, 0.0).astype(BF16)
    loffg = jnp.dot(lower, jnp.broadcast_to(ng, (LANES, LANES)).astype(BF16), preferred_element_type=F32)[:, 0:1]
    t_i = lax.broadcasted_iota(I32, (tm, tm), 0)
    t_j = lax.broadcasted_iota(I32, (tm, tm), 1)
    before = jnp.dot(picked.astype(BF16), jnp.where(t_i < t_j, 1.0, 0.0).astype(BF16), preferred_element_type=F32)
    base = before + float(GRAN) * loffg
    slot_t = jnp.zeros((LANES, tm), F32)
    wts_t = jnp.zeros((LANES, tm), F32)
    for k in range(TOP_K):
        slot_k = jnp.sum(jnp.where(e_id == sel_idx[k], base, 0.0), axis=0, keepdims=True)
        slot_t = jnp.where(e_id == float(k), slot_k, slot_t)
        wts_t = jnp.where(e_id == float(k), ROUTED_SCALE * sel_val[k] / total, wts_t)
    slott_ref[0] = slot_t[:KPAD, :].astype(I32)
    slot_ref[...] = slot_t.T[:, :KPAD].astype(I32)
    wts_ref[...] = wts_t.T[:, :KPAD]

    run = run_ref[...]
    end_c = loffg + ng
    val_c = run - loffg
    jj = lax.broadcasted_iota(I32, (LANES, NGRAN), 1).astype(F32)
    ee = lax.broadcasted_iota(I32, (LANES, NGRAN), 0).astype(F32)
    e_of_j = jnp.sum(jnp.where(end_c <= jj, 1.0, 0.0), axis=0, keepdims=True)
    dst = jnp.sum(jnp.where(ee == e_of_j, val_c + jj, 0.0), axis=0, keepdims=True)
    used = jnp.broadcast_to(jnp.sum(ng, axis=0, keepdims=True), (1, NGRAN))
    gtab_ref[0] = jnp.concatenate([e_of_j, dst, used, jnp.zeros((5, NGRAN), F32)], axis=0).astype(I32)
    new_run = run + ng
    run_ref[...] = new_run
    tot_ref[...] = new_run


def _post_attn(ya, yb, w_out, ctx, x, x_tile_off, ctx_first, mods, mod_sel, g_moe, w_router, b_router):
    b, n, _ = ya.shape
    d = D_MODEL
    tm = TOK_TILE
    nt = n // tm
    t_tot = b * n
    wr = jnp.zeros((d, LANES), F32).at[:, :N_EXPERTS].set(w_router.astype(F32))
    wr_hi = wr.astype(BF16)
    wr_lo = (wr - wr_hi.astype(F32)).astype(BF16)
    wr = jnp.concatenate([wr_hi, wr_lo, wr_hi], axis=0).T
    br = jnp.zeros((LANES, 1), F32).at[:N_EXPERTS, 0].set(b_router.astype(F32))
    flat = lambda c: pl.BlockSpec((tm, c), lambda i, t: (i * nt + t, 0))
    outs = pl.pallas_call(
        functools.partial(_post_attn_kernel, ctx_first=ctx_first),
        grid=(b, nt),
        in_specs=[pl.BlockSpec((1, tm, 512), lambda i, t: (i, t, 0)),
                  pl.BlockSpec((1, tm, 512), lambda i, t: (i, t, 0)),
                  pl.BlockSpec((d, d), lambda i, t: (0, 0)),
                  pl.BlockSpec((1, tm, d), lambda i, t: (i, 0, 0)),
                  pl.BlockSpec((1, tm, d), lambda i, t: (i, jnp.maximum(t + x_tile_off, 0), 0)),
                  pl.BlockSpec((1, 1, 6, d), lambda i, t: (i, mod_sel(t), 0, 0)),
                  pl.BlockSpec((1, d), lambda i, t: (0, 0)),
                  pl.BlockSpec((LANES, 3 * d), lambda i, t: (0, 0)),
                  pl.BlockSpec((LANES, 1), lambda i, t: (0, 0))],
        out_specs=[pl.BlockSpec((1, tm, d), lambda i, t: (i, t, 0)),
                   flat(d), flat(KPAD), flat(KPAD),
                   pl.BlockSpec((1, KPAD, tm), lambda i, t: (i * nt + t, 0, 0)),
                   pl.BlockSpec((1, 8, NGRAN), lambda i, t: (i * nt + t, 0, 0)),
                   pl.BlockSpec((LANES, 1), lambda i, t: (0, 0))],
        out_shape=[jax.ShapeDtypeStruct((b, n, d), F32),
                   jax.ShapeDtypeStruct((t_tot, d), BF16),
                   jax.ShapeDtypeStruct((t_tot, KPAD), I32),
                   jax.ShapeDtypeStruct((t_tot, KPAD), F32),
                   jax.ShapeDtypeStruct((b * nt, KPAD, tm), I32),
                   jax.ShapeDtypeStruct((b * nt, 8, NGRAN), I32),
                   jax.ShapeDtypeStruct((LANES, 1), F32)],
        scratch_shapes=[pltpu.VMEM((LANES, 1), F32)],
        compiler_params=_cparams(("arbitrary", "arbitrary")),
        name="post_attn_route",
    )(ya, yb, w_out.astype(BF16), ctx, x, mods, g_moe.reshape(1, d), wr, br)
    return outs


def _granule_copy(src, src_g, dst, dst_g, sem):
    return pltpu.make_async_copy(src.at[pl.ds(pl.multiple_of(src_g * GRAN, GRAN), GRAN), :],
                                 dst.at[pl.ds(pl.multiple_of(dst_g * GRAN, GRAN), GRAN), :], sem)


def _for_each(lo, hi, body):
    trips = lax.shift_right_logical(hi - lo, jnp.int32(ISSUE_UNROLL.bit_length() - 1))

    def block(b, carry):
        for u in range(ISSUE_UNROLL):
            body(lo + b * ISSUE_UNROLL + u)
        return carry

    def single(j, carry):
        body(j)
        return carry

    lax.fori_loop(0, trips, block, 0)
    lax.fori_loop(lo + trips * ISSUE_UNROLL, hi, single, 0)


def _drain(src, dst, sem, n):
    rows = WAIT_BATCH * GRAN
    batches = lax.shift_right_logical(n, jnp.int32(WAIT_BATCH.bit_length() - 1))

    def big(j, carry):
        pltpu.make_async_copy(src.at[pl.ds(0, rows), :], dst.at[pl.ds(0, rows), :], sem).wait()
        return carry

    def small(j, carry):
        _granule_copy(src, 0, dst, 0, sem).wait()
        return carry

    lax.fori_loop(0, batches, big, 0)
    lax.fori_loop(0, n - batches * WAIT_BATCH, small, 0)


def _dispatch_kernel(fill_ref, gtab_ref, h_ref, slott_ref, xs_ref, xloc, gprev, sems, *, nt):
    i = pl.program_id(0)
    cur = i % 2
    used = gtab_ref[0, 2, 0]
    slott = slott_ref[0]
    h = h_ref[...]
    tm = h.shape[0]
    def sort_rows(r0, n):
        rows = r0 + lax.broadcasted_iota(I32, (n, tm), 0)
        p = jnp.zeros((n, tm), F32)
        for k in range(TOP_K):
            p = jnp.where(rows == slott[k:k + 1, :], 1.0, p)
        xloc[cur, r0:r0 + n, :] = jnp.dot(p.astype(BF16), h, preferred_element_type=F32)

    def send(j):
        _granule_copy(xloc.at[cur], j, xs_ref, gtab_ref[0, 0, j], sems.at[cur]).start()

    for r0 in range(0, SORT_ROWS, CHUNK):
        if r0 + CHUNK <= ALWAYS_ROWS:
            sort_rows(r0, CHUNK)
            for j in range(r0 // GRAN, (r0 + CHUNK) // GRAN):
                send(j)
        else:
            pl.when(r0 < used * GRAN)(functools.partial(sort_rows, r0, CHUNK))
    _for_each(ALWAYS_ROWS // GRAN, used, send)

    @pl.when(i > 0)
    def _():
        _drain(xloc.at[1 - cur], xs_ref, sems.at[1 - cur], gprev[0])

    gprev[0] = used

    @pl.when(i == nt - 1)
    def _():
        _drain(xloc.at[cur], xs_ref, sems.at[cur], used)
        zeros = xloc.at[1 - cur]
        zeros[0:SLOT_TILE, :] = jnp.zeros((SLOT_TILE, xloc.shape[2]), F32)

        def fill_expert(e, count):
            lo, hi = fill_ref[0, e], fill_ref[1, e]
            lax.fori_loop(lo, hi, lambda g, c: (_granule_copy(zeros, 0, xs_ref, g, sems.at[0]).start(), c)[1], 0)
            return count + (hi - lo)

        _drain(zeros, xs_ref, sems.at[0], lax.fori_loop(0, N_EXPERTS, fill_expert, jnp.int32(0)))

        def tile_copy(t):
            return pltpu.make_async_copy(zeros.at[pl.ds(0, SLOT_TILE), :],
                                         xs_ref.at[pl.ds(pl.multiple_of(t * SLOT_TILE, SLOT_TILE), SLOT_TILE), :],
                                         sems.at[1])

        first_free, n_tiles = fill_ref[2, 0], xs_ref.shape[0] // SLOT_TILE
        lax.fori_loop(first_free, n_tiles, lambda t, c: (tile_copy(t).start(), c)[1], 0)
        lax.fori_loop(first_free, n_tiles, lambda t, c: (tile_copy(0).wait(), c)[1], 0)


def _dispatch(h2, slott, gtab, fill, n_slots):
    t_tot, d = h2.shape
    tm = TOK_TILE
    nt = t_tot // tm
    return pl.pallas_call(
        functools.partial(_dispatch_kernel, nt=nt),
        grid=(nt,),
        in_specs=[pl.BlockSpec(memory_space=pltpu.SMEM),
                  pl.BlockSpec((1, 8, NGRAN), lambda i: (i, 0, 0), memory_space=pltpu.SMEM),
                  pl.BlockSpec((tm, d), lambda i: (i, 0)),
                  pl.BlockSpec((1, KPAD, tm), lambda i: (i, 0, 0))],
        out_specs=pl.BlockSpec(memory_space=pl.ANY),
        out_shape=jax.ShapeDtypeStruct((n_slots, d), F32),
        scratch_shapes=[pltpu.VMEM((2, SORT_ROWS, d), F32), pltpu.SMEM((1,), I32),
                        pltpu.SemaphoreType.DMA((2,))],
        compiler_params=_cparams(("arbitrary",)),
        name="moe_dispatch",
    )(fill, gtab, h2, slott)


def _expert_kernel(te_ref, nused_ref, xs_ref, wg_ref, wu_ref, wd_ref, ys_ref, wgu_s, wd_s):
    i = pl.program_id(0)
    changed = (i == 0) | (te_ref[i] != te_ref[jnp.maximum(i - 1, 0)])

    @pl.when(changed)
    def _():
        wgu_s[:, 0:EXPERT_FF] = wg_ref[0].astype(BF16)
        wgu_s[:, EXPERT_FF:2 * EXPERT_FF] = wu_ref[0].astype(BF16)
        wd_s[...] = wd_ref[0].astype(BF16)

    @pl.when(i < nused_ref[0])
    def _():
        gu = jnp.dot(xs_ref[...].astype(BF16), wgu_s[...], preferred_element_type=F32)
        gate, up = gu[:, :EXPERT_FF], gu[:, EXPERT_FF:]
        act = gate * jax.nn.sigmoid(gate) * up
        ys_ref[...] = jnp.dot(act.astype(BF16), wd_s[...], preferred_element_type=F32)

    @pl.when(i >= nused_ref[0])
    def _():
        ys_ref[...] = jnp.zeros_like(ys_ref)


def _expert_ffn(xs, tile_expert, n_used, w_gate, w_up, w_down):
    n_slots, w = xs.shape
    ts = SLOT_TILE
    d, f = w_gate.shape[1], w_gate.shape[2]
    grid_spec = pltpu.PrefetchScalarGridSpec(
        num_scalar_prefetch=2,
        grid=(n_slots // ts,),
        in_specs=[pl.BlockSpec((ts, w), lambda i, te, nu: (jnp.minimum(i, nu[0] - 1), 0)),
                  pl.BlockSpec((1, d, f), lambda i, te, nu: (te[i], 0, 0)),
                  pl.BlockSpec((1, d, f), lambda i, te, nu: (te[i], 0, 0)),
                  pl.BlockSpec((1, f, d), lambda i, te, nu: (te[i], 0, 0))],
        out_specs=pl.BlockSpec((ts, w), lambda i, te, nu: (i, 0)),
        scratch_shapes=[pltpu.VMEM((d, 2 * f), BF16), pltpu.VMEM((f, d), BF16)],
    )
    return pl.pallas_call(
        _expert_kernel,
        grid_spec=grid_spec,
        out_shape=jax.ShapeDtypeStruct((n_slots, w), F32),
        compiler_params=_cparams(("arbitrary",)),
        name="moe_experts",
    )(tile_expert, n_used, xs, w_gate, w_up, w_down)


def _combine_kernel(gtab_ref, gnext_ref, ys_ref, slot_ref, wts_ref, h_ref, x_ref, mod_ref, wsgu_ref,
                    wsd_ref, gf_ref, o_ref, yloc, acc_ref, sems, *, nt, final_norm):
    i = pl.program_id(0)
    cur = i % 2

    @pl.when(i == 0)
    def _():
        _for_each(0, gtab_ref[0, 2, 0], lambda j: _granule_copy(ys_ref, gtab_ref[0, 0, j], yloc, j, sems.at[0]).start())

    n_always = ALWAYS_ROWS // CHUNK
    inline = ALWAYS_ROWS // GRAN
    pieces = [inline * p // (n_always + 1) for p in range(n_always + 2)]

    def fetch_next(j):
        _granule_copy(ys_ref, gnext_ref[0, 0, j], yloc, (1 - cur) * NGRAN + j, sems.at[1 - cur]).start()

    def fetch_piece(p):
        for j in range(pieces[p], pieces[p + 1]):
            fetch_next(j)

    fetch_piece(0)
    h = h_ref[...]
    tm = h.shape[0]
    gu = jnp.dot(h, wsgu_ref[...], preferred_element_type=F32)
    gate, up = gu[:, :SHARED_FF], gu[:, SHARED_FF:]
    shared = jnp.dot((gate * jax.nn.sigmoid(gate) * up).astype(BF16), wsd_ref[...], preferred_element_type=F32)

    used = gtab_ref[0, 2, 0]
    _drain(ys_ref, yloc, sems.at[cur], used)
    row0 = cur * SORT_ROWS

    slot_b = [jnp.broadcast_to(slot_ref[:, k:k + 1], (tm, LANES)) for k in range(TOP_K)]
    wts_b = [jnp.broadcast_to(wts_ref[:, k:k + 1], (tm, LANES)) for k in range(TOP_K)]
    lane = lax.broadcasted_iota(I32, (tm, LANES), 1)
    w = yloc.shape[1]

    def weights(c):
        blocks = []
        for j in range(CHUNK // LANES):
            cols = lane + (c * CHUNK + j * LANES)
            pw = jnp.zeros((tm, LANES), F32)
            for k in range(TOP_K):
                pw = jnp.where(cols == slot_b[k], wts_b[k], pw)
            blocks.append(pw.astype(BF16))
        return jnp.concatenate(blocks, axis=1)

    def values(c, masked):
        y = yloc[pl.ds(pl.multiple_of(row0 + c * CHUNK, CHUNK), CHUNK), :]
        if masked:
            rows = c * CHUNK + lax.broadcasted_iota(I32, (CHUNK, w), 0)
            y = jnp.where(rows < used * GRAN, y, 0.0)
        return y.astype(BF16)

    acc = shared
    for c in range(n_always):
        acc = acc + jnp.dot(weights(c), values(c, False), preferred_element_type=F32)
        fetch_piece(c + 1)
    acc_ref[...] = acc
    for c in range(n_always, SORT_ROWS // CHUNK):
        @pl.when(c * CHUNK < used * GRAN)
        def _():
            acc_ref[...] += jnp.dot(weights(c), values(c, True), preferred_element_type=F32)

    out = x_ref[...] + mod_ref[0, 0, 5:6, :] * acc_ref[...]
    if final_norm:
        out = _rms(out, gf_ref[...])
    o_ref[...] = out

    used_next = gnext_ref[0, 2, 0]
    _for_each(inline, used_next, fetch_next)

    @pl.when(i == nt - 1)
    def _():
        _drain(ys_ref, yloc, sems.at[1 - cur], used_next)


def _combine(ys, slot, wts, gtab, h2, xmid, mods, mod_map, ws_gate, ws_up, ws_down, g_final, final_norm):
    t_tot, d = h2.shape
    tm = TOK_TILE
    nt = t_tot // tm
    wsgu = jnp.concatenate([ws_gate, ws_up], axis=1).astype(BF16)
    return pl.pallas_call(
        functools.partial(_combine_kernel, nt=nt, final_norm=final_norm),
        grid=(nt,),
        in_specs=[pl.BlockSpec((1, 8, NGRAN), lambda i: (i, 0, 0), memory_space=pltpu.SMEM),
                  pl.BlockSpec((1, 8, NGRAN), lambda i: (jnp.minimum(i + 1, nt - 1), 0, 0), memory_space=pltpu.SMEM),
                  pl.BlockSpec(memory_space=pl.ANY),
                  pl.BlockSpec((tm, KPAD), lambda i: (i, 0)),
                  pl.BlockSpec((tm, KPAD), lambda i: (i, 0)),
                  pl.BlockSpec((tm, d), lambda i: (i, 0)),
                  pl.BlockSpec((tm, d), lambda i: (i, 0)),
                  pl.BlockSpec((1, 1, 6, d), mod_map),
                  pl.BlockSpec((d, 2 * SHARED_FF), lambda i: (0, 0)),
                  pl.BlockSpec((SHARED_FF, d), lambda i: (0, 0)),
                  pl.BlockSpec((1, d), lambda i: (0, 0))],
        out_specs=pl.BlockSpec((tm, d), lambda i: (i, 0)),
        out_shape=jax.ShapeDtypeStruct((t_tot, d), F32),
        scratch_shapes=[pltpu.VMEM((2 * SORT_ROWS, d), F32), pltpu.VMEM((tm, d), F32),
                        pltpu.SemaphoreType.DMA((2,))],
        compiler_params=_cparams(("arbitrary",)),
        name="moe_combine",
    )(gtab, gtab, ys, slot, wts, h2, xmid.reshape(t_tot, d), mods, wsgu, ws_down.astype(BF16),
      g_final.reshape(1, d).astype(F32))


def _moe(h2, slot, wts, slott, gtab, tot, xmid, mods, mod_map, moe_w, g_final, final_norm):
    (w_gate, w_up, w_down, ws_gate, ws_up, ws_down) = moe_w
    t_tot = h2.shape[0]
    ts = SLOT_TILE
    gpt = ts // GRAN
    nt = t_tot // TOK_TILE
    max_rows = t_tot * TOP_K + nt * N_EXPERTS * (GRAN - 1)
    n_tiles = -(-max_rows // ts) + N_EXPERTS
    totg = tot[:N_EXPERTS, 0].astype(I32)
    tiles_e = (totg + gpt - 1) // gpt
    ends = jnp.cumsum(tiles_e)
    poffg = jnp.zeros((LANES,), I32).at[:N_EXPERTS].set((ends - tiles_e) * gpt)
    n_used = ends[-1:]
    tile_ids = jnp.minimum(jnp.arange(n_tiles, dtype=I32), n_used[0] - 1)
    tile_expert = jnp.sum((ends[None, :] <= tile_ids[:, None]).astype(I32), axis=1)
    tile_expert = jnp.minimum(tile_expert, N_EXPERTS - 1)
    region = jnp.sum(jnp.where(gtab[:, 0, :, None] == jnp.arange(N_EXPERTS, dtype=I32), poffg[:N_EXPERTS], 0), axis=-1)
    gtab = gtab.at[:, 0, :].set(region + gtab[:, 1, :])
    fill = jnp.zeros((8, LANES), I32)
    fill = fill.at[0, :N_EXPERTS].set(poffg[:N_EXPERTS] + totg).at[1, :N_EXPERTS].set(ends * gpt).at[2, 0].set(ends[-1])
    xs = _dispatch(h2, slott, gtab, fill, n_tiles * ts)
    ys = _expert_ffn(xs, tile_expert, n_used.astype(I32), w_gate, w_up, w_down)
    return _combine(ys, slot, wts, gtab, h2, xmid, mods, mod_map, ws_gate, ws_up, ws_down, g_final, final_norm)


def kernel(x, c, ctx, c_ctx, l0_w_ada, l0_b_ada, l0_g_mix, l0_w_in, l0_sink, l0_rpb, l0_w_out, l0_g_moe, l0_w_router, l0_b_router, l0_w_gate, l0_w_up, l0_w_down, l0_ws_gate, l0_ws_up, l0_ws_down, l1_w_ada, l1_b_ada, l1_g_mix, l1_w_in, l1_g_q_lora, l1_w_uq, l1_g_kv_lora, l1_w_ukv, l1_g_qn, l1_g_kn, l1_w_out, l1_g_moe, l1_w_router, l1_b_router, l1_w_gate, l1_w_up, l1_w_down, l1_ws_gate, l1_ws_up, l1_ws_down, g_final):
    b, seq, d = x.shape
    assert d == D_MODEL and ctx.shape[1] == CTX_LEN and seq % TOK_TILE == 0
    u = CTX_LEN + seq
    tiles_u = u // TOK_TILE
    tiles_s = seq // TOK_TILE
    ctx, x = ctx.astype(F32), x.astype(F32)

    cos_a, sin_a = _rope_tables(seq, HEAD_DIM, 0, HEAD_DIM, 1.0)
    cos_q, sin_q = _rope_tables(seq, C_ROPE, C_NOPE, LANES, LOG2E * float((C_NOPE + C_ROPE) ** -0.5))
    cos_k, sin_k = _rope_tables(seq, C_ROPE, 0, LANES, 1.0)

    mods0 = _mods(c, c_ctx, l0_w_ada, l0_b_ada)
    qa, kva, qb, kb, vb = _proj0(ctx, x, mods0, l0_g_mix, l0_w_in, cos_a, sin_a)
    ya = _window_attention(qa, kva, l0_sink)
    yb = _na_attention(qb, kb, vb, l0_rpb)
    xmid, h2, slot, wts, slott, gtab, tot = _post_attn(ya, yb, l0_w_out, ctx, x, -1, True, mods0,
                                                   lambda t: jnp.minimum(t, 1), l0_g_moe, l0_w_router, l0_b_router)
    xa = _moe(h2, slot, wts, slott, gtab, tot, xmid, mods0,
              lambda i: (i // tiles_u, jnp.minimum(i % tiles_u, 1), 0, 0),
              (l0_w_gate, l0_w_up, l0_w_down, l0_ws_gate, l0_ws_up, l0_ws_down), g_final, False).reshape(b, u, d)

    mods1 = _mods(c, c_ctx, l1_w_ada, l1_b_ada)
    qm, qd, km, vm, kvd = _proj1(xa, mods1, l1_g_mix, l1_w_in, l1_g_q_lora, l1_w_uq, l1_g_kv_lora, l1_w_ukv,
                                 l1_g_qn, l1_g_kn, (cos_a, sin_a, cos_q, sin_q, cos_k, sin_k))
    ym = _mla_attention(qm, km, vm)
    yd = _gqa_attention(qd, kvd)
    xmid, h2, slot, wts, slott, gtab, tot = _post_attn(ym, yd, l1_w_out, xa, xa, CTX_LEN // TOK_TILE, False, mods1,
                                                   lambda t: 1, l1_g_moe, l1_w_router, l1_b_router)
    out = _moe(h2, slot, wts, slott, gtab, tot, xmid, mods1, lambda i: (i // tiles_s, 1, 0, 0),
               (l1_w_gate, l1_w_up, l1_w_down, l1_ws_gate, l1_ws_up, l1_ws_down), g_final, True)
    return out.reshape(b, seq, d)
```
